```python
import math
import jax, jax.numpy as jnp
from jax import lax
import numpy as np

D_MODEL = 2048
BATCH = 4
SEQ = 8192
DEPTH = 1
DEC_BATCH = 1
DEC_SEQ = 8192
PAST_LEN = 128

MIX_WIDTH = D_MODEL
ATT_WIDTH = MIX_WIDTH // 2
SSM_WIDTH = MIX_WIDTH - ATT_WIDTH
V_HEAD_DIM = 128
QK_NOPE_DIM = 128
QK_ROPE_DIM = 64
QK_HEAD_DIM = QK_NOPE_DIM + QK_ROPE_DIM
N_HEADS = ATT_WIDTH // V_HEAD_DIM
KV_LORA_RANK = 512
ROPE_BASE = 10000.0
Q_BLOCK = 128
SSM_GROUP = 16
N_SSM_GROUPS = SSM_WIDTH // SSM_GROUP
SSM_STATE = 64
DT_MIN = 1e-3
DT_MAX = 1e-1
D_FF = -(-8 * D_MODEL // (3 * 256)) * 256
Q_COLS = N_HEADS * QK_HEAD_DIM
IN_COLS = Q_COLS + KV_LORA_RANK + QK_ROPE_DIM + SSM_WIDTH
NORM_EPS = 1e-6

kernel_name = "hybrid_mla_s5_encoder_layer"


def rms_norm(x, g):
    xf = x.astype(jnp.float32)
    y = xf * lax.rsqrt(jnp.mean(xf * xf, axis=-1, keepdims=True) + NORM_EPS)
    return (y * g.astype(jnp.float32)).astype(x.dtype)


def rope_tables(length):
    pos = jnp.arange(length, dtype=jnp.float32)
    inv_freq = ROPE_BASE ** (-jnp.arange(0, QK_ROPE_DIM, 2, dtype=jnp.float32) / QK_ROPE_DIM)
    ang = pos[:, None] * inv_freq[None, :]
    return jnp.cos(ang), jnp.sin(ang)


def apply_rope(x, cos, sin):
    xf = x.astype(jnp.float32)
    x1, x2 = xf[..., : QK_ROPE_DIM // 2], xf[..., QK_ROPE_DIM // 2:]
    out = jnp.concatenate([x1 * cos - x2 * sin, x2 * cos + x1 * sin], axis=-1)
    return out.astype(x.dtype)


def dense_attention(q, k, v):
    b, h, length, dk = q.shape
    nb = length // Q_BLOCK
    scale = 1.0 / math.sqrt(dk)
    qb = q.reshape(b, h, nb, Q_BLOCK, dk).transpose(2, 0, 1, 3, 4)

    def one_block(q_blk):
        s = jnp.einsum("bhqd,bhkd->bhqk", q_blk, k, preferred_element_type=jnp.float32) * scale
        p = jax.nn.softmax(s, axis=-1)
        return jnp.einsum("bhqk,bhkd->bhqd", p.astype(v.dtype), v)

    o = lax.map(one_block, qb)
    return o.transpose(1, 0, 3, 2, 4).reshape(b, length, h * v.shape[-1])


def mla_mixer(q, c_kv, k_rope, kv_norm_g, w_ukv, q_norm_g, k_norm_g):
    b, length, _ = q.shape
    q = q.reshape(b, length, N_HEADS, QK_HEAD_DIM).transpose(0, 2, 1, 3)
    kv = rms_norm(c_kv, kv_norm_g) @ w_ukv
    kv = kv.reshape(b, length, N_HEADS, QK_NOPE_DIM + V_HEAD_DIM).transpose(0, 2, 1, 3)
    k_nope, v = kv[..., :QK_NOPE_DIM], kv[..., QK_NOPE_DIM:]
    k_r = jnp.broadcast_to(k_rope[:, None], (b, N_HEADS, length, QK_ROPE_DIM))
    k = jnp.concatenate([k_nope, k_r], axis=-1)
    q = rms_norm(q, q_norm_g)
    k = rms_norm(k, k_norm_g)
    cos, sin = rope_tables(length)
    q = jnp.concatenate([q[..., :QK_NOPE_DIM], apply_rope(q[..., QK_NOPE_DIM:], cos, sin)], axis=-1)
    k = jnp.concatenate([k[..., :QK_NOPE_DIM], apply_rope(k[..., QK_NOPE_DIM:], cos, sin)], axis=-1)
    return dense_attention(q, k, v)


def _scan_op(e1, e2):
    a1, b1 = e1
    a2, b2 = e2
    return a1 * a2, a2 * b1 + b2


def s5_direction(u, lam_re, lam_im, log_dt, b_re, b_im, c_re, c_im, reverse):
    lam = lax.complex(lam_re.astype(jnp.float32), lam_im.astype(jnp.float32))
    dt = jnp.exp(log_dt.astype(jnp.float32))[:, None]
    lam_bar = jnp.exp(lam * dt)
    b_mat = lax.complex(b_re.astype(jnp.float32), b_im.astype(jnp.float32))
    b_bar = ((lam_bar - 1.0) / lam)[..., None] * b_mat
    bu = jnp.einsum("gpc,lgc->lgp", b_bar, u.astype(jnp.complex64))
    a = jnp.broadcast_to(lam_bar, bu.shape)
    _, states = lax.associative_scan(_scan_op, (a, bu), axis=0, reverse=reverse)
    c_mat = lax.complex(c_re.astype(jnp.float32), c_im.astype(jnp.float32))
    return jnp.einsum("gcp,lgp->lgc", c_mat, states).real


def s5_mixer(u, lam_re, lam_im, log_dt, b_re, b_im, c_re, c_im, d_skip, w_glu, b_glu):
    b, length, _ = u.shape
    uf = u.astype(jnp.float32).reshape(b, length, N_SSM_GROUPS, SSM_GROUP)

    def one_sequence(us):
        fwd = s5_direction(us, lam_re[0], lam_im[0], log_dt[0], b_re, b_im,
                           c_re[0], c_im[0], reverse=False)
        bwd = s5_direction(us, lam_re[1], lam_im[1], log_dt[1], b_re, b_im,
                           c_re[1], c_im[1], reverse=True)
        return fwd + bwd

    y = lax.map(one_sequence, uf).reshape(b, length, SSM_WIDTH)
    y = y + d_skip.astype(jnp.float32) * uf.reshape(b, length, SSM_WIDTH)
    y = jax.nn.gelu(y).astype(u.dtype)
    return y * jax.nn.sigmoid(y @ w_glu + b_glu)


def encoder_layer(x, c, w_ada, b_ada, norm_mix_g, w_in, kv_norm_g, w_ukv, q_norm_g, k_norm_g,
                  lam_re, lam_im, log_dt, b_re, b_im, c_re, c_im, d_skip, w_glu, b_glu,
                  att_out_g, ssm_out_g, w_o, norm_ffn_g, w1, w3, w2):
    mod = (jax.nn.silu(c) @ w_ada + b_ada)[:, None, :]
    shift1, scale1, gate1, shift2, scale2, gate2 = jnp.split(mod, 6, axis=-1)
    h = rms_norm(x, norm_mix_g) * (1.0 + scale1) + shift1
    proj = h @ w_in
    q, c_kv, k_rope, u = jnp.split(
        proj, [Q_COLS, Q_COLS + KV_LORA_RANK, Q_COLS + KV_LORA_RANK + QK_ROPE_DIM], axis=-1)
    att = mla_mixer(q, c_kv, k_rope, kv_norm_g, w_ukv, q_norm_g, k_norm_g)
    ssm = s5_mixer(u, lam_re, lam_im, log_dt, b_re, b_im, c_re, c_im, d_skip, w_glu, b_glu)
    mixed = jnp.concatenate([rms_norm(att, att_out_g), rms_norm(ssm, ssm_out_g)], axis=-1)
    x = x + gate1 * (mixed @ w_o)
    h = rms_norm(x, norm_ffn_g) * (1.0 + scale2) + shift2
    ffn = (jax.nn.silu(h @ w1) * (h @ w3)) @ w2
    return x + gate2 * ffn


def setup_inputs(seed: int = 0) -> dict:
    key = jax.random.key(seed)
    ks = jax.random.split(key, 32)
    f32 = jnp.float32

    def nrm(k, shape, scale):
        return jax.random.normal(k, shape, f32) * scale

    def gain(k, n):
        return 1.0 + nrm(k, (DEPTH, n), 0.1)

    G, P = N_SSM_GROUPS, SSM_STATE
    n_idx = jnp.arange(P, dtype=f32)
    return {
        "x_prompt": nrm(ks[0], (BATCH, SEQ, D_MODEL), 1.0),
        "x_sample": nrm(ks[1], (DEC_BATCH, DEC_SEQ, D_MODEL), 1.0),
        "c_prompt": nrm(ks[2], (BATCH, D_MODEL), 1.0),
        "c_sample": nrm(ks[3], (DEC_BATCH, D_MODEL), 1.0),
        "w_ada": nrm(ks[4], (DEPTH, D_MODEL, 6 * D_MODEL), 0.5 * D_MODEL ** -0.5),
        "b_ada": nrm(ks[5], (DEPTH, 6 * D_MODEL), 0.01),
        "norm_mix_g": gain(ks[6], D_MODEL),
        "w_in": nrm(ks[7], (DEPTH, D_MODEL, IN_COLS), D_MODEL ** -0.5),
        "kv_norm_g": gain(ks[8], KV_LORA_RANK),
        "w_ukv": nrm(ks[9], (DEPTH, KV_LORA_RANK, N_HEADS * (QK_NOPE_DIM + V_HEAD_DIM)),
                     KV_LORA_RANK ** -0.5),
        "q_norm_g": gain(ks[10], QK_HEAD_DIM),
        "k_norm_g": gain(ks[11], QK_HEAD_DIM),
        "lam_re": -0.5 * jnp.exp(nrm(ks[12], (DEPTH, 2, G, P), 0.05)),
        "lam_im": jnp.pi * n_idx + nrm(ks[13], (DEPTH, 2, G, P), 0.01),
        "log_dt": jax.random.uniform(ks[14], (DEPTH, 2, G), f32,
                                     math.log(DT_MIN), math.log(DT_MAX)),
        "b_re": nrm(ks[15], (DEPTH, G, P, SSM_GROUP), (2 * SSM_GROUP) ** -0.5),
        "b_im": nrm(ks[16], (DEPTH, G, P, SSM_GROUP), (2 * SSM_GROUP) ** -0.5),
        "c_re": nrm(ks[17], (DEPTH, 2, G, SSM_GROUP, P), (2 * P) ** -0.5),
        "c_im": nrm(ks[18], (DEPTH, 2, G, SSM_GROUP, P), (2 * P) ** -0.5),
        "d_skip": nrm(ks[19], (DEPTH, SSM_WIDTH), 1.0),
        "w_glu": nrm(ks[20], (DEPTH, SSM_WIDTH, SSM_WIDTH), SSM_WIDTH ** -0.5),
        "b_glu": nrm(ks[21], (DEPTH, SSM_WIDTH), 0.01),
        "att_out_g": gain(ks[22], ATT_WIDTH),
        "ssm_out_g": gain(ks[23], SSM_WIDTH),
        "w_o": nrm(ks[24], (DEPTH, MIX_WIDTH, D_MODEL), MIX_WIDTH ** -0.5),
        "norm_ffn_g": gain(ks[25], D_MODEL),
        "w1": nrm(ks[26], (DEPTH, D_MODEL, D_FF), D_MODEL ** -0.5),
        "w3": nrm(ks[27], (DEPTH, D_MODEL, D_FF), D_MODEL ** -0.5),
        "w2": nrm(ks[28], (DEPTH, D_FF, D_MODEL), D_FF ** -0.5),
    }


def reference(x_prompt, x_sample, c_prompt, c_sample, w_ada, b_ada, norm_mix_g, w_in,
              kv_norm_g, w_ukv, q_norm_g, k_norm_g, lam_re, lam_im, log_dt, b_re, b_im,
              c_re, c_im, d_skip, w_glu, b_glu, att_out_g, ssm_out_g, w_o, norm_ffn_g,
              w1, w3, w2):
    y_prompt, y_sample = x_prompt, x_sample
    for l in range(DEPTH):
        layer = dict(w_ada=w_ada[l], b_ada=b_ada[l], norm_mix_g=norm_mix_g[l], w_in=w_in[l],
                     kv_norm_g=kv_norm_g[l], w_ukv=w_ukv[l], q_norm_g=q_norm_g[l],
                     k_norm_g=k_norm_g[l], lam_re=lam_re[l], lam_im=lam_im[l],
                     log_dt=log_dt[l], b_re=b_re[l], b_im=b_im[l], c_re=c_re[l], c_im=c_im[l],
                     d_skip=d_skip[l], w_glu=w_glu[l], b_glu=b_glu[l],
                     att_out_g=att_out_g[l], ssm_out_g=ssm_out_g[l], w_o=w_o[l],
                     norm_ffn_g=norm_ffn_g[l], w1=w1[l], w3=w3[l], w2=w2[l])
        y_prompt = encoder_layer(y_prompt, c_prompt, **layer)
        y_sample = encoder_layer(y_sample, c_sample, **layer)
    return (y_prompt, y_sample)
```

```python
import functools
import math

import jax
import jax.numpy as jnp
from jax import lax
from jax.experimental import pallas as pl
from jax.experimental.pallas import tpu as pltpu

F32 = jnp.float32
BF16 = jnp.bfloat16

D_MODEL = 2048
ATT_WIDTH = 1024
SSM_WIDTH = 1024
N_HEADS = 8
V_DIM = 128
NOPE_DIM = 128
ROPE_DIM = 64
QK_DIM = NOPE_DIM + ROPE_DIM
KV_RANK = 512
ROPE_BASE = 10000.0
SSM_GROUP = 16
N_GROUPS = SSM_WIDTH // SSM_GROUP
SSM_STATE = 64
D_FF = 5632
NORM_EPS = 1e-6

LANES = 128
SUBLANES = 8
VMEM_LIMIT = 56 * 1024 * 1024

C_QN = 0
C_QR = C_QN + N_HEADS * NOPE_DIM
C_KV = C_QR + N_HEADS * 2 * ROPE_DIM
C_KR = C_KV + KV_RANK
C_U = C_KR + 2 * ROPE_DIM
C_END = C_U + SSM_WIDTH

S5_CHUNK = 32
S5_ROWS = 8
S5_COLS = S5_CHUNK * SSM_GROUP
S5_STATE_COLS = 4 * SSM_STATE

HI = lax.Precision.HIGHEST


def _params(sem):
    return pltpu.CompilerParams(dimension_semantics=sem, vmem_limit_bytes=VMEM_LIMIT)


def _dot(a, b):
    return jnp.dot(a, b, preferred_element_type=F32)


def _ada_kernel(c_ref, w_ref, b_ref, o_ref):
    c = c_ref[...]
    s = c * jax.nn.sigmoid(c)
    s_hi = s.astype(BF16)
    s_lo = (s - s_hi.astype(F32)).astype(BF16)
    w = w_ref[...]
    w_hi = w.astype(BF16)
    w_lo = (w - w_hi.astype(F32)).astype(BF16)
    acc = _dot(s_hi, w_hi) + _dot(s_lo, w_hi) + _dot(s_hi, w_lo)
    o_ref[...] = acc + b_ref[...]


def ada_mod(c8, w_ada, b_ada):
    rows, d = c8.shape
    n = w_ada.shape[1]
    tn = 1024
    return pl.pallas_call(
        _ada_kernel,
        out_shape=jax.ShapeDtypeStruct((rows, n), F32),
        grid=(n // tn,),
        in_specs=[
            pl.BlockSpec((rows, d), lambda j: (0, 0)),
            pl.BlockSpec((d, tn), lambda j: (0, j)),
            pl.BlockSpec((1, tn), lambda j: (0, j)),
        ],
        out_specs=pl.BlockSpec((rows, tn), lambda j: (0, j)),
        compiler_params=_params(("arbitrary",)),
        name="ada_mod",
    )(c8, w_ada, b_ada.reshape(1, n))


def _rms(x, width):
    return lax.rsqrt(jnp.sum(x * x, axis=-1, keepdims=True) * (1.0 / width) + NORM_EPS)


def _in_proj_kernel(x_ref, mod_ref, g1_ref, w_ref, wukv_ref, kvg_ref, qg_ref, kg_ref,
                    qgr_ref, kgr_ref, cos_ref, sin_ref, q_ref, k_ref, v_ref, u_ref, *, q_scale):
    x = x_ref[0]
    shift = mod_ref[0, 0:1, :]
    scale = mod_ref[0, 1:2, :]
    h = (x * _rms(x, D_MODEL) * g1_ref[...]) * (1.0 + scale) + shift
    h = h.astype(BF16)

    cos = cos_ref[...]
    sin = sin_ref[...]

    def rope(col, gain):
        cg = col * gain
        return cg * cos + pltpu.roll(cg, ROPE_DIM, axis=1) * sin

    qn = _dot(h, w_ref[:, C_QN:C_QR])
    qr = _dot(h, w_ref[:, C_QR:C_KV])
    qg_n = qg_ref[...]
    qg_r = qgr_ref[...]
    for hd in range(N_HEADS):
        qn_h = qn[:, hd * LANES:(hd + 1) * LANES]
        qr_h = qr[:, hd * LANES:(hd + 1) * LANES]
        ss = (jnp.sum(qn_h * qn_h, axis=-1, keepdims=True)
              + 0.5 * jnp.sum(qr_h * qr_h, axis=-1, keepdims=True))
        r = lax.rsqrt(ss * (1.0 / QK_DIM) + NORM_EPS) * q_scale
        q_ref[0, hd, :, 0:NOPE_DIM] = (qn_h * r * qg_n).astype(BF16)
        q_ref[0, hd, :, NOPE_DIM:QK_DIM] = (rope(qr_h, qg_r) * r)[:, 0:ROPE_DIM].astype(BF16)

    ckv = _dot(h, w_ref[:, C_KV:C_KR])
    ckv_n = (ckv * _rms(ckv, KV_RANK) * kvg_ref[...]).astype(BF16)
    kv = _dot(ckv_n, wukv_ref[...])
    kr = _dot(h, w_ref[:, C_KR:C_U])
    kr_ss = 0.5 * jnp.sum(kr * kr, axis=-1, keepdims=True)
    kr_rot = rope(kr, kgr_ref[...])
    kg_n = kg_ref[...]
    ones = jnp.ones((x.shape[0], LANES), BF16)
    for hd in range(N_HEADS):
        kn_h = kv[:, hd * LANES:(hd + 1) * LANES]
        ss = jnp.sum(kn_h * kn_h, axis=-1, keepdims=True) + kr_ss
        r = lax.rsqrt(ss * (1.0 / QK_DIM) + NORM_EPS)
        k_ref[0, hd, :, 0:NOPE_DIM] = (kn_h * r * kg_n).astype(BF16)
        k_ref[0, hd, :, NOPE_DIM:QK_DIM] = (kr_rot * r)[:, 0:ROPE_DIM].astype(BF16)
        v_h = kv[:, ATT_WIDTH + hd * LANES:ATT_WIDTH + (hd + 1) * LANES]
        v_ref[0, hd, :, 0:V_DIM] = v_h.astype(BF16)
        v_ref[0, hd, :, V_DIM:2 * V_DIM] = ones

    u_ref[0] = _dot(h, w_ref[:, C_U:C_END])


def in_proj(x, mod, g1, w_all, w_ukv, kv_g, q_gn, k_gn, q_gr, k_gr, cos_t, sin_t, tm):
    b, length, d = x.shape
    tm = min(tm, length)
    const = lambda bi, i: (0, 0)
    kern = functools.partial(_in_proj_kernel, q_scale=math.log2(math.e) / math.sqrt(QK_DIM))
    return pl.pallas_call(
        kern,
        out_shape=(
            jax.ShapeDtypeStruct((b, N_HEADS, length, QK_DIM), BF16),
            jax.ShapeDtypeStruct((b, N_HEADS, length, QK_DIM), BF16),
            jax.ShapeDtypeStruct((b, N_HEADS, length, 2 * V_DIM), BF16),
            jax.ShapeDtypeStruct((b, length, SSM_WIDTH), F32),
        ),
        grid=(b, length // tm),
        in_specs=[
            pl.BlockSpec((1, tm, d), lambda bi, i: (bi, i, 0)),
            pl.BlockSpec((1, 6, d), lambda bi, i: (bi, 0, 0)),
            pl.BlockSpec((1, d), const),
            pl.BlockSpec((d, C_END), const),
            pl.BlockSpec((KV_RANK, 2 * ATT_WIDTH), const),
            pl.BlockSpec((1, KV_RANK), const),
            pl.BlockSpec((1, LANES), const),
            pl.BlockSpec((1, LANES), const),
            pl.BlockSpec((1, LANES), const),
            pl.BlockSpec((1, LANES), const),
            pl.BlockSpec((tm, LANES), lambda bi, i: (i, 0)),
            pl.BlockSpec((tm, LANES), lambda bi, i: (i, 0)),
        ],
        out_specs=(
            pl.BlockSpec((1, N_HEADS, tm, QK_DIM), lambda bi, i: (bi, 0, i, 0)),
            pl.BlockSpec((1, N_HEADS, tm, QK_DIM), lambda bi, i: (bi, 0, i, 0)),
            pl.BlockSpec((1, N_HEADS, tm, 2 * V_DIM), lambda bi, i: (bi, 0, i, 0)),
            pl.BlockSpec((1, tm, SSM_WIDTH), lambda bi, i: (bi, i, 0)),
        ),
        compiler_params=_params(("parallel", "parallel")),
        name="in_proj",
    )(x, mod, g1, w_all, w_ukv, kv_g, q_gn, k_gn, q_gr, k_gr, cos_t, sin_t)


def _attn_kernel(q_ref, k_ref, v_ref, o_ref, *, tk):
    q = q_ref[0, 0]
    tq = q.shape[0]
    n_chunks = k_ref.shape[2] // tk

    def body(j, carry):
        m, acc = carry
        start = pl.multiple_of(j * tk, tk)
        kc = k_ref[0, 0, pl.ds(start, tk), :]
        vc = v_ref[0, 0, pl.ds(start, tk), :]
        s = lax.dot_general(q, kc, (((1,), (1,)), ((), ())), preferred_element_type=F32)
        m_new = jnp.maximum(m, jnp.max(s, axis=-1, keepdims=True))
        alpha = jnp.exp2(m - m_new)
        p = jnp.exp2(s - m_new).astype(BF16)
        acc = alpha * acc + _dot(p, vc)
        return m_new, acc

    m0 = jnp.full((tq, 1), -jnp.inf, F32)
    acc0 = jnp.zeros((tq, 2 * V_DIM), F32)
    _, acc = lax.fori_loop(0, n_chunks, body, (m0, acc0))
    o_ref[0] = (acc[:, 0:V_DIM] / acc[:, V_DIM:2 * V_DIM]).astype(o_ref.dtype)


def attention(q, k, v, tq, tk):
    b, h, length, _ = q.shape
    tq = min(tq, length)
    tk = min(tk, length)
    return pl.pallas_call(
        functools.partial(_attn_kernel, tk=tk),
        out_shape=jax.ShapeDtypeStruct((b, length, h * V_DIM), BF16),
        grid=(b, h, length // tq),
        in_specs=[
            pl.BlockSpec((1, 1, tq, QK_DIM), lambda bi, hi, i: (bi, hi, i, 0)),
            pl.BlockSpec((1, 1, length, QK_DIM), lambda bi, hi, i: (bi, hi, 0, 0)),
            pl.BlockSpec((1, 1, length, 2 * V_DIM), lambda bi, hi, i: (bi, hi, 0, 0)),
        ],
        out_specs=pl.BlockSpec((1, tq, V_DIM), lambda bi, hi, i: (bi, i, hi)),
        compiler_params=_params(("parallel", "parallel", "parallel")),
        name="attention",
    )(q, k, v)


def s5_matrices(lam_re, lam_im, log_dt, b_re, b_im, c_re, c_im):
    t = S5_CHUNK
    dt = jnp.exp(log_dt)[..., None]
    mag1 = jnp.exp(lam_re * dt)
    ang1 = lam_im * dt
    lb_re, lb_im = mag1 * jnp.cos(ang1), mag1 * jnp.sin(ang1)
    den = lam_re * lam_re + lam_im * lam_im
    n_re, n_im = lb_re - 1.0, lb_im
    cf_re = (n_re * lam_re + n_im * lam_im) / den
    cf_im = (n_im * lam_re - n_re * lam_im) / den
    bb_re = cf_re[..., None] * b_re[None] - cf_im[..., None] * b_im[None]
    bb_im = cf_re[..., None] * b_im[None] + cf_im[..., None] * b_re[None]

    kk = jnp.arange(t + 1, dtype=F32)
    mag = jnp.exp(lam_re[..., None] * dt[..., None] * kk)
    ang = ang1[..., None] * kk
    pw_re, pw_im = mag * jnp.cos(ang), mag * jnp.sin(ang)

    pr = jnp.moveaxis(pw_re, -1, 2)[:, :, :, None, :]
    pi = jnp.moveaxis(pw_im, -1, 2)[:, :, :, None, :]
    cr, ci = c_re[:, :, None], c_im[:, :, None]
    cp_re = cr * pr - ci * pi
    cp_im = cr * pi + ci * pr
    kern = (jnp.einsum("dgkop,dgpi->dgkoi", cp_re, bb_re, precision=HI)
            - jnp.einsum("dgkop,dgpi->dgkoi", cp_im, bb_im, precision=HI))

    s_idx = jnp.arange(t)[:, None]
    t_idx = jnp.arange(t)[None, :]
    dfw = t_idx - s_idx
    kf = jnp.where((dfw >= 0)[None, :, :, None, None], kern[0][:, jnp.clip(dfw, 0, t)], 0.0)
    kb = jnp.where((dfw <= 0)[None, :, :, None, None], kern[1][:, jnp.clip(-dfw, 0, t)], 0.0)
    toep = (kf + kb).transpose(0, 1, 4, 2, 3).reshape(N_GROUPS, S5_COLS, S5_COLS)

    def st(pw_r, pw_i, br, bi, order):
        pr_ = jnp.moveaxis(pw_r[..., order], -1, 1)[..., None]
        pi_ = jnp.moveaxis(pw_i[..., order], -1, 1)[..., None]
        re = pr_ * br[:, None] - pi_ * bi[:, None]
        im = pr_ * bi[:, None] + pi_ * br[:, None]
        fix = lambda z: z.transpose(0, 1, 3, 2).reshape(N_GROUPS, S5_COLS, SSM_STATE)
        return fix(re), fix(im)

    f_re, f_im = st(pw_re[0], pw_im[0], bb_re[0], bb_im[0], jnp.arange(t - 1, -1, -1))
    g_re, g_im = st(pw_re[1], pw_im[1], bb_re[1], bb_im[1], jnp.arange(t))
    m_st = jnp.concatenate([f_re, g_re, f_im, g_im], axis=-1)

    def outm(cp_r, cp_i, order):
        wr = cp_r[:, order]
        wi = cp_i[:, order]
        fix = lambda z: z.transpose(0, 3, 1, 2).reshape(N_GROUPS, SSM_STATE, S5_COLS)
        return fix(wr), fix(-wi)

    of_re, of_im = outm(cp_re[0], cp_im[0], jnp.arange(1, t + 1))
    ob_re, ob_im = outm(cp_re[1], cp_im[1], jnp.arange(t, 0, -1))
    m_out = jnp.concatenate([of_re, ob_re, of_im, ob_im], axis=1)

    a_re = jnp.concatenate([pw_re[0][..., t], pw_re[1][..., t]], axis=-1)[:, None, :]
    a_im = jnp.concatenate([pw_im[0][..., t], pw_im[1][..., t]], axis=-1)[:, None, :]
    return toep.astype(BF16), m_st.astype(BF16), m_out.astype(BF16), a_re, a_im


def _s5_kernel(u_ref, toep_ref, mst_ref, mout_ref, are_ref, aim_ref, y_ref, st_ref, xin_ref):
    u = u_ref[0]
    n_chunks = u.shape[0] // S5_ROWS
    st_ref[...] = _dot(u, mst_ref[0])

    a_re = jnp.broadcast_to(are_ref[0], (S5_ROWS, LANES))
    a_im = jnp.broadcast_to(aim_ref[0], (S5_ROWS, LANES))
    is_fwd = lax.broadcasted_iota(jnp.int32, (S5_ROWS, LANES), 1) < SSM_STATE
    half = SSM_STATE

    def body(i, carry):
        x_re, x_im = carry
        rf = pl.multiple_of(i * S5_ROWS, S5_ROWS)
        rb = pl.multiple_of((n_chunks - 1 - i) * S5_ROWS, S5_ROWS)
        xin_ref[pl.ds(rf, S5_ROWS), 0:half] = x_re[:, 0:half]
        xin_ref[pl.ds(rb, S5_ROWS), half:LANES] = x_re[:, half:LANES]
        xin_ref[pl.ds(rf, S5_ROWS), LANES:LANES + half] = x_im[:, 0:half]
        xin_ref[pl.ds(rb, S5_ROWS), LANES + half:2 * LANES] = x_im[:, half:LANES]
        s_re = jnp.where(is_fwd, st_ref[pl.ds(rf, S5_ROWS), 0:LANES], st_ref[pl.ds(rb, S5_ROWS), 0:LANES])
        s_im = jnp.where(is_fwd, st_ref[pl.ds(rf, S5_ROWS), LANES:2 * LANES],
                         st_ref[pl.ds(rb, S5_ROWS), LANES:2 * LANES])
        n_re = a_re * x_re - a_im * x_im + s_re
        n_im = a_re * x_im + a_im * x_re + s_im
        return n_re, n_im

    zero = jnp.zeros((S5_ROWS, LANES), F32)
    lax.fori_loop(0, n_chunks, body, (zero, zero))
    y_ref[0] = _dot(u, toep_ref[0]) + _dot(xin_ref[...].astype(BF16), mout_ref[0])


def s5_scan(u_flat, toep, m_st, m_out, a_re, a_im):
    g, rows, cols = u_flat.shape
    gmap = lambda gi: (gi, 0, 0)
    return pl.pallas_call(
        _s5_kernel,
        out_shape=jax.ShapeDtypeStruct((g, rows, cols), F32),
        grid=(g,),
        in_specs=[
            pl.BlockSpec((1, rows, cols), gmap),
            pl.BlockSpec((1, cols, cols), gmap),
            pl.BlockSpec((1, cols, S5_STATE_COLS), gmap),
            pl.BlockSpec((1, S5_STATE_COLS, cols), gmap),
            pl.BlockSpec((1, 1, LANES), gmap),
            pl.BlockSpec((1, 1, LANES), gmap),
        ],
        out_specs=pl.BlockSpec((1, rows, cols), gmap),
        scratch_shapes=[pltpu.VMEM((rows, S5_STATE_COLS), F32), pltpu.VMEM((rows, S5_STATE_COLS), F32)],
        compiler_params=_params(("parallel",)),
        name="s5_scan",
    )(u_flat, toep, m_st, m_out, a_re, a_im)


def _gelu_tanh(x):
    return 0.5 * x * (1.0 + jnp.tanh(math.sqrt(2.0 / math.pi) * (x + 0.044715 * (x * x * x))))


def _post_mix_kernel(att_ref, ys_ref, u_ref, x_ref, mod_ref, dsk_ref, wglu_ref, bglu_ref,
                     ag_ref, sg_ref, wo_ref, g2_ref, x1_ref, h2_ref):
    gate1 = mod_ref[0, 2:3, :]
    shift2 = mod_ref[0, 3:4, :]
    scale2 = mod_ref[0, 4:5, :]
    y = ys_ref[0] + dsk_ref[...] * u_ref[0]
    y = _gelu_tanh(y)
    z = _dot(y.astype(BF16), wglu_ref[...]) + bglu_ref[...]
    ssm = y * jax.nn.sigmoid(z)
    att = att_ref[0].astype(F32)
    att_n = (att * _rms(att, ATT_WIDTH) * ag_ref[...]).astype(BF16)
    ssm_n = (ssm * _rms(ssm, SSM_WIDTH) * sg_ref[...]).astype(BF16)
    mix = _dot(att_n, wo_ref[0:ATT_WIDTH, :]) + _dot(ssm_n, wo_ref[ATT_WIDTH:2 * ATT_WIDTH, :])
    x1 = x_ref[0] + gate1 * mix
    x1_ref[0] = x1
    h2 = (x1 * _rms(x1, D_MODEL) * g2_ref[...]) * (1.0 + scale2) + shift2
    h2_ref[0] = h2.astype(BF16)


def post_mix(att, ys, u, x, mod, d_skip, w_glu, b_glu, att_g, ssm_g, w_o, g2, tm):
    b, length, d = x.shape
    tm = min(tm, length)
    const = lambda bi, i: (0, 0)
    tile = lambda w: pl.BlockSpec((1, tm, w), lambda bi, i: (bi, i, 0))
    return pl.pallas_call(
        _post_mix_kernel,
        out_shape=(jax.ShapeDtypeStruct((b, length, d), F32),
                   jax.ShapeDtypeStruct((b, length, d), BF16)),
        grid=(b, length // tm),
        in_specs=[
            tile(ATT_WIDTH), tile(SSM_WIDTH), tile(SSM_WIDTH), tile(d),
            pl.BlockSpec((1, 6, d), lambda bi, i: (bi, 0, 0)),
            pl.BlockSpec((1, SSM_WIDTH), const),
            pl.BlockSpec((SSM_WIDTH, SSM_WIDTH), const),
            pl.BlockSpec((1, SSM_WIDTH), const),
            pl.BlockSpec((1, ATT_WIDTH), const),
            pl.BlockSpec((1, SSM_WIDTH), const),
            pl.BlockSpec((d, d), const),
            pl.BlockSpec((1, d), const),
        ],
        out_specs=(tile(d), tile(d)),
        compiler_params=_params(("parallel", "parallel")),
        name="post_mix",
    )(att, ys, u, x, mod, d_skip, w_glu, b_glu, att_g, ssm_g, w_o, g2)


def _ffn_kernel(h_ref, x1_ref, mod_ref, w1_ref, w3_ref, w2_ref, o_ref):
    f = pl.program_id(2)
    h = h_ref[0]
    a = _dot(h, w1_ref[...])
    g = _dot(h, w3_ref[...])
    act = (a * jax.nn.sigmoid(a) * g).astype(BF16)
    part = _dot(act, w2_ref[...])

    @pl.when(f == 0)
    def _():
        o_ref[0] = part

    @pl.when(f > 0)
    def _():
        o_ref[0] += part

    @pl.when(f == pl.num_programs(2) - 1)
    def _():
        o_ref[0] = x1_ref[0] + mod_ref[0, 5:6, :] * o_ref[0]


def ffn(h2, x1, mod, w1, w3, w2, tm, tf):
    b, length, d = x1.shape
    tm = min(tm, length)
    return pl.pallas_call(
        _ffn_kernel,
        out_shape=jax.ShapeDtypeStruct((b, length, d), F32),
        grid=(b, length // tm, D_FF // tf),
        in_specs=[
            pl.BlockSpec((1, tm, d), lambda bi, i, f: (bi, i, 0)),
            pl.BlockSpec((1, tm, d), lambda bi, i, f: (bi, i, 0)),
            pl.BlockSpec((1, 6, d), lambda bi, i, f: (bi, 0, 0)),
            pl.BlockSpec((d, tf), lambda bi, i, f: (0, f)),
            pl.BlockSpec((d, tf), lambda bi, i, f: (0, f)),
            pl.BlockSpec((tf, d), lambda bi, i, f: (f, 0)),
        ],
        out_specs=pl.BlockSpec((1, tm, d), lambda bi, i, f: (bi, i, 0)),
        compiler_params=_params(("parallel", "parallel", "arbitrary")),
        name="ffn",
    )(h2, x1, mod, w1, w3, w2)


def _swap_halves(a, axis=-1):
    lo, hi = jnp.split(a, 2, axis=axis)
    return jnp.concatenate([hi, lo], axis=axis)


def _prep_w_in(w_in):
    d = w_in.shape[0]
    q_cols = N_HEADS * QK_DIM
    wq = w_in[:, :q_cols].reshape(d, N_HEADS, QK_DIM)
    wq_n = wq[:, :, :NOPE_DIM].reshape(d, N_HEADS * NOPE_DIM)
    wq_r = wq[:, :, NOPE_DIM:]
    wq_rr = jnp.concatenate([wq_r, _swap_halves(wq_r)], axis=-1).reshape(d, N_HEADS * 2 * ROPE_DIM)
    w_c = w_in[:, q_cols:q_cols + KV_RANK]
    w_kr = w_in[:, q_cols + KV_RANK:q_cols + KV_RANK + ROPE_DIM]
    w_krr = jnp.concatenate([w_kr, _swap_halves(w_kr)], axis=-1)
    w_u = w_in[:, q_cols + KV_RANK + ROPE_DIM:]
    return jnp.concatenate([wq_n, wq_rr, w_c, w_krr, w_u], axis=-1).astype(BF16)


def _prep_w_ukv(w_ukv):
    w = w_ukv.reshape(KV_RANK, N_HEADS, NOPE_DIM + V_DIM)
    wk = w[:, :, :NOPE_DIM].reshape(KV_RANK, N_HEADS * NOPE_DIM)
    wv = w[:, :, NOPE_DIM:].reshape(KV_RANK, N_HEADS * V_DIM)
    return jnp.concatenate([wk, wv], axis=-1).astype(BF16)


def _rope_gain(g):
    gr = g[NOPE_DIM:]
    return jnp.concatenate([gr, _swap_halves(gr)]).reshape(1, 2 * ROPE_DIM)


def _rope_tables(length):
    pos = jnp.arange(length, dtype=F32)
    inv_freq = ROPE_BASE ** (-jnp.arange(0, ROPE_DIM, 2, dtype=F32) / ROPE_DIM)
    ang = pos[:, None] * inv_freq[None, :]
    cos, sin = jnp.cos(ang), jnp.sin(ang)
    return jnp.tile(cos, (1, 4)), jnp.tile(jnp.concatenate([-sin, sin], axis=-1), (1, 2))


def _to_s5_rows(us):
    u = jnp.concatenate(us, axis=0)
    b, length, _ = u.shape
    u = u.reshape(b, length // S5_CHUNK, S5_CHUNK, N_GROUPS, SSM_GROUP).astype(BF16)
    u = u.transpose(3, 1, 0, 2, 4)
    u = jnp.pad(u, ((0, 0), (0, 0), (0, S5_ROWS - b), (0, 0), (0, 0)))
    return u.reshape(N_GROUPS, (length // S5_CHUNK) * S5_ROWS, S5_COLS)


def _from_s5_rows(y, batches, length):
    y = y.reshape(N_GROUPS, length // S5_CHUNK, S5_ROWS, S5_CHUNK, SSM_GROUP)
    y = y.transpose(2, 1, 3, 0, 4).reshape(S5_ROWS, length, SSM_WIDTH)
    out, start = [], 0
    for nb in batches:
        out.append(y[start:start + nb])
        start += nb
    return out


def kernel(x_prompt, x_sample, c_prompt, c_sample, w_ada, b_ada, norm_mix_g, w_in, kv_norm_g, w_ukv,
           q_norm_g, k_norm_g, lam_re, lam_im, log_dt, b_re, b_im, c_re, c_im, d_skip, w_glu, b_glu,
           att_out_g, ssm_out_g, w_o, norm_ffn_g, w1, w3, w2):
    assert w_ada.shape[0] == 1, "single-layer kernel"
    xs = (x_prompt, x_sample)
    batches = tuple(x.shape[0] for x in xs)
    length = x_prompt.shape[1]
    assert x_sample.shape[1] == length and sum(batches) <= S5_ROWS

    c_all = jnp.concatenate([c_prompt, c_sample], axis=0)
    c8 = jnp.pad(c_all, ((0, SUBLANES - c_all.shape[0]), (0, 0)))
    mod_all = ada_mod(c8, w_ada[0], b_ada[0]).reshape(SUBLANES, 6, D_MODEL)
    mods = (mod_all[:batches[0]], mod_all[batches[0]:batches[0] + batches[1]])

    w_all = _prep_w_in(w_in[0])
    w_ukv_p = _prep_w_ukv(w_ukv[0])
    row = lambda a: a.reshape(1, -1)
    q_gn, k_gn = row(q_norm_g[0][:NOPE_DIM]), row(k_norm_g[0][:NOPE_DIM])
    q_gr, k_gr = _rope_gain(q_norm_g[0]), _rope_gain(k_norm_g[0])
    cos_t, sin_t = _rope_tables(length)
    s5_ops = s5_matrices(lam_re[0], lam_im[0], log_dt[0], b_re[0], b_im[0], c_re[0], c_im[0])
    w_glu_b, w_o_b = w_glu[0].astype(BF16), w_o[0].astype(BF16)
    w1_b, w3_b, w2_b = w1[0].astype(BF16), w3[0].astype(BF16), w2[0].astype(BF16)

    proj = [in_proj(x, m, row(norm_mix_g[0]), w_all, w_ukv_p, row(kv_norm_g[0]), q_gn, k_gn, q_gr, k_gr,
                    cos_t, sin_t, tm=256) for x, m in zip(xs, mods)]
    atts = [attention(q, k, v, tq=256, tk=1024) for q, k, v, _ in proj]
    us = [p[3] for p in proj]
    y_flat = s5_scan(_to_s5_rows(us), *s5_ops)
    yss = _from_s5_rows(y_flat, batches, length)

    outs = []
    for x, m, att, ys, u in zip(xs, mods, atts, yss, us):
        x1, h2 = post_mix(att, ys, u, x, m, row(d_skip[0]), w_glu_b, row(b_glu[0]), row(att_out_g[0]),
                          row(ssm_out_g[0]), w_o_b, row(norm_ffn_g[0]), tm=256)
        outs.append(ffn(h2, x1, m, w1_b, w3_b, w2_b, tm=512, tf=512))
    return tuple(outs)
```

```python
import functools
import math

import jax
import jax.numpy as jnp
from jax import lax
from jax.experimental import pallas as pl
from jax.experimental.pallas import tpu as pltpu

F32 = jnp.float32
BF16 = jnp.bfloat16

D_MODEL = 2048
ATT_WIDTH = 1024
SSM_WIDTH = 1024
N_HEADS = 8
V_DIM = 128
NOPE_DIM = 128
ROPE_DIM = 64
QK_DIM = NOPE_DIM + ROPE_DIM
KV_RANK = 512
ROPE_BASE = 10000.0
SSM_GROUP = 16
N_GROUPS = SSM_WIDTH // SSM_GROUP
SSM_STATE = 64
D_FF = 5632
NORM_EPS = 1e-6

LANES = 128
SUBLANES = 8
VMEM_LIMIT = 56 * 1024 * 1024
CAST_BLOCK_BYTES = 4 * 1024 * 1024

C_QN = 0
C_QR = C_QN + N_HEADS * NOPE_DIM
C_KV = C_QR + N_HEADS * 2 * ROPE_DIM
C_KR = C_KV + KV_RANK
C_U = C_KR + 2 * ROPE_DIM
C_END = C_U + SSM_WIDTH

S5_CHUNK = 32
S5_ROWS = SUBLANES
S5_SUB = SUBLANES
S5_TILE = S5_SUB * S5_CHUNK
S5_COLS = S5_CHUNK * SSM_GROUP
S5_STATE_COLS = 4 * SSM_STATE
GROUPS_PER_COL = LANES // SSM_GROUP

HI = lax.Precision.HIGHEST


def _params(sem):
    return pltpu.CompilerParams(dimension_semantics=sem, vmem_limit_bytes=VMEM_LIMIT)


def _dot(a, b):
    return jnp.dot(a, b, preferred_element_type=F32)


def _ada_kernel(c_ref, w_ref, b_ref, o_ref):
    c = c_ref[...]
    s = c * jax.nn.sigmoid(c)
    s_hi = s.astype(BF16)
    s_lo = (s - s_hi.astype(F32)).astype(BF16)
    w = w_ref[...]
    w_hi = w.astype(BF16)
    w_lo = (w - w_hi.astype(F32)).astype(BF16)
    acc = _dot(s_hi, w_hi) + _dot(s_lo, w_hi) + _dot(s_hi, w_lo)
    o_ref[...] = acc + b_ref[...]


def ada_mod(c8, w_ada, b_ada):
    rows, d = c8.shape
    n = w_ada.shape[1]
    tn = 1024
    return pl.pallas_call(
        _ada_kernel,
        out_shape=jax.ShapeDtypeStruct((rows, n), F32),
        grid=(n // tn,),
        in_specs=[
            pl.BlockSpec((rows, d), lambda j: (0, 0)),
            pl.BlockSpec((d, tn), lambda j: (0, j)),
            pl.BlockSpec((1, tn), lambda j: (0, j)),
        ],
        out_specs=pl.BlockSpec((rows, tn), lambda j: (0, j)),
        compiler_params=_params(("arbitrary",)),
        name="ada_mod",
    )(c8, w_ada, b_ada.reshape(1, n))


def _rms(x, width):
    return lax.rsqrt(jnp.sum(x * x, axis=-1, keepdims=True) * (1.0 / width) + NORM_EPS)


def _granule_masks():
    pos = lax.broadcasted_iota(jnp.int32, (SUBLANES, LANES), 1) // SSM_GROUP
    return [pos == p for p in range(GROUPS_PER_COL)]


def _to_chunk_rows(u_cols, dst):
    masks = _granule_masks()
    for j in range(SSM_WIDTH // LANES):
        rolled = []
        for t in range(S5_CHUNK):
            v = u_cols[j, pl.ds(t, S5_SUB, stride=S5_CHUNK), :]
            r = t % GROUPS_PER_COL
            rolled.append(pltpu.roll(v, SSM_GROUP * r, axis=1) if r else v)
        for k in range(GROUPS_PER_COL):
            for d in range(S5_COLS // LANES):
                acc = rolled[GROUPS_PER_COL * d + GROUPS_PER_COL - 1]
                for rho in range(GROUPS_PER_COL - 2, -1, -1):
                    acc = jnp.where(masks[(k + rho) % GROUPS_PER_COL], rolled[GROUPS_PER_COL * d + rho], acc)
                dst[GROUPS_PER_COL * j + k, :, d * LANES:(d + 1) * LANES] = acc


def _from_chunk_rows(src, y_cols):
    masks = _granule_masks()
    for j in range(SSM_WIDTH // LANES):
        for d in range(S5_COLS // LANES):
            cols = [src[GROUPS_PER_COL * j + k, :, d * LANES:(d + 1) * LANES] for k in range(GROUPS_PER_COL)]
            for rho in range(GROUPS_PER_COL):
                acc = cols[GROUPS_PER_COL - 1]
                for k in range(GROUPS_PER_COL - 2, -1, -1):
                    acc = jnp.where(masks[(k + rho) % GROUPS_PER_COL], cols[k], acc)
                v = pltpu.roll(acc, LANES - SSM_GROUP * rho, axis=1) if rho else acc
                y_cols[j, pl.ds(GROUPS_PER_COL * d + rho, S5_SUB, stride=S5_CHUNK), :] = v


def _in_proj_kernel(x_ref, mod_ref, g1_ref, w_ref, wukv_ref, kvg_ref, qg_ref, kg_ref,
                    qgr_ref, kgr_ref, cos_ref, sin_ref, q_ref, k_ref, v_ref, u_ref, ur_ref, u_cols,
                    *, q_scale):
    x = x_ref[0]
    shift = mod_ref[0, 0:1, :]
    scale = mod_ref[0, 1:2, :]
    h = (x * _rms(x, D_MODEL) * g1_ref[...]) * (1.0 + scale) + shift
    h = h.astype(BF16)

    cos = cos_ref[...]
    sin = sin_ref[...]

    def rope(col, gain):
        cg = col * gain
        return cg * cos + pltpu.roll(cg, ROPE_DIM, axis=1) * sin

    qn = _dot(h, w_ref[:, C_QN:C_QR])
    qr = _dot(h, w_ref[:, C_QR:C_KV])
    qg_n = qg_ref[...]
    qg_r = qgr_ref[...]
    for hd in range(N_HEADS):
        qn_h = qn[:, hd * LANES:(hd + 1) * LANES]
        qr_h = qr[:, hd * LANES:(hd + 1) * LANES]
        ss = (jnp.sum(qn_h * qn_h, axis=-1, keepdims=True)
              + 0.5 * jnp.sum(qr_h * qr_h, axis=-1, keepdims=True))
        r = lax.rsqrt(ss * (1.0 / QK_DIM) + NORM_EPS) * q_scale
        q_ref[0, hd, :, 0:NOPE_DIM] = (qn_h * r * qg_n).astype(BF16)
        q_ref[0, hd, :, NOPE_DIM:QK_DIM] = (rope(qr_h, qg_r) * r)[:, 0:ROPE_DIM].astype(BF16)

    ckv = _dot(h, w_ref[:, C_KV:C_KR])
    ckv_n = (ckv * _rms(ckv, KV_RANK) * kvg_ref[...]).astype(BF16)
    kv = _dot(ckv_n, wukv_ref[...])
    kr = _dot(h, w_ref[:, C_KR:C_U])
    kr_ss = 0.5 * jnp.sum(kr * kr, axis=-1, keepdims=True)
    kr_rot = rope(kr, kgr_ref[...])
    kg_n = kg_ref[...]
    ones = jnp.ones((x.shape[0], LANES), BF16)
    for hd in range(N_HEADS):
        kn_h = kv[:, hd * LANES:(hd + 1) * LANES]
        ss = jnp.sum(kn_h * kn_h, axis=-1, keepdims=True) + kr_ss
        r = lax.rsqrt(ss * (1.0 / QK_DIM) + NORM_EPS)
        k_ref[0, hd, :, 0:NOPE_DIM] = (kn_h * r * kg_n).astype(BF16)
        k_ref[0, hd, :, NOPE_DIM:QK_DIM] = (kr_rot * r)[:, 0:ROPE_DIM].astype(BF16)
        v_h = kv[:, ATT_WIDTH + hd * LANES:ATT_WIDTH + (hd + 1) * LANES]
        v_ref[0, hd, :, 0:V_DIM] = v_h.astype(BF16)
        v_ref[0, hd, :, V_DIM:2 * V_DIM] = ones

    u = _dot(h, w_ref[:, C_U:C_END])
    u_ref[0] = u
    for j in range(SSM_WIDTH // LANES):
        u_cols[j] = u[:, j * LANES:(j + 1) * LANES]
    _to_chunk_rows(u_cols, ur_ref)


def in_proj(x, mod, g1, w_all, w_ukv, kv_g, q_gn, k_gn, q_gr, k_gr, cos_t, sin_t):
    b, length, d = x.shape
    tm = S5_TILE
    assert length % tm == 0
    const = lambda bi, i: (0, 0)
    kern = functools.partial(_in_proj_kernel, q_scale=math.log2(math.e) / math.sqrt(QK_DIM))
    return pl.pallas_call(
        kern,
        out_shape=(
            jax.ShapeDtypeStruct((b, N_HEADS, length, QK_DIM), BF16),
            jax.ShapeDtypeStruct((b, N_HEADS, length, QK_DIM), BF16),
            jax.ShapeDtypeStruct((b, N_HEADS, length, 2 * V_DIM), BF16),
            jax.ShapeDtypeStruct((b, length, SSM_WIDTH), F32),
            jax.ShapeDtypeStruct((N_GROUPS, length // S5_CHUNK, b * S5_COLS), F32),
        ),
        grid=(b, length // tm),
        in_specs=[
            pl.BlockSpec((1, tm, d), lambda bi, i: (bi, i, 0)),
            pl.BlockSpec((1, 6, d), lambda bi, i: (bi, 0, 0)),
            pl.BlockSpec((1, d), const),
            pl.BlockSpec((d, C_END), const),
            pl.BlockSpec((KV_RANK, 2 * ATT_WIDTH), const),
            pl.BlockSpec((1, KV_RANK), const),
            pl.BlockSpec((1, LANES), const),
            pl.BlockSpec((1, LANES), const),
            pl.BlockSpec((1, LANES), const),
            pl.BlockSpec((1, LANES), const),
            pl.BlockSpec((tm, LANES), lambda bi, i: (i, 0)),
            pl.BlockSpec((tm, LANES), lambda bi, i: (i, 0)),
        ],
        out_specs=(
            pl.BlockSpec((1, N_HEADS, tm, QK_DIM), lambda bi, i: (bi, 0, i, 0)),
            pl.BlockSpec((1, N_HEADS, tm, QK_DIM), lambda bi, i: (bi, 0, i, 0)),
            pl.BlockSpec((1, N_HEADS, tm, 2 * V_DIM), lambda bi, i: (bi, 0, i, 0)),
            pl.BlockSpec((1, tm, SSM_WIDTH), lambda bi, i: (bi, i, 0)),
            pl.BlockSpec((N_GROUPS, S5_SUB, S5_COLS), lambda bi, i: (0, i, bi)),
        ),
        scratch_shapes=[pltpu.VMEM((SSM_WIDTH // LANES, tm, LANES), F32)],
        compiler_params=_params(("parallel", "parallel")),
        name="in_proj",
    )(x, mod, g1, w_all, w_ukv, kv_g, q_gn, k_gn, q_gr, k_gr, cos_t, sin_t)


def _attn_kernel(q_ref, k_ref, v_ref, o_ref, *, tk):
    q = q_ref[0, 0]
    tq = q.shape[0]
    m = jnp.full((tq, 1), -jnp.inf, F32)
    acc = jnp.zeros((tq, 2 * V_DIM), F32)
    for j in range(k_ref.shape[2] // tk):
        kc = k_ref[0, 0, j * tk:(j + 1) * tk, :]
        vc = v_ref[0, 0, j * tk:(j + 1) * tk, :]
        s = lax.dot_general(q, kc, (((1,), (1,)), ((), ())), preferred_element_type=F32)
        m_new = jnp.maximum(m, jnp.max(s, axis=-1, keepdims=True))
        alpha = jnp.exp2(m - m_new)
        p = jnp.exp2(s - m_new).astype(BF16)
        acc = alpha * acc + _dot(p, vc)
        m = m_new
    o_ref[0] = (acc[:, 0:V_DIM] / acc[:, V_DIM:2 * V_DIM]).astype(o_ref.dtype)


def attention(q, k, v, tq, tk):
    b, h, length, _ = q.shape
    tq = min(tq, length)
    tk = min(tk, length)
    return pl.pallas_call(
        functools.partial(_attn_kernel, tk=tk),
        out_shape=jax.ShapeDtypeStruct((b, length, h * V_DIM), BF16),
        grid=(b, h, length // tq),
        in_specs=[
            pl.BlockSpec((1, 1, tq, QK_DIM), lambda bi, hi, i: (bi, hi, i, 0)),
            pl.BlockSpec((1, 1, length, QK_DIM), lambda bi, hi, i: (bi, hi, 0, 0)),
            pl.BlockSpec((1, 1, length, 2 * V_DIM), lambda bi, hi, i: (bi, hi, 0, 0)),
        ],
        out_specs=pl.BlockSpec((1, tq, V_DIM), lambda bi, hi, i: (bi, i, hi)),
        compiler_params=_params(("parallel", "parallel", "parallel")),
        name="attention",
    )(q, k, v)


def s5_matrices(lam_re, lam_im, log_dt, b_re, b_im, c_re, c_im):
    t = S5_CHUNK
    dt = jnp.exp(log_dt)[..., None]
    mag1 = jnp.exp(lam_re * dt)
    ang1 = lam_im * dt
    lb_re, lb_im = mag1 * jnp.cos(ang1), mag1 * jnp.sin(ang1)
    den = lam_re * lam_re + lam_im * lam_im
    n_re, n_im = lb_re - 1.0, lb_im
    cf_re = (n_re * lam_re + n_im * lam_im) / den
    cf_im = (n_im * lam_re - n_re * lam_im) / den
    bb_re = cf_re[..., None] * b_re[None] - cf_im[..., None] * b_im[None]
    bb_im = cf_re[..., None] * b_im[None] + cf_im[..., None] * b_re[None]

    kk = jnp.arange(t + 1, dtype=F32)
    mag = jnp.exp(lam_re[..., None] * dt[..., None] * kk)
    ang = ang1[..., None] * kk
    pw_re, pw_im = mag * jnp.cos(ang), mag * jnp.sin(ang)

    pr = jnp.moveaxis(pw_re, -1, 2)[:, :, :, None, :]
    pi = jnp.moveaxis(pw_im, -1, 2)[:, :, :, None, :]
    cr, ci = c_re[:, :, None], c_im[:, :, None]
    cp_re = cr * pr - ci * pi
    cp_im = cr * pi + ci * pr
    kern = (jnp.einsum("dgkop,dgpi->dgkoi", cp_re, bb_re, precision=HI)
            - jnp.einsum("dgkop,dgpi->dgkoi", cp_im, bb_im, precision=HI))

    s_idx = jnp.arange(t)[:, None]
    t_idx = jnp.arange(t)[None, :]
    dfw = t_idx - s_idx
    kf = jnp.where((dfw >= 0)[None, :, :, None, None], kern[0][:, jnp.clip(dfw, 0, t)], 0.0)
    kb = jnp.where((dfw <= 0)[None, :, :, None, None], kern[1][:, jnp.clip(-dfw, 0, t)], 0.0)
    toep = (kf + kb).transpose(0, 1, 4, 2, 3).reshape(N_GROUPS, S5_COLS, S5_COLS)

    def st(pw_r, pw_i, br, bi, order):
        pr_ = jnp.moveaxis(pw_r[..., order], -1, 1)[..., None]
        pi_ = jnp.moveaxis(pw_i[..., order], -1, 1)[..., None]
        re = pr_ * br[:, None] - pi_ * bi[:, None]
        im = pr_ * bi[:, None] + pi_ * br[:, None]
        fix = lambda z: z.transpose(0, 1, 3, 2).reshape(N_GROUPS, S5_COLS, SSM_STATE)
        return fix(re), fix(im)

    f_re, f_im = st(pw_re[0], pw_im[0], bb_re[0], bb_im[0], jnp.arange(t - 1, -1, -1))
    g_re, g_im = st(pw_re[1], pw_im[1], bb_re[1], bb_im[1], jnp.arange(t))
    m_st = jnp.concatenate([f_re, g_re, f_im, g_im], axis=-1)

    def outm(cp_r, cp_i, order):
        wr = cp_r[:, order]
        wi = cp_i[:, order]
        fix = lambda z: z.transpose(0, 3, 1, 2).reshape(N_GROUPS, SSM_STATE, S5_COLS)
        return fix(wr), fix(-wi)

    of_re, of_im = outm(cp_re[0], cp_im[0], jnp.arange(1, t + 1))
    ob_re, ob_im = outm(cp_re[1], cp_im[1], jnp.arange(t, 0, -1))
    m_out = jnp.concatenate([of_re, ob_re, of_im, ob_im], axis=1)

    a_re = jnp.concatenate([pw_re[0][..., t], pw_re[1][..., t]], axis=-1)[:, None, :]
    a_im = jnp.concatenate([pw_im[0][..., t], pw_im[1][..., t]], axis=-1)[:, None, :]
    toep = _s5_lane_order(_s5_lane_order(toep.astype(BF16), 1), 2)
    m_st = _s5_lane_order(m_st.astype(BF16), 1)
    m_out = _s5_lane_order(m_out.astype(BF16), 2)
    return toep, m_st, m_out, a_re, a_im


def _s5_lane_order(a, axis):
    a = jnp.moveaxis(a, axis, -1)
    lead = a.shape[1:-1]
    a = a.reshape(N_GROUPS // GROUPS_PER_COL, GROUPS_PER_COL, *lead,
                  S5_CHUNK // GROUPS_PER_COL, GROUPS_PER_COL, SSM_GROUP)
    a = jnp.stack([jnp.roll(a[:, k], k, axis=-2) for k in range(GROUPS_PER_COL)], axis=1)
    return jnp.moveaxis(a.reshape(N_GROUPS, *lead, S5_COLS), -1, axis)


def _s5_kernel(*refs, batches):
    n_in = len(batches)
    u_refs = refs[:n_in]
    toep_ref, mst_ref, mout_ref, are_ref, aim_ref = refs[n_in:n_in + 5]
    y_refs = refs[n_in + 5:2 * n_in + 5]
    st_ref, xin_ref = refs[2 * n_in + 5:]
    n_chunks = u_refs[0].shape[1]
    seqs = [(u_ref, y_ref, b) for u_ref, y_ref, nb in zip(u_refs, y_refs, batches) for b in range(nb)]

    def u_rows(u_ref, b):
        return u_ref[0, :, b * S5_COLS:(b + 1) * S5_COLS].astype(BF16)

    for slot in range(S5_ROWS):
        rows = pl.ds(slot, n_chunks, stride=S5_ROWS)
        if slot < len(seqs):
            st = _dot(u_rows(seqs[slot][0], seqs[slot][2]), mst_ref[0])
        else:
            st = jnp.zeros((n_chunks, S5_STATE_COLS), F32)
        st_ref[0, rows, :] = st[:, 0:LANES]
        st_ref[1, rows, :] = st[:, LANES:2 * LANES]

    a_re = jnp.broadcast_to(are_ref[0], (S5_ROWS, LANES))
    a_im = jnp.broadcast_to(aim_ref[0], (S5_ROWS, LANES))
    is_fwd = lax.broadcasted_iota(jnp.int32, (S5_ROWS, LANES), 1) < SSM_STATE
    half = SSM_STATE

    def body(i, carry):
        x_re, x_im = carry
        rf = pl.multiple_of(i * S5_ROWS, S5_ROWS)
        rb = pl.multiple_of((n_chunks - 1 - i) * S5_ROWS, S5_ROWS)
        xin_ref[0, pl.ds(rf, S5_ROWS), 0:half] = x_re[:, 0:half]
        xin_ref[0, pl.ds(rb, S5_ROWS), half:LANES] = x_re[:, half:LANES]
        xin_ref[1, pl.ds(rf, S5_ROWS), 0:half] = x_im[:, 0:half]
        xin_ref[1, pl.ds(rb, S5_ROWS), half:LANES] = x_im[:, half:LANES]
        s_re = jnp.where(is_fwd, st_ref[0, pl.ds(rf, S5_ROWS), :], st_ref[0, pl.ds(rb, S5_ROWS), :])
        s_im = jnp.where(is_fwd, st_ref[1, pl.ds(rf, S5_ROWS), :], st_ref[1, pl.ds(rb, S5_ROWS), :])
        n_re = a_re * x_re - a_im * x_im + s_re
        n_im = a_re * x_im + a_im * x_re + s_im
        return n_re, n_im

    zero = jnp.zeros((S5_ROWS, LANES), F32)
    lax.fori_loop(0, n_chunks, body, (zero, zero))

    for slot, (u_ref, y_ref, b) in enumerate(seqs):
        rows = pl.ds(slot, n_chunks, stride=S5_ROWS)
        x_in = jnp.concatenate([xin_ref[0, rows, :], xin_ref[1, rows, :]], axis=1).astype(BF16)
        y_ref[0, :, b * S5_COLS:(b + 1) * S5_COLS] = (_dot(u_rows(u_ref, b), toep_ref[0])
                                                      + _dot(x_in, mout_ref[0]))


def s5_scan(u_rows, toep, m_st, m_out, a_re, a_im):
    g, n_chunks, _ = u_rows[0].shape
    batches = tuple(u.shape[2] // S5_COLS for u in u_rows)
    assert sum(batches) <= S5_ROWS
    gmap = lambda gi: (gi, 0, 0)
    io_specs = [pl.BlockSpec((1, n_chunks, u.shape[2]), gmap) for u in u_rows]
    return pl.pallas_call(
        functools.partial(_s5_kernel, batches=batches),
        out_shape=tuple(jax.ShapeDtypeStruct(u.shape, F32) for u in u_rows),
        grid=(g,),
        in_specs=io_specs + [
            pl.BlockSpec((1, S5_COLS, S5_COLS), gmap),
            pl.BlockSpec((1, S5_COLS, S5_STATE_COLS), gmap),
            pl.BlockSpec((1, S5_STATE_COLS, S5_COLS), gmap),
            pl.BlockSpec((1, 1, LANES), gmap),
            pl.BlockSpec((1, 1, LANES), gmap),
        ],
        out_specs=tuple(io_specs),
        scratch_shapes=[pltpu.VMEM((2, n_chunks * S5_ROWS, LANES), F32),
                        pltpu.VMEM((2, n_chunks * S5_ROWS, LANES), F32)],
        compiler_params=_params(("parallel",)),
        name="s5_scan",
    )(*u_rows, toep, m_st, m_out, a_re, a_im)


def _gelu_tanh(x):
    return 0.5 * x * (1.0 + jnp.tanh(math.sqrt(2.0 / math.pi) * (x + 0.044715 * (x * x * x))))


def _post_mix_kernel(att_ref, ys_ref, u_ref, x_ref, mod_ref, dsk_ref, wglu_ref, bglu_ref,
                     ag_ref, sg_ref, wo_ref, g2_ref, x1_ref, h2_ref, y_cols):
    gate1 = mod_ref[0, 2:3, :]
    shift2 = mod_ref[0, 3:4, :]
    scale2 = mod_ref[0, 4:5, :]
    _from_chunk_rows(ys_ref, y_cols)
    y = jnp.concatenate([y_cols[j] for j in range(SSM_WIDTH // LANES)], axis=1)
    y = y + dsk_ref[...] * u_ref[0]
    y = _gelu_tanh(y)
    z = _dot(y.astype(BF16), wglu_ref[...]) + bglu_ref[...]
    ssm = y * jax.nn.sigmoid(z)
    att = att_ref[0].astype(F32)
    att_n = (att * _rms(att, ATT_WIDTH) * ag_ref[...]).astype(BF16)
    ssm_n = (ssm * _rms(ssm, SSM_WIDTH) * sg_ref[...]).astype(BF16)
    mix = _dot(att_n, wo_ref[0:ATT_WIDTH, :]) + _dot(ssm_n, wo_ref[ATT_WIDTH:2 * ATT_WIDTH, :])
    x1 = x_ref[0] + gate1 * mix
    x1_ref[0] = x1
    h2 = (x1 * _rms(x1, D_MODEL) * g2_ref[...]) * (1.0 + scale2) + shift2
    h2_ref[0] = h2.astype(BF16)


def post_mix(att, ys, u, x, mod, d_skip, w_glu, b_glu, att_g, ssm_g, w_o, g2):
    b, length, d = x.shape
    tm = S5_TILE
    const = lambda bi, i: (0, 0)
    tile = lambda w: pl.BlockSpec((1, tm, w), lambda bi, i: (bi, i, 0))
    return pl.pallas_call(
        _post_mix_kernel,
        out_shape=(jax.ShapeDtypeStruct((b, length, d), F32),
                   jax.ShapeDtypeStruct((b, length, d), BF16)),
        grid=(b, length // tm),
        in_specs=[
            tile(ATT_WIDTH),
            pl.BlockSpec((N_GROUPS, S5_SUB, S5_COLS), lambda bi, i: (0, i, bi)),
            tile(SSM_WIDTH), tile(d),
            pl.BlockSpec((1, 6, d), lambda bi, i: (bi, 0, 0)),
            pl.BlockSpec((1, SSM_WIDTH), const),
            pl.BlockSpec((SSM_WIDTH, SSM_WIDTH), const),
            pl.BlockSpec((1, SSM_WIDTH), const),
            pl.BlockSpec((1, ATT_WIDTH), const),
            pl.BlockSpec((1, SSM_WIDTH), const),
            pl.BlockSpec((d, d), const),
            pl.BlockSpec((1, d), const),
        ],
        out_specs=(tile(d), tile(d)),
        scratch_shapes=[pltpu.VMEM((SSM_WIDTH // LANES, tm, LANES), F32)],
        compiler_params=_params(("parallel", "parallel")),
        name="post_mix",
    )(att, ys, u, x, mod, d_skip, w_glu, b_glu, att_g, ssm_g, w_o, g2)


def _ffn_kernel(h_ref, x1_ref, mod_ref, w1_ref, w3_ref, w2_ref, o_ref):
    f = pl.program_id(2)
    h = h_ref[0]
    a = _dot(h, w1_ref[...])
    g = _dot(h, w3_ref[...])
    act = (a * jax.nn.sigmoid(a) * g).astype(BF16)
    part = _dot(act, w2_ref[...])

    @pl.when(f == 0)
    def _():
        o_ref[0] = part

    @pl.when(f > 0)
    def _():
        o_ref[0] += part

    @pl.when(f == pl.num_programs(2) - 1)
    def _():
        o_ref[0] = x1_ref[0] + mod_ref[0, 5:6, :] * o_ref[0]


def ffn(h2, x1, mod, w1, w3, w2, tm, tf):
    b, length, d = x1.shape
    tm = min(tm, length)
    return pl.pallas_call(
        _ffn_kernel,
        out_shape=jax.ShapeDtypeStruct((b, length, d), F32),
        grid=(b, length // tm, D_FF // tf),
        in_specs=[
            pl.BlockSpec((1, tm, d), lambda bi, i, f: (bi, i, 0)),
            pl.BlockSpec((1, tm, d), lambda bi, i, f: (bi, i, 0)),
            pl.BlockSpec((1, 6, d), lambda bi, i, f: (bi, 0, 0)),
            pl.BlockSpec((d, tf), lambda bi, i, f: (0, f)),
            pl.BlockSpec((d, tf), lambda bi, i, f: (0, f)),
            pl.BlockSpec((tf, d), lambda bi, i, f: (f, 0)),
        ],
        out_specs=pl.BlockSpec((1, tm, d), lambda bi, i, f: (bi, i, 0)),
        compiler_params=_params(("parallel", "parallel", "arbitrary")),
        name="ffn",
    )(h2, x1, mod, w1, w3, w2)


def _cast_kernel(x_ref, o_ref):
    o_ref[...] = x_ref[...].astype(o_ref.dtype)


def cast_bf16(w):
    rows, cols = w.shape
    tr = rows
    while tr * cols * 4 > CAST_BLOCK_BYTES and tr % (4 * SUBLANES) == 0:
        tr //= 2
    return pl.pallas_call(
        _cast_kernel,
        out_shape=jax.ShapeDtypeStruct((rows, cols), BF16),
        grid=(rows // tr,),
        in_specs=[pl.BlockSpec((tr, cols), lambda i: (i, 0))],
        out_specs=pl.BlockSpec((tr, cols), lambda i: (i, 0)),
        compiler_params=_params(("parallel",)),
        name="cast_bf16",
    )(w)


def _swap_halves(a, axis=-1):
    lo, hi = jnp.split(a, 2, axis=axis)
    return jnp.concatenate([hi, lo], axis=axis)


def _prep_w_in(w_in):
    d = w_in.shape[0]
    q_cols = N_HEADS * QK_DIM
    wq = w_in[:, :q_cols].reshape(d, N_HEADS, QK_DIM)
    wq_n = wq[:, :, :NOPE_DIM].reshape(d, N_HEADS * NOPE_DIM)
    wq_r = wq[:, :, NOPE_DIM:]
    wq_rr = jnp.concatenate([wq_r, _swap_halves(wq_r)], axis=-1).reshape(d, N_HEADS * 2 * ROPE_DIM)
    w_c = w_in[:, q_cols:q_cols + KV_RANK]
    w_kr = w_in[:, q_cols + KV_RANK:q_cols + KV_RANK + ROPE_DIM]
    w_krr = jnp.concatenate([w_kr, _swap_halves(w_kr)], axis=-1)
    w_u = w_in[:, q_cols + KV_RANK + ROPE_DIM:]
    return jnp.concatenate([wq_n, wq_rr, w_c, w_krr, w_u], axis=-1).astype(BF16)


def _prep_w_ukv(w_ukv):
    w = w_ukv.reshape(KV_RANK, N_HEADS, NOPE_DIM + V_DIM)
    wk = w[:, :, :NOPE_DIM].reshape(KV_RANK, N_HEADS * NOPE_DIM)
    wv = w[:, :, NOPE_DIM:].reshape(KV_RANK, N_HEADS * V_DIM)
    return jnp.concatenate([wk, wv], axis=-1).astype(BF16)


def _rope_gain(g):
    gr = g[NOPE_DIM:]
    return jnp.concatenate([gr, _swap_halves(gr)]).reshape(1, 2 * ROPE_DIM)


def _rope_tables(length):
    pos = jnp.arange(length, dtype=F32)
    inv_freq = ROPE_BASE ** (-jnp.arange(0, ROPE_DIM, 2, dtype=F32) / ROPE_DIM)
    ang = pos[:, None] * inv_freq[None, :]
    cos, sin = jnp.cos(ang), jnp.sin(ang)
    return jnp.tile(cos, (1, 4)), jnp.tile(jnp.concatenate([-sin, sin], axis=-1), (1, 2))


def kernel(x_prompt, x_sample, c_prompt, c_sample, w_ada, b_ada, norm_mix_g, w_in, kv_norm_g, w_ukv,
           q_norm_g, k_norm_g, lam_re, lam_im, log_dt, b_re, b_im, c_re, c_im, d_skip, w_glu, b_glu,
           att_out_g, ssm_out_g, w_o, norm_ffn_g, w1, w3, w2):
    assert w_ada.shape[0] == 1, "single-layer kernel"
    xs = (x_prompt, x_sample)
    batches = tuple(x.shape[0] for x in xs)
    length = x_prompt.shape[1]
    assert x_sample.shape[1] == length and sum(batches) <= S5_ROWS

    c_all = jnp.concatenate([c_prompt, c_sample], axis=0)
    c8 = jnp.pad(c_all, ((0, SUBLANES - c_all.shape[0]), (0, 0)))
    mod_all = ada_mod(c8, w_ada[0], b_ada[0]).reshape(SUBLANES, 6, D_MODEL)
    mods = (mod_all[:batches[0]], mod_all[batches[0]:batches[0] + batches[1]])

    w_all = _prep_w_in(w_in[0])
    w_ukv_p = _prep_w_ukv(w_ukv[0])
    row = lambda a: a.reshape(1, -1)
    q_gn, k_gn = row(q_norm_g[0][:NOPE_DIM]), row(k_norm_g[0][:NOPE_DIM])
    q_gr, k_gr = _rope_gain(q_norm_g[0]), _rope_gain(k_norm_g[0])
    cos_t, sin_t = _rope_tables(length)
    s5_ops = s5_matrices(lam_re[0], lam_im[0], log_dt[0], b_re[0], b_im[0], c_re[0], c_im[0])
    w_glu_b, w_o_b = cast_bf16(w_glu[0]), cast_bf16(w_o[0])
    w1_b, w3_b, w2_b = cast_bf16(w1[0]), cast_bf16(w3[0]), cast_bf16(w2[0])

    proj = [in_proj(x, m, row(norm_mix_g[0]), w_all, w_ukv_p, row(kv_norm_g[0]), q_gn, k_gn, q_gr, k_gr,
                    cos_t, sin_t) for x, m in zip(xs, mods)]
    atts = [attention(q, k, v, tq=512, tk=512) for q, k, v, _, _ in proj]
    y_rows = s5_scan([p[4] for p in proj], *s5_ops)

    outs = []
    for x, m, att, ys, p in zip(xs, mods, atts, y_rows, proj):
        x1, h2 = post_mix(att, ys, p[3], x, m, row(d_skip[0]), w_glu_b, row(b_glu[0]), row(att_out_g[0]),
                          row(ssm_out_g[0]), w_o_b, row(norm_ffn_g[0]))
        outs.append(ffn(h2, x1, m, w1_b, w3_b, w2_b, tm=512, tf=512))
    return tuple(outs)
```

```python
import functools
import math

import jax
import jax.numpy as jnp
from jax import lax
from jax.experimental import pallas as pl
from jax.experimental.pallas import tpu as pltpu

F32 = jnp.float32
BF16 = jnp.bfloat16

D_MODEL = 2048
ATT_WIDTH = 1024
SSM_WIDTH = 1024
N_HEADS = 8
V_DIM = 128
NOPE_DIM = 128
ROPE_DIM = 64
QK_DIM = NOPE_DIM + ROPE_DIM
KV_RANK = 512
ROPE_BASE = 10000.0
SSM_GROUP = 16
N_GROUPS = SSM_WIDTH // SSM_GROUP
SSM_STATE = 64
D_FF = 5632
NORM_EPS = 1e-6

LANES = 128
SUBLANES = 8
VMEM_LIMIT = 56 * 1024 * 1024
CAST_BLOCK_BYTES = 4 * 1024 * 1024

ATT_TQ, ATT_TK = 512, 512
FFN_UP_TM, FFN_UP_TF = 1024, 512
FFN_DOWN_TM, FFN_DOWN_TN = 512, 1024
W_PREP_ROWS = 256

C_QN = 0
C_QR = C_QN + N_HEADS * NOPE_DIM
C_KV = C_QR + N_HEADS * 2 * ROPE_DIM
C_KR = C_KV + KV_RANK
C_U = C_KR + 2 * ROPE_DIM
C_END = C_U + SSM_WIDTH

S5_CHUNK = 32
S5_ROWS = SUBLANES
S5_SUB = SUBLANES
S5_TILE = S5_SUB * S5_CHUNK
S5_COLS = S5_CHUNK * SSM_GROUP
S5_STATE_COLS = 4 * SSM_STATE
GROUPS_PER_COL = LANES // SSM_GROUP


def _params(sem):
    return pltpu.CompilerParams(dimension_semantics=sem, vmem_limit_bytes=VMEM_LIMIT)


def _dot(a, b):
    return jnp.dot(a, b, preferred_element_type=F32)


def _dot_f32(a, b):
    a_hi = a.astype(BF16)
    a_lo = (a - a_hi.astype(F32)).astype(BF16)
    b_hi = b.astype(BF16)
    b_lo = (b - b_hi.astype(F32)).astype(BF16)
    return _dot(a_hi, b_hi) + _dot(a_lo, b_hi) + _dot(a_hi, b_lo)


def _ada_kernel(c_ref, w_ref, b_ref, o_ref):
    c = c_ref[...]
    o_ref[...] = _dot_f32(c * jax.nn.sigmoid(c), w_ref[...]) + b_ref[...]


def ada_mod(c8, w_ada, b_ada):
    rows, d = c8.shape
    n = w_ada.shape[1]
    tn = 1024
    return pl.pallas_call(
        _ada_kernel,
        out_shape=jax.ShapeDtypeStruct((rows, n), F32),
        grid=(n // tn,),
        in_specs=[
            pl.BlockSpec((rows, d), lambda j: (0, 0)),
            pl.BlockSpec((d, tn), lambda j: (0, j)),
            pl.BlockSpec((1, tn), lambda j: (0, j)),
        ],
        out_specs=pl.BlockSpec((rows, tn), lambda j: (0, j)),
        compiler_params=_params(("arbitrary",)),
        name="ada_mod",
    )(c8, w_ada, b_ada.reshape(1, n))


def _rms(x, width):
    return lax.rsqrt(jnp.sum(x * x, axis=-1, keepdims=True) * (1.0 / width) + NORM_EPS)


def _granule_masks():
    pos = lax.broadcasted_iota(jnp.int32, (SUBLANES, LANES), 1) // SSM_GROUP
    return [pos == p for p in range(GROUPS_PER_COL)]


def _to_chunk_rows(u_cols, dst):
    masks = _granule_masks()
    for j in range(SSM_WIDTH // LANES):
        rolled = []
        for t in range(S5_CHUNK):
            v = u_cols[j, pl.ds(t, S5_SUB, stride=S5_CHUNK), :]
            r = t % GROUPS_PER_COL
            rolled.append(pltpu.roll(v, SSM_GROUP * r, axis=1) if r else v)
        for k in range(GROUPS_PER_COL):
            for d in range(S5_COLS // LANES):
                acc = rolled[GROUPS_PER_COL * d + GROUPS_PER_COL - 1]
                for rho in range(GROUPS_PER_COL - 2, -1, -1):
                    acc = jnp.where(masks[(k + rho) % GROUPS_PER_COL], rolled[GROUPS_PER_COL * d + rho], acc)
                dst[GROUPS_PER_COL * j + k, :, d * LANES:(d + 1) * LANES] = acc


def _from_chunk_rows(src, y_cols):
    masks = _granule_masks()
    for j in range(SSM_WIDTH // LANES):
        for d in range(S5_COLS // LANES):
            cols = [src[GROUPS_PER_COL * j + k, :, d * LANES:(d + 1) * LANES] for k in range(GROUPS_PER_COL)]
            for rho in range(GROUPS_PER_COL):
                acc = cols[GROUPS_PER_COL - 1]
                for k in range(GROUPS_PER_COL - 2, -1, -1):
                    acc = jnp.where(masks[(k + rho) % GROUPS_PER_COL], cols[k], acc)
                v = pltpu.roll(acc, LANES - SSM_GROUP * rho, axis=1) if rho else acc
                y_cols[j, pl.ds(GROUPS_PER_COL * d + rho, S5_SUB, stride=S5_CHUNK), :] = v


def _in_proj_kernel(x_ref, mod_ref, g1_ref, w_ref, wukv_ref, kvg_ref, qg_ref, kg_ref,
                    qgr_ref, kgr_ref, cos_ref, sin_ref, q_ref, k_ref, v_ref, u_ref, ur_ref, u_cols,
                    *, q_scale):
    x = x_ref[0]
    shift = mod_ref[0, 0:1, :]
    scale = mod_ref[0, 1:2, :]
    h = (x * _rms(x, D_MODEL) * g1_ref[...]) * (1.0 + scale) + shift
    h = h.astype(BF16)

    cos = cos_ref[...]
    sin = sin_ref[...]

    def rope(col, gain):
        cg = col * gain
        return cg * cos + pltpu.roll(cg, ROPE_DIM, axis=1) * sin

    qn = _dot(h, w_ref[:, C_QN:C_QR])
    qr = _dot(h, w_ref[:, C_QR:C_KV])
    qg_n = qg_ref[...]
    qg_r = qgr_ref[...]
    for hd in range(N_HEADS):
        qn_h = qn[:, hd * LANES:(hd + 1) * LANES]
        qr_h = qr[:, hd * LANES:(hd + 1) * LANES]
        ss = (jnp.sum(qn_h * qn_h, axis=-1, keepdims=True)
              + 0.5 * jnp.sum(qr_h * qr_h, axis=-1, keepdims=True))
        r = lax.rsqrt(ss * (1.0 / QK_DIM) + NORM_EPS) * q_scale
        q_ref[0, hd, :, 0:NOPE_DIM] = (qn_h * r * qg_n).astype(BF16)
        q_ref[0, hd, :, NOPE_DIM:QK_DIM] = (rope(qr_h, qg_r) * r)[:, 0:ROPE_DIM].astype(BF16)

    ckv = _dot(h, w_ref[:, C_KV:C_KR])
    ckv_n = (ckv * _rms(ckv, KV_RANK) * kvg_ref[...]).astype(BF16)
    kv = _dot(ckv_n, wukv_ref[...])
    kr = _dot(h, w_ref[:, C_KR:C_U])
    kr_ss = 0.5 * jnp.sum(kr * kr, axis=-1, keepdims=True)
    kr_rot = rope(kr, kgr_ref[...])
    kg_n = kg_ref[...]
    ones = jnp.ones((x.shape[0], LANES), BF16)
    for hd in range(N_HEADS):
        kn_h = kv[:, hd * LANES:(hd + 1) * LANES]
        ss = jnp.sum(kn_h * kn_h, axis=-1, keepdims=True) + kr_ss
        r = lax.rsqrt(ss * (1.0 / QK_DIM) + NORM_EPS)
        k_ref[0, hd, :, 0:NOPE_DIM] = (kn_h * r * kg_n).astype(BF16)
        k_ref[0, hd, :, NOPE_DIM:QK_DIM] = (kr_rot * r)[:, 0:ROPE_DIM].astype(BF16)
        v_h = kv[:, ATT_WIDTH + hd * LANES:ATT_WIDTH + (hd + 1) * LANES]
        v_ref[0, hd, :, 0:V_DIM] = v_h.astype(BF16)
        v_ref[0, hd, :, V_DIM:2 * V_DIM] = ones

    u = _dot(h, w_ref[:, C_U:C_END])
    u_ref[0] = u
    for j in range(SSM_WIDTH // LANES):
        u_cols[j] = u[:, j * LANES:(j + 1) * LANES]
    _to_chunk_rows(u_cols, ur_ref)


def in_proj(x, mod, g1, w_all, w_ukv, kv_g, q_gn, k_gn, q_gr, k_gr, cos_t, sin_t):
    b, length, d = x.shape
    tm = S5_TILE
    assert length % tm == 0
    const = lambda bi, i: (0, 0)
    kern = functools.partial(_in_proj_kernel, q_scale=math.log2(math.e) / math.sqrt(QK_DIM))
    return pl.pallas_call(
        kern,
        out_shape=(
            jax.ShapeDtypeStruct((b, N_HEADS, length, QK_DIM), BF16),
            jax.ShapeDtypeStruct((b, N_HEADS, length, QK_DIM), BF16),
            jax.ShapeDtypeStruct((b, N_HEADS, length, 2 * V_DIM), BF16),
            jax.ShapeDtypeStruct((b, length, SSM_WIDTH), F32),
            jax.ShapeDtypeStruct((N_GROUPS, length // S5_CHUNK, b * S5_COLS), F32),
        ),
        grid=(b, length // tm),
        in_specs=[
            pl.BlockSpec((1, tm, d), lambda bi, i: (bi, i, 0)),
            pl.BlockSpec((1, 6, d), lambda bi, i: (bi, 0, 0)),
            pl.BlockSpec((1, d), const),
            pl.BlockSpec((d, C_END), const),
            pl.BlockSpec((KV_RANK, 2 * ATT_WIDTH), const),
            pl.BlockSpec((1, KV_RANK), const),
            pl.BlockSpec((1, LANES), const),
            pl.BlockSpec((1, LANES), const),
            pl.BlockSpec((1, LANES), const),
            pl.BlockSpec((1, LANES), const),
            pl.BlockSpec((tm, LANES), lambda bi, i: (i, 0)),
            pl.BlockSpec((tm, LANES), lambda bi, i: (i, 0)),
        ],
        out_specs=(
            pl.BlockSpec((1, N_HEADS, tm, QK_DIM), lambda bi, i: (bi, 0, i, 0)),
            pl.BlockSpec((1, N_HEADS, tm, QK_DIM), lambda bi, i: (bi, 0, i, 0)),
            pl.BlockSpec((1, N_HEADS, tm, 2 * V_DIM), lambda bi, i: (bi, 0, i, 0)),
            pl.BlockSpec((1, tm, SSM_WIDTH), lambda bi, i: (bi, i, 0)),
            pl.BlockSpec((N_GROUPS, S5_SUB, S5_COLS), lambda bi, i: (0, i, bi)),
        ),
        scratch_shapes=[pltpu.VMEM((SSM_WIDTH // LANES, tm, LANES), F32)],
        compiler_params=_params(("parallel", "parallel")),
        name="in_proj",
    )(x, mod, g1, w_all, w_ukv, kv_g, q_gn, k_gn, q_gr, k_gr, cos_t, sin_t)


def _attn_kernel(q_ref, k_ref, v_ref, o_ref, *, tk):
    q = q_ref[0, 0]
    tq = q.shape[0]
    m = jnp.full((tq, 1), -jnp.inf, F32)
    acc = jnp.zeros((tq, 2 * V_DIM), F32)
    for j in range(k_ref.shape[2] // tk):
        kc = k_ref[0, 0, j * tk:(j + 1) * tk, :]
        vc = v_ref[0, 0, j * tk:(j + 1) * tk, :]
        s = lax.dot_general(q, kc, (((1,), (1,)), ((), ())), preferred_element_type=F32)
        m_new = jnp.maximum(m, jnp.max(s, axis=-1, keepdims=True))
        alpha = jnp.exp2(m - m_new)
        p = jnp.exp2(s - m_new).astype(BF16)
        acc = alpha * acc + _dot(p, vc)
        m = m_new
    o_ref[0] = (acc[:, 0:V_DIM] / acc[:, V_DIM:2 * V_DIM]).astype(o_ref.dtype)


def attention(q, k, v):
    b, h, length, _ = q.shape
    tq = min(ATT_TQ, length)
    tk = min(ATT_TK, length)
    return pl.pallas_call(
        functools.partial(_attn_kernel, tk=tk),
        out_shape=jax.ShapeDtypeStruct((b, length, h * V_DIM), BF16),
        grid=(b, h, length // tq),
        in_specs=[
            pl.BlockSpec((1, 1, tq, QK_DIM), lambda bi, hi, i: (bi, hi, i, 0)),
            pl.BlockSpec((1, 1, length, QK_DIM), lambda bi, hi, i: (bi, hi, 0, 0)),
            pl.BlockSpec((1, 1, length, 2 * V_DIM), lambda bi, hi, i: (bi, hi, 0, 0)),
        ],
        out_specs=pl.BlockSpec((1, tq, V_DIM), lambda bi, hi, i: (bi, i, hi)),
        compiler_params=_params(("parallel", "parallel", "parallel")),
        name="attention",
    )(q, k, v)


def s5_matrices(lam_re, lam_im, log_dt, b_re, b_im, c_re, c_im):
    t = S5_CHUNK
    dt = jnp.exp(log_dt)[..., None]
    mag1 = jnp.exp(lam_re * dt)
    ang1 = lam_im * dt
    lb_re, lb_im = mag1 * jnp.cos(ang1), mag1 * jnp.sin(ang1)
    den = lam_re * lam_re + lam_im * lam_im
    n_re, n_im = lb_re - 1.0, lb_im
    cf_re = (n_re * lam_re + n_im * lam_im) / den
    cf_im = (n_im * lam_re - n_re * lam_im) / den
    bb_re = cf_re[..., None] * b_re[None] - cf_im[..., None] * b_im[None]
    bb_im = cf_re[..., None] * b_im[None] + cf_im[..., None] * b_re[None]

    kk = jnp.arange(-t, t + 1, dtype=F32)
    mag = jnp.exp(lam_re[..., None] * dt[..., None] * kk)
    ang = ang1[..., None] * kk
    pw_re, pw_im = mag * jnp.cos(ang), mag * jnp.sin(ang)
    steps = jnp.arange(t)

    def rows_op(d, ks):
        pr = jnp.moveaxis(pw_re[d][..., ks + t], -1, 1)[:, :, None, :]
        pi = jnp.moveaxis(pw_im[d][..., ks + t], -1, 1)[:, :, None, :]
        br = jnp.swapaxes(bb_re[d], 1, 2)[:, None]
        bi = jnp.swapaxes(bb_im[d], 1, 2)[:, None]
        flat = lambda z: z.reshape(N_GROUPS, S5_COLS, SSM_STATE)
        return flat(pr * br - pi * bi), flat(pr * bi + pi * br)

    def cols_op(d, ks):
        pr = pw_re[d][..., ks + t][..., None]
        pi = pw_im[d][..., ks + t][..., None]
        cr = jnp.swapaxes(c_re[d], 1, 2)[:, :, None, :]
        ci = jnp.swapaxes(c_im[d], 1, 2)[:, :, None, :]
        flat = lambda z: z.reshape(N_GROUPS, SSM_STATE, S5_COLS)
        return flat(pr * cr - pi * ci), flat(pr * ci + pi * cr)

    lf_re, lf_im = rows_op(0, -steps)
    lb_re_, lb_im_ = rows_op(1, steps)
    sf_re, sf_im = rows_op(0, t - 1 - steps)
    rf_re, rf_im = cols_op(0, steps)
    rb_re, rb_im = cols_op(1, -steps)
    of_re, of_im = cols_op(0, steps + 1)
    ob_re, ob_im = cols_op(1, t - steps)

    rows = lambda z: _s5_lane_order(z, 1)
    cols = lambda z: _s5_lane_order(z, 2)
    l_f = rows(jnp.concatenate([lf_re, -lf_im], axis=-1))
    l_b = rows(jnp.concatenate([lb_re_, -lb_im_], axis=-1))
    r_f = cols(jnp.concatenate([rf_re, rf_im], axis=1))
    r_b = cols(jnp.concatenate([rb_re, rb_im], axis=1))
    m_st = rows(jnp.concatenate([sf_re, lb_re_, sf_im, lb_im_], axis=-1).astype(BF16))
    m_out = cols(jnp.concatenate([of_re, ob_re, -of_im, -ob_im], axis=1).astype(BF16))
    a_re = jnp.concatenate([pw_re[0][..., 2 * t], pw_re[1][..., 2 * t]], axis=-1)[:, None, :]
    a_im = jnp.concatenate([pw_im[0][..., 2 * t], pw_im[1][..., 2 * t]], axis=-1)[:, None, :]
    return l_f, r_f, l_b, r_b, m_st, m_out, a_re, a_im


def _s5_lane_order(a, axis):
    a = jnp.moveaxis(a, axis, -1)
    lead = a.shape[1:-1]
    a = a.reshape(N_GROUPS // GROUPS_PER_COL, GROUPS_PER_COL, *lead,
                  S5_CHUNK // GROUPS_PER_COL, GROUPS_PER_COL, SSM_GROUP)
    a = jnp.stack([jnp.roll(a[:, k], k, axis=-2) for k in range(GROUPS_PER_COL)], axis=1)
    return jnp.moveaxis(a.reshape(N_GROUPS, *lead, S5_COLS), -1, axis)


def _s5_kernel(*refs, batches):
    n_in = len(batches)
    u_refs = refs[:n_in]
    lf_ref, rf_ref, lb_ref, rb_ref, mst_ref, mout_ref, are_ref, aim_ref = refs[n_in:n_in + 8]
    y_refs = refs[n_in + 8:2 * n_in + 8]
    st_ref, xin_ref = refs[2 * n_in + 8:]
    n_chunks = u_refs[0].shape[1]

    def step_of(idx):
        k = pl.program_id(0) % GROUPS_PER_COL
        d = lax.shift_right_logical(idx, LANES.bit_length() - 1)
        pos = lax.shift_right_logical(idx, SSM_GROUP.bit_length() - 1) & (GROUPS_PER_COL - 1)
        return d * GROUPS_PER_COL + ((pos - k) & (GROUPS_PER_COL - 1))

    s_row = step_of(lax.broadcasted_iota(jnp.int32, (S5_COLS, S5_COLS), 0))
    t_col = step_of(lax.broadcasted_iota(jnp.int32, (S5_COLS, S5_COLS), 1))
    toep = (jnp.where(t_col >= s_row, _dot_f32(lf_ref[0], rf_ref[0]), 0.0)
            + jnp.where(t_col <= s_row, _dot_f32(lb_ref[0], rb_ref[0]), 0.0)).astype(BF16)
    seqs = [(u_ref, y_ref, b) for u_ref, y_ref, nb in zip(u_refs, y_refs, batches) for b in range(nb)]

    def u_rows(u_ref, b):
        return u_ref[0, :, b * S5_COLS:(b + 1) * S5_COLS].astype(BF16)

    for slot in range(S5_ROWS):
        rows = pl.ds(slot, n_chunks, stride=S5_ROWS)
        if slot < len(seqs):
            st = _dot(u_rows(seqs[slot][0], seqs[slot][2]), mst_ref[0])
        else:
            st = jnp.zeros((n_chunks, S5_STATE_COLS), F32)
        st_ref[0, rows, :] = st[:, 0:LANES]
        st_ref[1, rows, :] = st[:, LANES:2 * LANES]

    a_re = jnp.broadcast_to(are_ref[0], (S5_ROWS, LANES))
    a_im = jnp.broadcast_to(aim_ref[0], (S5_ROWS, LANES))
    is_fwd = lax.broadcasted_iota(jnp.int32, (S5_ROWS, LANES), 1) < SSM_STATE
    half = SSM_STATE

    def body(i, carry):
        x_re, x_im = carry
        rf = pl.multiple_of(i * S5_ROWS, S5_ROWS)
        rb = pl.multiple_of((n_chunks - 1 - i) * S5_ROWS, S5_ROWS)
        xin_ref[0, pl.ds(rf, S5_ROWS), 0:half] = x_re[:, 0:half]
        xin_ref[0, pl.ds(rb, S5_ROWS), half:LANES] = x_re[:, half:LANES]
        xin_ref[1, pl.ds(rf, S5_ROWS), 0:half] = x_im[:, 0:half]
        xin_ref[1, pl.ds(rb, S5_ROWS), half:LANES] = x_im[:, half:LANES]
        s_re = jnp.where(is_fwd, st_ref[0, pl.ds(rf, S5_ROWS), :], st_ref[0, pl.ds(rb, S5_ROWS), :])
        s_im = jnp.where(is_fwd, st_ref[1, pl.ds(rf, S5_ROWS), :], st_ref[1, pl.ds(rb, S5_ROWS), :])
        n_re = a_re * x_re - a_im * x_im + s_re
        n_im = a_re * x_im + a_im * x_re + s_im
        return n_re, n_im

    zero = jnp.zeros((S5_ROWS, LANES), F32)
    lax.fori_loop(0, n_chunks, body, (zero, zero))

    for slot, (u_ref, y_ref, b) in enumerate(seqs):
        rows = pl.ds(slot, n_chunks, stride=S5_ROWS)
        x_in = jnp.concatenate([xin_ref[0, rows, :], xin_ref[1, rows, :]], axis=1).astype(BF16)
        y_ref[0, :, b * S5_COLS:(b + 1) * S5_COLS] = _dot(u_rows(u_ref, b), toep) + _dot(x_in, mout_ref[0])


def s5_scan(u_rows, l_f, r_f, l_b, r_b, m_st, m_out, a_re, a_im):
    g, n_chunks, _ = u_rows[0].shape
    batches = tuple(u.shape[2] // S5_COLS for u in u_rows)
    assert sum(batches) <= S5_ROWS
    gmap = lambda gi: (gi, 0, 0)
    io_specs = [pl.BlockSpec((1, n_chunks, u.shape[2]), gmap) for u in u_rows]
    return pl.pallas_call(
        functools.partial(_s5_kernel, batches=batches),
        out_shape=tuple(jax.ShapeDtypeStruct(u.shape, F32) for u in u_rows),
        grid=(g,),
        in_specs=io_specs + [
            pl.BlockSpec((1, S5_COLS, 2 * SSM_STATE), gmap),
            pl.BlockSpec((1, 2 * SSM_STATE, S5_COLS), gmap),
            pl.BlockSpec((1, S5_COLS, 2 * SSM_STATE), gmap),
            pl.BlockSpec((1, 2 * SSM_STATE, S5_COLS), gmap),
            pl.BlockSpec((1, S5_COLS, S5_STATE_COLS), gmap),
            pl.BlockSpec((1, S5_STATE_COLS, S5_COLS), gmap),
            pl.BlockSpec((1, 1, LANES), gmap),
            pl.BlockSpec((1, 1, LANES), gmap),
        ],
        out_specs=tuple(io_specs),
        scratch_shapes=[pltpu.VMEM((2, n_chunks * S5_ROWS, LANES), F32),
                        pltpu.VMEM((2, n_chunks * S5_ROWS, LANES), F32)],
        compiler_params=_params(("parallel",)),
        name="s5_scan",
    )(*u_rows, l_f, r_f, l_b, r_b, m_st, m_out, a_re, a_im)


def _gelu_tanh(x):
    return 0.5 * x * (1.0 + jnp.tanh(math.sqrt(2.0 / math.pi) * (x + 0.044715 * (x * x * x))))


def _post_mix_kernel(att_ref, ys_ref, u_ref, x_ref, mod_ref, dsk_ref, wglu_ref, bglu_ref,
                     ag_ref, sg_ref, wo_ref, g2_ref, x1_ref, h2_ref, y_cols):
    gate1 = mod_ref[0, 2:3, :]
    shift2 = mod_ref[0, 3:4, :]
    scale2 = mod_ref[0, 4:5, :]
    _from_chunk_rows(ys_ref, y_cols)
    y = jnp.concatenate([y_cols[j] for j in range(SSM_WIDTH // LANES)], axis=1)
    y = y + dsk_ref[...] * u_ref[0]
    y = _gelu_tanh(y)
    z = _dot(y.astype(BF16), wglu_ref[...]) + bglu_ref[...]
    ssm = y * jax.nn.sigmoid(z)
    att = att_ref[0].astype(F32)
    att_n = (att * _rms(att, ATT_WIDTH) * ag_ref[...]).astype(BF16)
    ssm_n = (ssm * _rms(ssm, SSM_WIDTH) * sg_ref[...]).astype(BF16)
    mix = _dot(att_n, wo_ref[0:ATT_WIDTH, :]) + _dot(ssm_n, wo_ref[ATT_WIDTH:2 * ATT_WIDTH, :])
    x1 = x_ref[0] + gate1 * mix
    x1_ref[0] = x1
    h2 = (x1 * _rms(x1, D_MODEL) * g2_ref[...]) * (1.0 + scale2) + shift2
    h2_ref[0] = h2.astype(BF16)


def post_mix(att, ys, u, x, mod, d_skip, w_glu, b_glu, att_g, ssm_g, w_o, g2):
    b, length, d = x.shape
    tm = S5_TILE
    const = lambda bi, i: (0, 0)
    tile = lambda w: pl.BlockSpec((1, tm, w), lambda bi, i: (bi, i, 0))
    return pl.pallas_call(
        _post_mix_kernel,
        out_shape=(jax.ShapeDtypeStruct((b, length, d), F32),
                   jax.ShapeDtypeStruct((b, length, d), BF16)),
        grid=(b, length // tm),
        in_specs=[
            tile(ATT_WIDTH),
            pl.BlockSpec((N_GROUPS, S5_SUB, S5_COLS), lambda bi, i: (0, i, bi)),
            tile(SSM_WIDTH), tile(d),
            pl.BlockSpec((1, 6, d), lambda bi, i: (bi, 0, 0)),
            pl.BlockSpec((1, SSM_WIDTH), const),
            pl.BlockSpec((SSM_WIDTH, SSM_WIDTH), const),
            pl.BlockSpec((1, SSM_WIDTH), const),
            pl.BlockSpec((1, ATT_WIDTH), const),
            pl.BlockSpec((1, SSM_WIDTH), const),
            pl.BlockSpec((d, d), const),
            pl.BlockSpec((1, d), const),
        ],
        out_specs=(tile(d), tile(d)),
        scratch_shapes=[pltpu.VMEM((SSM_WIDTH // LANES, tm, LANES), F32)],
        compiler_params=_params(("parallel", "parallel")),
        name="post_mix",
    )(att, ys, u, x, mod, d_skip, w_glu, b_glu, att_g, ssm_g, w_o, g2)


def _ffn_up_kernel(h_ref, w1_ref, w3_ref, o_ref):
    h = h_ref[0]
    a = _dot(h, w1_ref[...])
    g = _dot(h, w3_ref[...])
    o_ref[0] = (a * jax.nn.sigmoid(a) * g).astype(BF16)


def ffn_up(h2, w1, w3):
    b, length, d = h2.shape
    tm, tf = min(FFN_UP_TM, length), FFN_UP_TF
    return pl.pallas_call(
        _ffn_up_kernel,
        out_shape=jax.ShapeDtypeStruct((b, length, D_FF), BF16),
        grid=(D_FF // tf, b, length // tm),
        in_specs=[
            pl.BlockSpec((1, tm, d), lambda f, bi, i: (bi, i, 0)),
            pl.BlockSpec((d, tf), lambda f, bi, i: (0, f)),
            pl.BlockSpec((d, tf), lambda f, bi, i: (0, f)),
        ],
        out_specs=pl.BlockSpec((1, tm, tf), lambda f, bi, i: (bi, i, f)),
        compiler_params=_params(("parallel", "parallel", "parallel")),
        name="ffn_up",
    )(h2, w1, w3)


def _ffn_down_kernel(a_ref, x1_ref, mod_ref, w2_ref, o_ref):
    o_ref[0] = x1_ref[0] + mod_ref[0, 5:6, :] * _dot(a_ref[0], w2_ref[...])


def ffn_down(act, x1, mod, w2):
    b, length, d = x1.shape
    tm, tn = min(FFN_DOWN_TM, length), FFN_DOWN_TN
    return pl.pallas_call(
        _ffn_down_kernel,
        out_shape=jax.ShapeDtypeStruct((b, length, d), F32),
        grid=(d // tn, b, length // tm),
        in_specs=[
            pl.BlockSpec((1, tm, D_FF), lambda n, bi, i: (bi, i, 0)),
            pl.BlockSpec((1, tm, tn), lambda n, bi, i: (bi, i, n)),
            pl.BlockSpec((1, 6, tn), lambda n, bi, i: (bi, 0, n)),
            pl.BlockSpec((D_FF, tn), lambda n, bi, i: (0, n)),
        ],
        out_specs=pl.BlockSpec((1, tm, tn), lambda n, bi, i: (bi, i, n)),
        compiler_params=_params(("parallel", "parallel", "parallel")),
        name="ffn_down",
    )(act, x1, mod, w2)


def _cast_kernel(x_ref, o_ref):
    o_ref[...] = x_ref[...].astype(o_ref.dtype)


def cast_bf16(w):
    rows, cols = w.shape
    tr = rows
    while tr * cols * 4 > CAST_BLOCK_BYTES and tr % (4 * SUBLANES) == 0:
        tr //= 2
    return pl.pallas_call(
        _cast_kernel,
        out_shape=jax.ShapeDtypeStruct((rows, cols), BF16),
        grid=(rows // tr,),
        in_specs=[pl.BlockSpec((tr, cols), lambda i: (i, 0))],
        out_specs=pl.BlockSpec((tr, cols), lambda i: (i, 0)),
        compiler_params=_params(("parallel",)),
        name="cast_bf16",
    )(w)


def _swap_halves(a, axis=-1):
    lo, hi = jnp.split(a, 2, axis=axis)
    return jnp.concatenate([hi, lo], axis=axis)


def _prep_w_in_kernel(w_ref, o_ref):
    w = w_ref[...]
    q_cols = N_HEADS * QK_DIM
    half = ROPE_DIM // 2

    def with_swapped(r):
        return [r, r[:, half:], r[:, :half]]

    cols = [w[:, hd * QK_DIM:hd * QK_DIM + NOPE_DIM] for hd in range(N_HEADS)]
    for hd in range(N_HEADS):
        cols += with_swapped(w[:, hd * QK_DIM + NOPE_DIM:(hd + 1) * QK_DIM])
    cols.append(w[:, q_cols:q_cols + KV_RANK])
    cols += with_swapped(w[:, q_cols + KV_RANK:q_cols + KV_RANK + ROPE_DIM])
    cols.append(w[:, q_cols + KV_RANK + ROPE_DIM:])
    o_ref[...] = jnp.concatenate(cols, axis=1).astype(BF16)


def prep_w_in(w_in):
    d, n = w_in.shape
    return pl.pallas_call(
        _prep_w_in_kernel,
        out_shape=jax.ShapeDtypeStruct((d, C_END), BF16),
        grid=(d // W_PREP_ROWS,),
        in_specs=[pl.BlockSpec((W_PREP_ROWS, n), lambda i: (i, 0))],
        out_specs=pl.BlockSpec((W_PREP_ROWS, C_END), lambda i: (i, 0)),
        compiler_params=_params(("parallel",)),
        name="prep_w_in",
    )(w_in)


def prep_w_ukv(w_ukv):
    rank, n = w_ukv.shape
    src = lambda j: (0, 2 * (j % N_HEADS) + j // N_HEADS)
    return pl.pallas_call(
        _cast_kernel,
        out_shape=jax.ShapeDtypeStruct((rank, n), BF16),
        grid=(n // LANES,),
        in_specs=[pl.BlockSpec((rank, LANES), src)],
        out_specs=pl.BlockSpec((rank, LANES), lambda j: (0, j)),
        compiler_params=_params(("parallel",)),
        name="prep_w_ukv",
    )(w_ukv)


def _rope_gain(g):
    gr = g[NOPE_DIM:]
    return jnp.concatenate([gr, _swap_halves(gr)]).reshape(1, 2 * ROPE_DIM)


def _rope_tables(length):
    pos = jnp.arange(length, dtype=F32)
    inv_freq = ROPE_BASE ** (-jnp.arange(0, ROPE_DIM, 2, dtype=F32) / ROPE_DIM)
    ang = pos[:, None] * inv_freq[None, :]
    cos, sin = jnp.cos(ang), jnp.sin(ang)
    return jnp.tile(cos, (1, 4)), jnp.tile(jnp.concatenate([-sin, sin], axis=-1), (1, 2))


def kernel(x_prompt, x_sample, c_prompt, c_sample, w_ada, b_ada, norm_mix_g, w_in, kv_norm_g, w_ukv,
           q_norm_g, k_norm_g, lam_re, lam_im, log_dt, b_re, b_im, c_re, c_im, d_skip, w_glu, b_glu,
           att_out_g, ssm_out_g, w_o, norm_ffn_g, w1, w3, w2):
    assert w_ada.shape[0] == 1, "single-layer kernel"
    xs = (x_prompt, x_sample)
    batches = tuple(x.shape[0] for x in xs)
    length = x_prompt.shape[1]
    assert x_sample.shape[1] == length and sum(batches) <= S5_ROWS

    c_all = jnp.concatenate([c_prompt, c_sample], axis=0)
    c8 = jnp.pad(c_all, ((0, SUBLANES - c_all.shape[0]), (0, 0)))
    mod_all = ada_mod(c8, w_ada[0], b_ada[0]).reshape(SUBLANES, 6, D_MODEL)
    mods = (mod_all[:batches[0]], mod_all[batches[0]:batches[0] + batches[1]])

    w_all = prep_w_in(w_in[0])
    w_ukv_p = prep_w_ukv(w_ukv[0])
    row = lambda a: a.reshape(1, -1)
    q_gn, k_gn = row(q_norm_g[0][:NOPE_DIM]), row(k_norm_g[0][:NOPE_DIM])
    q_gr, k_gr = _rope_gain(q_norm_g[0]), _rope_gain(k_norm_g[0])
    cos_t, sin_t = _rope_tables(length)
    s5_ops = s5_matrices(lam_re[0], lam_im[0], log_dt[0], b_re[0], b_im[0], c_re[0], c_im[0])
    w_glu_b, w_o_b = cast_bf16(w_glu[0]), cast_bf16(w_o[0])
    w1_b, w3_b, w2_b = cast_bf16(w1[0]), cast_bf16(w3[0]), cast_bf16(w2[0])

    proj = [in_proj(x, m, row(norm_mix_g[0]), w_all, w_ukv_p, row(kv_norm_g[0]), q_gn, k_gn, q_gr, k_gr,
                    cos_t, sin_t) for x, m in zip(xs, mods)]
    atts = [attention(q, k, v) for q, k, v, _, _ in proj]
    y_rows = s5_scan([p[4] for p in proj], *s5_ops)

    outs = []
    for x, m, att, ys, p in zip(xs, mods, atts, y_rows, proj):
        x1, h2 = post_mix(att, ys, p[3], x, m, row(d_skip[0]), w_glu_b, row(b_glu[0]), row(att_out_g[0]),
                          row(ssm_out_g[0]), w_o_b, row(norm_ffn_g[0]))
        outs.append(ffn_down(ffn_up(h2, w1_b, w3_b), x1, m, w2_b))
    return tuple(outs)
```

```python
import functools
import math

import jax
import jax.numpy as jnp
from jax import lax
from jax.experimental import pallas as pl
from jax.experimental.pallas import tpu as pltpu

F32 = jnp.float32
BF16 = jnp.bfloat16

D_MODEL = 2048
ATT_WIDTH = 1024
SSM_WIDTH = 1024
N_HEADS = 8
V_DIM = 128
NOPE_DIM = 128
ROPE_DIM = 64
QK_DIM = NOPE_DIM + ROPE_DIM
KV_RANK = 512
ROPE_BASE = 10000.0
SSM_GROUP = 16
N_GROUPS = SSM_WIDTH // SSM_GROUP
SSM_STATE = 64
D_FF = 5632
NORM_EPS = 1e-6

LANES = 128
SUBLANES = 8
VMEM_LIMIT = 56 * 1024 * 1024
CAST_BLOCK_BYTES = 4 * 1024 * 1024

ATT_TQ, ATT_TK = 1024, 256
FFN_UP_TM, FFN_UP_TF = 1024, 512
FFN_DOWN_TM, FFN_DOWN_TN = 512, 1024
W_PREP_ROWS = 256

C_QN = 0
C_QR = C_QN + N_HEADS * NOPE_DIM
C_KV = C_QR + N_HEADS * 2 * ROPE_DIM
C_KR = C_KV + KV_RANK
C_U = C_KR + 2 * ROPE_DIM
C_END = C_U + SSM_WIDTH

S5_CHUNK = 32
S5_ROWS = SUBLANES
S5_SUB = SUBLANES
S5_TILE = S5_SUB * S5_CHUNK
S5_COLS = S5_CHUNK * SSM_GROUP
S5_STATE_COLS = 4 * SSM_STATE
GROUPS_PER_COL = LANES // SSM_GROUP


def _params(sem):
    return pltpu.CompilerParams(dimension_semantics=sem, vmem_limit_bytes=VMEM_LIMIT)


def _dot(a, b):
    return jnp.dot(a, b, preferred_element_type=F32)


def _dot_f32(a, b):
    a_hi = a.astype(BF16)
    a_lo = (a - a_hi.astype(F32)).astype(BF16)
    b_hi = b.astype(BF16)
    b_lo = (b - b_hi.astype(F32)).astype(BF16)
    return _dot(a_hi, b_hi) + _dot(a_lo, b_hi) + _dot(a_hi, b_lo)


def _ada_kernel(c_ref, w_ref, b_ref, o_ref):
    c = c_ref[...]
    o_ref[...] = _dot_f32(c * jax.nn.sigmoid(c), w_ref[...]) + b_ref[...]


def ada_mod(c8, w_ada, b_ada):
    rows, d = c8.shape
    n = w_ada.shape[1]
    tn = 1024
    return pl.pallas_call(
        _ada_kernel,
        out_shape=jax.ShapeDtypeStruct((rows, n), F32),
        grid=(n // tn,),
        in_specs=[
            pl.BlockSpec((rows, d), lambda j: (0, 0)),
            pl.BlockSpec((d, tn), lambda j: (0, j)),
            pl.BlockSpec((1, tn), lambda j: (0, j)),
        ],
        out_specs=pl.BlockSpec((rows, tn), lambda j: (0, j)),
        compiler_params=_params(("arbitrary",)),
        name="ada_mod",
    )(c8, w_ada, b_ada.reshape(1, n))


def _rms(x, width):
    return lax.rsqrt(jnp.sum(x * x, axis=-1, keepdims=True) * (1.0 / width) + NORM_EPS)


def _granule_masks():
    pos = lax.broadcasted_iota(jnp.int32, (SUBLANES, LANES), 1) // SSM_GROUP
    return [pos == p for p in range(GROUPS_PER_COL)]


def _to_chunk_rows(u_cols, dst):
    masks = _granule_masks()
    for j in range(SSM_WIDTH // LANES):
        rolled = []
        for t in range(S5_CHUNK):
            v = u_cols[j, pl.ds(t, S5_SUB, stride=S5_CHUNK), :]
            r = t % GROUPS_PER_COL
            rolled.append(pltpu.roll(v, SSM_GROUP * r, axis=1) if r else v)
        for k in range(GROUPS_PER_COL):
            for d in range(S5_COLS // LANES):
                acc = rolled[GROUPS_PER_COL * d + GROUPS_PER_COL - 1]
                for rho in range(GROUPS_PER_COL - 2, -1, -1):
                    acc = jnp.where(masks[(k + rho) % GROUPS_PER_COL], rolled[GROUPS_PER_COL * d + rho], acc)
                dst[GROUPS_PER_COL * j + k, :, d * LANES:(d + 1) * LANES] = acc


def _from_chunk_rows(src, y_cols):
    masks = _granule_masks()
    for j in range(SSM_WIDTH // LANES):
        for d in range(S5_COLS // LANES):
            cols = [src[GROUPS_PER_COL * j + k, :, d * LANES:(d + 1) * LANES] for k in range(GROUPS_PER_COL)]
            for rho in range(GROUPS_PER_COL):
                acc = cols[GROUPS_PER_COL - 1]
                for k in range(GROUPS_PER_COL - 2, -1, -1):
                    acc = jnp.where(masks[(k + rho) % GROUPS_PER_COL], cols[k], acc)
                v = pltpu.roll(acc, LANES - SSM_GROUP * rho, axis=1) if rho else acc
                y_cols[j, pl.ds(GROUPS_PER_COL * d + rho, S5_SUB, stride=S5_CHUNK), :] = v


def _in_proj_kernel(x_ref, mod_ref, g1_ref, w_ref, wukv_ref, kvg_ref, qg_ref, kg_ref,
                    qgr_ref, kgr_ref, cos_ref, sin_ref, q_ref, k_ref, v_ref, u_ref, ur_ref, u_cols,
                    *, q_scale):
    x = x_ref[0]
    shift = mod_ref[0, 0:1, :]
    scale = mod_ref[0, 1:2, :]
    h = (x * _rms(x, D_MODEL) * g1_ref[...]) * (1.0 + scale) + shift
    h = h.astype(BF16)

    cos = cos_ref[...]
    sin = sin_ref[...]

    def rope(col, gain):
        cg = col * gain
        return cg * cos + pltpu.roll(cg, ROPE_DIM, axis=1) * sin

    qn = _dot(h, w_ref[:, C_QN:C_QR])
    qr = _dot(h, w_ref[:, C_QR:C_KV])
    qg_n = qg_ref[...]
    qg_r = qgr_ref[...]
    for hd in range(N_HEADS):
        qn_h = qn[:, hd * LANES:(hd + 1) * LANES]
        qr_h = qr[:, hd * LANES:(hd + 1) * LANES]
        ss = (jnp.sum(qn_h * qn_h, axis=-1, keepdims=True)
              + 0.5 * jnp.sum(qr_h * qr_h, axis=-1, keepdims=True))
        r = lax.rsqrt(ss * (1.0 / QK_DIM) + NORM_EPS) * q_scale
        q_ref[0, hd, :, 0:NOPE_DIM] = (qn_h * r * qg_n).astype(BF16)
        q_ref[0, hd, :, NOPE_DIM:QK_DIM] = (rope(qr_h, qg_r) * r)[:, 0:ROPE_DIM].astype(BF16)

    ckv = _dot(h, w_ref[:, C_KV:C_KR])
    ckv_n = (ckv * _rms(ckv, KV_RANK) * kvg_ref[...]).astype(BF16)
    kv = _dot(ckv_n, wukv_ref[...])
    kr = _dot(h, w_ref[:, C_KR:C_U])
    kr_ss = 0.5 * jnp.sum(kr * kr, axis=-1, keepdims=True)
    kr_rot = rope(kr, kgr_ref[...])
    kg_n = kg_ref[...]
    ones = jnp.ones((x.shape[0], LANES), BF16)
    for hd in range(N_HEADS):
        kn_h = kv[:, hd * LANES:(hd + 1) * LANES]
        ss = jnp.sum(kn_h * kn_h, axis=-1, keepdims=True) + kr_ss
        r = lax.rsqrt(ss * (1.0 / QK_DIM) + NORM_EPS)
        k_ref[0, hd, :, 0:NOPE_DIM] = (kn_h * r * kg_n).astype(BF16)
        k_ref[0, hd, :, NOPE_DIM:QK_DIM] = (kr_rot * r)[:, 0:ROPE_DIM].astype(BF16)
        v_h = kv[:, ATT_WIDTH + hd * LANES:ATT_WIDTH + (hd + 1) * LANES]
        v_ref[0, hd, :, 0:V_DIM] = v_h.astype(BF16)
        v_ref[0, hd, :, V_DIM:2 * V_DIM] = ones

    u = _dot(h, w_ref[:, C_U:C_END])
    u_ref[0] = u
    for j in range(SSM_WIDTH // LANES):
        u_cols[j] = u[:, j * LANES:(j + 1) * LANES]
    _to_chunk_rows(u_cols, ur_ref)


def in_proj(x, mod, g1, w_all, w_ukv, kv_g, q_gn, k_gn, q_gr, k_gr, cos_t, sin_t):
    b, length, d = x.shape
    tm = S5_TILE
    assert length % tm == 0
    const = lambda bi, i: (0, 0)
    kern = functools.partial(_in_proj_kernel, q_scale=math.log2(math.e) / math.sqrt(QK_DIM))
    return pl.pallas_call(
        kern,
        out_shape=(
            jax.ShapeDtypeStruct((b, N_HEADS, length, QK_DIM), BF16),
            jax.ShapeDtypeStruct((b, N_HEADS, length, QK_DIM), BF16),
            jax.ShapeDtypeStruct((b, N_HEADS, length, 2 * V_DIM), BF16),
            jax.ShapeDtypeStruct((b, length, SSM_WIDTH), F32),
            jax.ShapeDtypeStruct((N_GROUPS, length // S5_CHUNK, b * S5_COLS), F32),
        ),
        grid=(b, length // tm),
        in_specs=[
            pl.BlockSpec((1, tm, d), lambda bi, i: (bi, i, 0)),
            pl.BlockSpec((1, 6, d), lambda bi, i: (bi, 0, 0)),
            pl.BlockSpec((1, d), const),
            pl.BlockSpec((d, C_END), const),
            pl.BlockSpec((KV_RANK, 2 * ATT_WIDTH), const),
            pl.BlockSpec((1, KV_RANK), const),
            pl.BlockSpec((1, LANES), const),
            pl.BlockSpec((1, LANES), const),
            pl.BlockSpec((1, LANES), const),
            pl.BlockSpec((1, LANES), const),
            pl.BlockSpec((tm, LANES), lambda bi, i: (i, 0)),
            pl.BlockSpec((tm, LANES), lambda bi, i: (i, 0)),
        ],
        out_specs=(
            pl.BlockSpec((1, N_HEADS, tm, QK_DIM), lambda bi, i: (bi, 0, i, 0)),
            pl.BlockSpec((1, N_HEADS, tm, QK_DIM), lambda bi, i: (bi, 0, i, 0)),
            pl.BlockSpec((1, N_HEADS, tm, 2 * V_DIM), lambda bi, i: (bi, 0, i, 0)),
            pl.BlockSpec((1, tm, SSM_WIDTH), lambda bi, i: (bi, i, 0)),
            pl.BlockSpec((N_GROUPS, S5_SUB, S5_COLS), lambda bi, i: (0, i, bi)),
        ),
        scratch_shapes=[pltpu.VMEM((SSM_WIDTH // LANES, tm, LANES), F32)],
        compiler_params=_params(("parallel", "parallel")),
        name="in_proj",
    )(x, mod, g1, w_all, w_ukv, kv_g, q_gn, k_gn, q_gr, k_gr, cos_t, sin_t)


def _attn_kernel(q_ref, k_ref, v_ref, o_ref, *, tk):
    q = q_ref[0, 0]
    tq = q.shape[0]
    m = jnp.full((tq, 1), -jnp.inf, F32)
    acc = jnp.zeros((tq, 2 * V_DIM), F32)
    for j in range(k_ref.shape[2] // tk):
        kc = k_ref[0, 0, j * tk:(j + 1) * tk, :]
        vc = v_ref[0, 0, j * tk:(j + 1) * tk, :]
        s = lax.dot_general(q, kc, (((1,), (1,)), ((), ())), preferred_element_type=F32)
        m_new = jnp.maximum(m, jnp.max(s, axis=-1, keepdims=True))
        alpha = jnp.exp2(m - m_new)
        p = jnp.exp2(s - m_new).astype(BF16)
        acc = alpha * acc + _dot(p, vc)
        m = m_new
    o_ref[0] = (acc[:, 0:V_DIM] / acc[:, V_DIM:2 * V_DIM]).astype(o_ref.dtype)


def attention(q, k, v):
    b, h, length, _ = q.shape
    tq = min(ATT_TQ, length)
    tk = min(ATT_TK, length)
    return pl.pallas_call(
        functools.partial(_attn_kernel, tk=tk),
        out_shape=jax.ShapeDtypeStruct((b, length, h * V_DIM), BF16),
        grid=(b, h, length // tq),
        in_specs=[
            pl.BlockSpec((1, 1, tq, QK_DIM), lambda bi, hi, i: (bi, hi, i, 0)),
            pl.BlockSpec((1, 1, length, QK_DIM), lambda bi, hi, i: (bi, hi, 0, 0)),
            pl.BlockSpec((1, 1, length, 2 * V_DIM), lambda bi, hi, i: (bi, hi, 0, 0)),
        ],
        out_specs=pl.BlockSpec((1, tq, V_DIM), lambda bi, hi, i: (bi, i, hi)),
        compiler_params=_params(("parallel", "parallel", "parallel")),
        name="attention",
    )(q, k, v)


def s5_matrices(lam_re, lam_im, log_dt, b_re, b_im, c_re, c_im):
    t = S5_CHUNK
    dt = jnp.exp(log_dt)[..., None]
    mag1 = jnp.exp(lam_re * dt)
    ang1 = lam_im * dt
    lb_re, lb_im = mag1 * jnp.cos(ang1), mag1 * jnp.sin(ang1)
    den = lam_re * lam_re + lam_im * lam_im
    n_re, n_im = lb_re - 1.0, lb_im
    cf_re = (n_re * lam_re + n_im * lam_im) / den
    cf_im = (n_im * lam_re - n_re * lam_im) / den
    bb_re = cf_re[..., None] * b_re[None] - cf_im[..., None] * b_im[None]
    bb_im = cf_re[..., None] * b_im[None] + cf_im[..., None] * b_re[None]

    kk = jnp.arange(-t, t + 1, dtype=F32)
    mag = jnp.exp(lam_re[..., None] * dt[..., None] * kk)
    ang = ang1[..., None] * kk
    pw_re, pw_im = mag * jnp.cos(ang), mag * jnp.sin(ang)
    steps = jnp.arange(t)

    def rows_op(d, ks):
        pr = jnp.moveaxis(pw_re[d][..., ks + t], -1, 1)[:, :, None, :]
        pi = jnp.moveaxis(pw_im[d][..., ks + t], -1, 1)[:, :, None, :]
        br = jnp.swapaxes(bb_re[d], 1, 2)[:, None]
        bi = jnp.swapaxes(bb_im[d], 1, 2)[:, None]
        flat = lambda z: z.reshape(N_GROUPS, S5_COLS, SSM_STATE)
        return flat(pr * br - pi * bi), flat(pr * bi + pi * br)

    def cols_op(d, ks):
        pr = pw_re[d][..., ks + t][..., None]
        pi = pw_im[d][..., ks + t][..., None]
        cr = jnp.swapaxes(c_re[d], 1, 2)[:, :, None, :]
        ci = jnp.swapaxes(c_im[d], 1, 2)[:, :, None, :]
        flat = lambda z: z.reshape(N_GROUPS, SSM_STATE, S5_COLS)
        return flat(pr * cr - pi * ci), flat(pr * ci + pi * cr)

    lf_re, lf_im = rows_op(0, -steps)
    lb_re_, lb_im_ = rows_op(1, steps)
    sf_re, sf_im = rows_op(0, t - 1 - steps)
    rf_re, rf_im = cols_op(0, steps)
    rb_re, rb_im = cols_op(1, -steps)
    of_re, of_im = cols_op(0, steps + 1)
    ob_re, ob_im = cols_op(1, t - steps)

    rows = lambda z: _s5_lane_order(z, 1)
    cols = lambda z: _s5_lane_order(z, 2)
    l_f = rows(jnp.concatenate([lf_re, -lf_im], axis=-1))
    l_b = rows(jnp.concatenate([lb_re_, -lb_im_], axis=-1))
    r_f = cols(jnp.concatenate([rf_re, rf_im], axis=1))
    r_b = cols(jnp.concatenate([rb_re, rb_im], axis=1))
    m_st = rows(jnp.concatenate([sf_re, lb_re_, sf_im, lb_im_], axis=-1).astype(BF16))
    m_out = cols(jnp.concatenate([of_re, ob_re, -of_im, -ob_im], axis=1).astype(BF16))
    a_re = jnp.concatenate([pw_re[0][..., 2 * t], pw_re[1][..., 2 * t]], axis=-1)[:, None, :]
    a_im = jnp.concatenate([pw_im[0][..., 2 * t], pw_im[1][..., 2 * t]], axis=-1)[:, None, :]
    return l_f, r_f, l_b, r_b, m_st, m_out, a_re, a_im


def _s5_lane_order(a, axis):
    a = jnp.moveaxis(a, axis, -1)
    lead = a.shape[1:-1]
    a = a.reshape(N_GROUPS // GROUPS_PER_COL, GROUPS_PER_COL, *lead,
                  S5_CHUNK // GROUPS_PER_COL, GROUPS_PER_COL, SSM_GROUP)
    a = jnp.stack([jnp.roll(a[:, k], k, axis=-2) for k in range(GROUPS_PER_COL)], axis=1)
    return jnp.moveaxis(a.reshape(N_GROUPS, *lead, S5_COLS), -1, axis)


def _s5_kernel(*refs, batches):
    n_in = len(batches)
    u_refs = refs[:n_in]
    lf_ref, rf_ref, lb_ref, rb_ref, mst_ref, mout_ref, are_ref, aim_ref = refs[n_in:n_in + 8]
    y_refs = refs[n_in + 8:2 * n_in + 8]
    st_ref, xin_ref = refs[2 * n_in + 8:]
    n_chunks = u_refs[0].shape[1]

    def step_of(idx):
        k = pl.program_id(0) % GROUPS_PER_COL
        d = lax.shift_right_logical(idx, LANES.bit_length() - 1)
        pos = lax.shift_right_logical(idx, SSM_GROUP.bit_length() - 1) & (GROUPS_PER_COL - 1)
        return d * GROUPS_PER_COL + ((pos - k) & (GROUPS_PER_COL - 1))

    s_row = step_of(lax.broadcasted_iota(jnp.int32, (S5_COLS, S5_COLS), 0))
    t_col = step_of(lax.broadcasted_iota(jnp.int32, (S5_COLS, S5_COLS), 1))
    toep = (jnp.where(t_col >= s_row, _dot_f32(lf_ref[0], rf_ref[0]), 0.0)
            + jnp.where(t_col <= s_row, _dot_f32(lb_ref[0], rb_ref[0]), 0.0)).astype(BF16)
    seqs = [(u_ref, y_ref, b) for u_ref, y_ref, nb in zip(u_refs, y_refs, batches) for b in range(nb)]

    def u_rows(u_ref, b):
        return u_ref[0, :, b * S5_COLS:(b + 1) * S5_COLS].astype(BF16)

    for slot in range(S5_ROWS):
        rows = pl.ds(slot, n_chunks, stride=S5_ROWS)
        if slot < len(seqs):
            st = _dot(u_rows(seqs[slot][0], seqs[slot][2]), mst_ref[0])
        else:
            st = jnp.zeros((n_chunks, S5_STATE_COLS), F32)
        st_ref[0, rows, :] = st[:, 0:LANES]
        st_ref[1, rows, :] = st[:, LANES:2 * LANES]

    a_re = jnp.broadcast_to(are_ref[0], (S5_ROWS, LANES))
    a_im = jnp.broadcast_to(aim_ref[0], (S5_ROWS, LANES))
    is_fwd = lax.broadcasted_iota(jnp.int32, (S5_ROWS, LANES), 1) < SSM_STATE
    half = SSM_STATE

    def body(i, carry):
        x_re, x_im = carry
        rf = pl.multiple_of(i * S5_ROWS, S5_ROWS)
        rb = pl.multiple_of((n_chunks - 1 - i) * S5_ROWS, S5_ROWS)
        xin_ref[0, pl.ds(rf, S5_ROWS), 0:half] = x_re[:, 0:half]
        xin_ref[0, pl.ds(rb, S5_ROWS), half:LANES] = x_re[:, half:LANES]
        xin_ref[1, pl.ds(rf, S5_ROWS), 0:half] = x_im[:, 0:half]
        xin_ref[1, pl.ds(rb, S5_ROWS), half:LANES] = x_im[:, half:LANES]
        s_re = jnp.where(is_fwd, st_ref[0, pl.ds(rf, S5_ROWS), :], st_ref[0, pl.ds(rb, S5_ROWS), :])
        s_im = jnp.where(is_fwd, st_ref[1, pl.ds(rf, S5_ROWS), :], st_ref[1, pl.ds(rb, S5_ROWS), :])
        n_re = a_re * x_re - a_im * x_im + s_re
        n_im = a_re * x_im + a_im * x_re + s_im
        return n_re, n_im

    zero = jnp.zeros((S5_ROWS, LANES), F32)
    lax.fori_loop(0, n_chunks, body, (zero, zero), unroll=4)

    for slot, (u_ref, y_ref, b) in enumerate(seqs):
        rows = pl.ds(slot, n_chunks, stride=S5_ROWS)
        x_in = jnp.concatenate([xin_ref[0, rows, :], xin_ref[1, rows, :]], axis=1).astype(BF16)
        y_ref[0, :, b * S5_COLS:(b + 1) * S5_COLS] = _dot(u_rows(u_ref, b), toep) + _dot(x_in, mout_ref[0])


def s5_scan(u_rows, l_f, r_f, l_b, r_b, m_st, m_out, a_re, a_im):
    g, n_chunks, _ = u_rows[0].shape
    batches = tuple(u.shape[2] // S5_COLS for u in u_rows)
    assert sum(batches) <= S5_ROWS
    gmap = lambda gi: (gi, 0, 0)
    io_specs = [pl.BlockSpec((1, n_chunks, u.shape[2]), gmap) for u in u_rows]
    return pl.pallas_call(
        functools.partial(_s5_kernel, batches=batches),
        out_shape=tuple(jax.ShapeDtypeStruct(u.shape, F32) for u in u_rows),
        grid=(g,),
        in_specs=io_specs + [
            pl.BlockSpec((1, S5_COLS, 2 * SSM_STATE), gmap),
            pl.BlockSpec((1, 2 * SSM_STATE, S5_COLS), gmap),
            pl.BlockSpec((1, S5_COLS, 2 * SSM_STATE), gmap),
            pl.BlockSpec((1, 2 * SSM_STATE, S5_COLS), gmap),
            pl.BlockSpec((1, S5_COLS, S5_STATE_COLS), gmap),
            pl.BlockSpec((1, S5_STATE_COLS, S5_COLS), gmap),
            pl.BlockSpec((1, 1, LANES), gmap),
            pl.BlockSpec((1, 1, LANES), gmap),
        ],
        out_specs=tuple(io_specs),
        scratch_shapes=[pltpu.VMEM((2, n_chunks * S5_ROWS, LANES), F32),
                        pltpu.VMEM((2, n_chunks * S5_ROWS, LANES), F32)],
        compiler_params=_params(("parallel",)),
        name="s5_scan",
    )(*u_rows, l_f, r_f, l_b, r_b, m_st, m_out, a_re, a_im)


def _gelu_tanh(x):
    return 0.5 * x * (1.0 + jnp.tanh(math.sqrt(2.0 / math.pi) * (x + 0.044715 * (x * x * x))))


def _post_mix_kernel(att_ref, ys_ref, u_ref, x_ref, mod_ref, dsk_ref, wglu_ref, bglu_ref,
                     ag_ref, sg_ref, wo_ref, g2_ref, x1_ref, h2_ref, y_cols):
    gate1 = mod_ref[0, 2:3, :]
    shift2 = mod_ref[0, 3:4, :]
    scale2 = mod_ref[0, 4:5, :]
    _from_chunk_rows(ys_ref, y_cols)
    y = jnp.concatenate([y_cols[j] for j in range(SSM_WIDTH // LANES)], axis=1)
    y = y + dsk_ref[...] * u_ref[0]
    y = _gelu_tanh(y)
    z = _dot(y.astype(BF16), wglu_ref[...]) + bglu_ref[...]
    ssm = y * jax.nn.sigmoid(z)
    att = att_ref[0].astype(F32)
    att_n = (att * _rms(att, ATT_WIDTH) * ag_ref[...]).astype(BF16)
    ssm_n = (ssm * _rms(ssm, SSM_WIDTH) * sg_ref[...]).astype(BF16)
    mix = _dot(att_n, wo_ref[0:ATT_WIDTH, :]) + _dot(ssm_n, wo_ref[ATT_WIDTH:2 * ATT_WIDTH, :])
    x1 = x_ref[0] + gate1 * mix
    x1_ref[0] = x1
    h2 = (x1 * _rms(x1, D_MODEL) * g2_ref[...]) * (1.0 + scale2) + shift2
    h2_ref[0] = h2.astype(BF16)


def post_mix(att, ys, u, x, mod, d_skip, w_glu, b_glu, att_g, ssm_g, w_o, g2):
    b, length, d = x.shape
    tm = S5_TILE
    const = lambda bi, i: (0, 0)
    tile = lambda w: pl.BlockSpec((1, tm, w), lambda bi, i: (bi, i, 0))
    return pl.pallas_call(
        _post_mix_kernel,
        out_shape=(jax.ShapeDtypeStruct((b, length, d), F32),
                   jax.ShapeDtypeStruct((b, length, d), BF16)),
        grid=(b, length // tm),
        in_specs=[
            tile(ATT_WIDTH),
            pl.BlockSpec((N_GROUPS, S5_SUB, S5_COLS), lambda bi, i: (0, i, bi)),
            tile(SSM_WIDTH), tile(d),
            pl.BlockSpec((1, 6, d), lambda bi, i: (bi, 0, 0)),
            pl.BlockSpec((1, SSM_WIDTH), const),
            pl.BlockSpec((SSM_WIDTH, SSM_WIDTH), const),
            pl.BlockSpec((1, SSM_WIDTH), const),
            pl.BlockSpec((1, ATT_WIDTH), const),
            pl.BlockSpec((1, SSM_WIDTH), const),
            pl.BlockSpec((d, d), const),
            pl.BlockSpec((1, d), const),
        ],
        out_specs=(tile(d), tile(d)),
        scratch_shapes=[pltpu.VMEM((SSM_WIDTH // LANES, tm, LANES), F32)],
        compiler_params=_params(("parallel", "parallel")),
        name="post_mix",
    )(att, ys, u, x, mod, d_skip, w_glu, b_glu, att_g, ssm_g, w_o, g2)


def _ffn_up_kernel(h_ref, w1_ref, w3_ref, o_ref):
    h = h_ref[0]
    a = _dot(h, w1_ref[...])
    g = _dot(h, w3_ref[...])
    o_ref[0] = (a * jax.nn.sigmoid(a) * g).astype(BF16)


def ffn_up(h2, w1, w3):
    b, length, d = h2.shape
    tm, tf = min(FFN_UP_TM, length), FFN_UP_TF
    return pl.pallas_call(
        _ffn_up_kernel,
        out_shape=jax.ShapeDtypeStruct((b, length, D_FF), BF16),
        grid=(D_FF // tf, b, length // tm),
        in_specs=[
            pl.BlockSpec((1, tm, d), lambda f, bi, i: (bi, i, 0)),
            pl.BlockSpec((d, tf), lambda f, bi, i: (0, f)),
            pl.BlockSpec((d, tf), lambda f, bi, i: (0, f)),
        ],
        out_specs=pl.BlockSpec((1, tm, tf), lambda f, bi, i: (bi, i, f)),
        compiler_params=_params(("parallel", "parallel", "parallel")),
        name="ffn_up",
    )(h2, w1, w3)


def _ffn_down_kernel(a_ref, x1_ref, mod_ref, w2_ref, o_ref):
    o_ref[0] = x1_ref[0] + mod_ref[0, 5:6, :] * _dot(a_ref[0], w2_ref[...])


def ffn_down(act, x1, mod, w2):
    b, length, d = x1.shape
    tm, tn = min(FFN_DOWN_TM, length), FFN_DOWN_TN
    return pl.pallas_call(
        _ffn_down_kernel,
        out_shape=jax.ShapeDtypeStruct((b, length, d), F32),
        grid=(d // tn, b, length // tm),
        in_specs=[
            pl.BlockSpec((1, tm, D_FF), lambda n, bi, i: (bi, i, 0)),
            pl.BlockSpec((1, tm, tn), lambda n, bi, i: (bi, i, n)),
            pl.BlockSpec((1, 6, tn), lambda n, bi, i: (bi, 0, n)),
            pl.BlockSpec((D_FF, tn), lambda n, bi, i: (0, n)),
        ],
        out_specs=pl.BlockSpec((1, tm, tn), lambda n, bi, i: (bi, i, n)),
        compiler_params=_params(("parallel", "parallel", "parallel")),
        name="ffn_down",
    )(act, x1, mod, w2)


def _cast_kernel(x_ref, o_ref):
    o_ref[...] = x_ref[...].astype(o_ref.dtype)


def cast_bf16(w):
    rows, cols = w.shape
    tr = rows
    while tr * cols * 4 > CAST_BLOCK_BYTES and tr % (4 * SUBLANES) == 0:
        tr //= 2
    return pl.pallas_call(
        _cast_kernel,
        out_shape=jax.ShapeDtypeStruct((rows, cols), BF16),
        grid=(rows // tr,),
        in_specs=[pl.BlockSpec((tr, cols), lambda i: (i, 0))],
        out_specs=pl.BlockSpec((tr, cols), lambda i: (i, 0)),
        compiler_params=_params(("parallel",)),
        name="cast_bf16",
    )(w)


def _swap_halves(a, axis=-1):
    lo, hi = jnp.split(a, 2, axis=axis)
    return jnp.concatenate([hi, lo], axis=axis)


def _prep_w_in_kernel(w_ref, o_ref):
    w = w_ref[...]
    q_cols = N_HEADS * QK_DIM
    half = ROPE_DIM // 2

    def with_swapped(r):
        return [r, r[:, half:], r[:, :half]]

    cols = [w[:, hd * QK_DIM:hd * QK_DIM + NOPE_DIM] for hd in range(N_HEADS)]
    for hd in range(N_HEADS):
        cols += with_swapped(w[:, hd * QK_DIM + NOPE_DIM:(hd + 1) * QK_DIM])
    cols.append(w[:, q_cols:q_cols + KV_RANK])
    cols += with_swapped(w[:, q_cols + KV_RANK:q_cols + KV_RANK + ROPE_DIM])
    cols.append(w[:, q_cols + KV_RANK + ROPE_DIM:])
    o_ref[...] = jnp.concatenate(cols, axis=1).astype(BF16)


def prep_w_in(w_in):
    d, n = w_in.shape
    return pl.pallas_call(
        _prep_w_in_kernel,
        out_shape=jax.ShapeDtypeStruct((d, C_END), BF16),
        grid=(d // W_PREP_ROWS,),
        in_specs=[pl.BlockSpec((W_PREP_ROWS, n), lambda i: (i, 0))],
        out_specs=pl.BlockSpec((W_PREP_ROWS, C_END), lambda i: (i, 0)),
        compiler_params=_params(("parallel",)),
        name="prep_w_in",
    )(w_in)


def prep_w_ukv(w_ukv):
    rank, n = w_ukv.shape
    src = lambda j: (0, 2 * (j % N_HEADS) + j // N_HEADS)
    return pl.pallas_call(
        _cast_kernel,
        out_shape=jax.ShapeDtypeStruct((rank, n), BF16),
        grid=(n // LANES,),
        in_specs=[pl.BlockSpec((rank, LANES), src)],
        out_specs=pl.BlockSpec((rank, LANES), lambda j: (0, j)),
        compiler_params=_params(("parallel",)),
        name="prep_w_ukv",
    )(w_ukv)


def _rope_gain(g):
    gr = g[NOPE_DIM:]
    return jnp.concatenate([gr, _swap_halves(gr)]).reshape(1, 2 * ROPE_DIM)


def _rope_tables(length):
    pos = jnp.arange(length, dtype=F32)
    inv_freq = ROPE_BASE ** (-jnp.arange(0, ROPE_DIM, 2, dtype=F32) / ROPE_DIM)
    ang = pos[:, None] * inv_freq[None, :]
    cos, sin = jnp.cos(ang), jnp.sin(ang)
    return jnp.tile(cos, (1, 4)), jnp.tile(jnp.concatenate([-sin, sin], axis=-1), (1, 2))


def kernel(x_prompt, x_sample, c_prompt, c_sample, w_ada, b_ada, norm_mix_g, w_in, kv_norm_g, w_ukv,
           q_norm_g, k_norm_g, lam_re, lam_im, log_dt, b_re, b_im, c_re, c_im, d_skip, w_glu, b_glu,
           att_out_g, ssm_out_g, w_o, norm_ffn_g, w1, w3, w2):
    assert w_ada.shape[0] == 1, "single-layer kernel"
    xs = (x_prompt, x_sample)
    batches = tuple(x.shape[0] for x in xs)
    length = x_prompt.shape[1]
    assert x_sample.shape[1] == length and sum(batches) <= S5_ROWS

    c_all = jnp.concatenate([c_prompt, c_sample], axis=0)
    c8 = jnp.pad(c_all, ((0, SUBLANES - c_all.shape[0]), (0, 0)))
    mod_all = ada_mod(c8, w_ada[0], b_ada[0]).reshape(SUBLANES, 6, D_MODEL)
    mods = (mod_all[:batches[0]], mod_all[batches[0]:batches[0] + batches[1]])

    w_all = prep_w_in(w_in[0])
    w_ukv_p = prep_w_ukv(w_ukv[0])
    row = lambda a: a.reshape(1, -1)
    q_gn, k_gn = row(q_norm_g[0][:NOPE_DIM]), row(k_norm_g[0][:NOPE_DIM])
    q_gr, k_gr = _rope_gain(q_norm_g[0]), _rope_gain(k_norm_g[0])
    cos_t, sin_t = _rope_tables(length)
    s5_ops = s5_matrices(lam_re[0], lam_im[0], log_dt[0], b_re[0], b_im[0], c_re[0], c_im[0])
    w_glu_b, w_o_b = cast_bf16(w_glu[0]), cast_bf16(w_o[0])
    w1_b, w3_b, w2_b = cast_bf16(w1[0]), cast_bf16(w3[0]), cast_bf16(w2[0])

    proj = [in_proj(x, m, row(norm_mix_g[0]), w_all, w_ukv_p, row(kv_norm_g[0]), q_gn, k_gn, q_gr, k_gr,
                    cos_t, sin_t) for x, m in zip(xs, mods)]
    atts = [attention(q, k, v) for q, k, v, _, _ in proj]
    y_rows = s5_scan([p[4] for p in proj], *s5_ops)

    outs = []
    for x, m, att, ys, p in zip(xs, mods, atts, y_rows, proj):
        x1, h2 = post_mix(att, ys, p[3], x, m, row(d_skip[0]), w_glu_b, row(b_glu[0]), row(att_out_g[0]),
                          row(ssm_out_g[0]), w_o_b, row(norm_ffn_g[0]))
        outs.append(ffn_down(ffn_up(h2, w1_b, w3_b), x1, m, w2_b))
    return tuple(outs)
```

```python
import functools
import math

import jax
import jax.numpy as jnp
import numpy as np
from jax import lax
from jax.experimental import pallas as pl
from jax.experimental.pallas import tpu as pltpu

F32 = jnp.float32
BF16 = jnp.bfloat16

D_MODEL = 2048
ATT_WIDTH = 1024
SSM_WIDTH = 1024
N_HEADS = 8
V_DIM = 128
NOPE_DIM = 128
ROPE_DIM = 64
QK_DIM = NOPE_DIM + ROPE_DIM
KV_RANK = 512
ROPE_BASE = 10000.0
SSM_GROUP = 16
N_GROUPS = SSM_WIDTH // SSM_GROUP
SSM_STATE = 64
D_FF = 5632
NORM_EPS = 1e-6

LANES = 128
SUBLANES = 8
VMEM_LIMIT = 56 * 1024 * 1024
CAST_BLOCK_BYTES = 4 * 1024 * 1024

ATT_TQ, ATT_TK = 1024, 256
FFN_UP_TM, FFN_UP_TF = 1024, 512
FFN_DOWN_TM, FFN_DOWN_TN = 512, 1024
W_PREP_ROWS = 256

C_QN = 0
C_QR = C_QN + N_HEADS * NOPE_DIM
C_KV = C_QR + N_HEADS * 2 * ROPE_DIM
C_KR = C_KV + KV_RANK
C_U = C_KR + 2 * ROPE_DIM
C_END = C_U + SSM_WIDTH

S5_CHUNK = 32
S5_ROWS = SUBLANES
S5_SUB = SUBLANES
S5_TILE = S5_SUB * S5_CHUNK
S5_COLS = S5_CHUNK * SSM_GROUP
S5_STATE_COLS = 4 * SSM_STATE
GROUPS_PER_COL = LANES // SSM_GROUP


def _params(sem):
    return pltpu.CompilerParams(dimension_semantics=sem, vmem_limit_bytes=VMEM_LIMIT)


def _dot(a, b):
    return jnp.dot(a, b, preferred_element_type=F32)


def _dot_f32(a, b):
    a_hi = a.astype(BF16)
    a_lo = (a - a_hi.astype(F32)).astype(BF16)
    b_hi = b.astype(BF16)
    b_lo = (b - b_hi.astype(F32)).astype(BF16)
    return _dot(a_hi, b_hi) + _dot(a_lo, b_hi) + _dot(a_hi, b_lo)


def _ada_kernel(c_ref, w_ref, b_ref, o_ref):
    c = c_ref[...]
    o_ref[...] = _dot_f32(c * jax.nn.sigmoid(c), w_ref[...]) + b_ref[...]


def ada_mod(c8, w_ada, b_ada):
    rows, d = c8.shape
    n = w_ada.shape[1]
    tn = 1024
    return pl.pallas_call(
        _ada_kernel,
        out_shape=jax.ShapeDtypeStruct((rows, n), F32),
        grid=(n // tn,),
        in_specs=[
            pl.BlockSpec((rows, d), lambda j: (0, 0)),
            pl.BlockSpec((d, tn), lambda j: (0, j)),
            pl.BlockSpec((1, tn), lambda j: (0, j)),
        ],
        out_specs=pl.BlockSpec((rows, tn), lambda j: (0, j)),
        compiler_params=_params(("arbitrary",)),
        name="ada_mod",
    )(c8, w_ada, b_ada.reshape(1, n))


def _rms(x, width):
    return lax.rsqrt(jnp.sum(x * x, axis=-1, keepdims=True) * (1.0 / width) + NORM_EPS)


def _granule_masks():
    pos = lax.broadcasted_iota(jnp.int32, (SUBLANES, LANES), 1) // SSM_GROUP
    return [pos == p for p in range(GROUPS_PER_COL)]


def _to_chunk_rows(u_cols, dst):
    masks = _granule_masks()
    for j in range(SSM_WIDTH // LANES):
        rolled = []
        for t in range(S5_CHUNK):
            v = u_cols[j, pl.ds(t, S5_SUB, stride=S5_CHUNK), :]
            r = t % GROUPS_PER_COL
            rolled.append(pltpu.roll(v, SSM_GROUP * r, axis=1) if r else v)
        for k in range(GROUPS_PER_COL):
            for d in range(S5_COLS // LANES):
                acc = rolled[GROUPS_PER_COL * d + GROUPS_PER_COL - 1]
                for rho in range(GROUPS_PER_COL - 2, -1, -1):
                    acc = jnp.where(masks[(k + rho) % GROUPS_PER_COL], rolled[GROUPS_PER_COL * d + rho], acc)
                dst[GROUPS_PER_COL * j + k, :, d * LANES:(d + 1) * LANES] = acc


def _from_chunk_rows(src, y_cols):
    masks = _granule_masks()
    for j in range(SSM_WIDTH // LANES):
        for d in range(S5_COLS // LANES):
            cols = [src[GROUPS_PER_COL * j + k, :, d * LANES:(d + 1) * LANES] for k in range(GROUPS_PER_COL)]
            for rho in range(GROUPS_PER_COL):
                acc = cols[GROUPS_PER_COL - 1]
                for k in range(GROUPS_PER_COL - 2, -1, -1):
                    acc = jnp.where(masks[(k + rho) % GROUPS_PER_COL], cols[k], acc)
                v = pltpu.roll(acc, LANES - SSM_GROUP * rho, axis=1) if rho else acc
                y_cols[j, pl.ds(GROUPS_PER_COL * d + rho, S5_SUB, stride=S5_CHUNK), :] = v


def _in_proj_kernel(x_ref, mod_ref, g1_ref, w_ref, wukv_ref, kvg_ref, qg_ref, kg_ref,
                    qgr_ref, kgr_ref, cos_ref, sin_ref, q_ref, k_ref, v_ref, u_ref, ur_ref, u_cols,
                    *, q_scale):
    x = x_ref[0]
    shift = mod_ref[0, 0:1, :]
    scale = mod_ref[0, 1:2, :]
    h = (x * _rms(x, D_MODEL) * g1_ref[...]) * (1.0 + scale) + shift
    h = h.astype(BF16)

    cos = cos_ref[...]
    sin = sin_ref[...]

    def rope(col, gain):
        cg = col * gain
        return cg * cos + pltpu.roll(cg, ROPE_DIM, axis=1) * sin

    qn = _dot(h, w_ref[:, C_QN:C_QR])
    qr = _dot(h, w_ref[:, C_QR:C_KV])
    qg_n = qg_ref[...]
    qg_r = qgr_ref[...]
    for hd in range(N_HEADS):
        qn_h = qn[:, hd * LANES:(hd + 1) * LANES]
        qr_h = qr[:, hd * LANES:(hd + 1) * LANES]
        ss = (jnp.sum(qn_h * qn_h, axis=-1, keepdims=True)
              + 0.5 * jnp.sum(qr_h * qr_h, axis=-1, keepdims=True))
        r = lax.rsqrt(ss * (1.0 / QK_DIM) + NORM_EPS) * q_scale
        q_ref[0, hd, :, 0:NOPE_DIM] = (qn_h * r * qg_n).astype(BF16)
        q_ref[0, hd, :, NOPE_DIM:QK_DIM] = (rope(qr_h, qg_r) * r)[:, 0:ROPE_DIM].astype(BF16)

    ckv = _dot(h, w_ref[:, C_KV:C_KR])
    ckv_n = (ckv * _rms(ckv, KV_RANK) * kvg_ref[...]).astype(BF16)
    kv = _dot(ckv_n, wukv_ref[...])
    kr = _dot(h, w_ref[:, C_KR:C_U])
    kr_ss = 0.5 * jnp.sum(kr * kr, axis=-1, keepdims=True)
    kr_rot = rope(kr, kgr_ref[...])
    kg_n = kg_ref[...]
    ones = jnp.ones((x.shape[0], LANES), BF16)
    for hd in range(N_HEADS):
        kn_h = kv[:, hd * LANES:(hd + 1) * LANES]
        ss = jnp.sum(kn_h * kn_h, axis=-1, keepdims=True) + kr_ss
        r = lax.rsqrt(ss * (1.0 / QK_DIM) + NORM_EPS)
        k_ref[0, hd, :, 0:NOPE_DIM] = (kn_h * r * kg_n).astype(BF16)
        k_ref[0, hd, :, NOPE_DIM:QK_DIM] = (kr_rot * r)[:, 0:ROPE_DIM].astype(BF16)
        v_h = kv[:, ATT_WIDTH + hd * LANES:ATT_WIDTH + (hd + 1) * LANES]
        v_ref[0, hd, :, 0:V_DIM] = v_h.astype(BF16)
        v_ref[0, hd, :, V_DIM:2 * V_DIM] = ones

    u = _dot(h, w_ref[:, C_U:C_END])
    u_ref[0] = u
    for j in range(SSM_WIDTH // LANES):
        u_cols[j] = u[:, j * LANES:(j + 1) * LANES]
    _to_chunk_rows(u_cols, ur_ref)


def in_proj(x, mod, g1, w_all, w_ukv, kv_g, q_gn, k_gn, q_gr, k_gr, cos_t, sin_t):
    b, length, d = x.shape
    tm = S5_TILE
    assert length % tm == 0
    const = lambda bi, i: (0, 0)
    kern = functools.partial(_in_proj_kernel, q_scale=math.log2(math.e) / math.sqrt(QK_DIM))
    return pl.pallas_call(
        kern,
        out_shape=(
            jax.ShapeDtypeStruct((b, N_HEADS, length, QK_DIM), BF16),
            jax.ShapeDtypeStruct((b, N_HEADS, length, QK_DIM), BF16),
            jax.ShapeDtypeStruct((b, N_HEADS, length, 2 * V_DIM), BF16),
            jax.ShapeDtypeStruct((b, length, SSM_WIDTH), F32),
            jax.ShapeDtypeStruct((N_GROUPS, length // S5_CHUNK, b * S5_COLS), F32),
        ),
        grid=(b, length // tm),
        in_specs=[
            pl.BlockSpec((1, tm, d), lambda bi, i: (bi, i, 0)),
            pl.BlockSpec((1, 6, d), lambda bi, i: (bi, 0, 0)),
            pl.BlockSpec((1, d), const),
            pl.BlockSpec((d, C_END), const),
            pl.BlockSpec((KV_RANK, 2 * ATT_WIDTH), const),
            pl.BlockSpec((1, KV_RANK), const),
            pl.BlockSpec((1, LANES), const),
            pl.BlockSpec((1, LANES), const),
            pl.BlockSpec((1, LANES), const),
            pl.BlockSpec((1, LANES), const),
            pl.BlockSpec((tm, LANES), lambda bi, i: (i, 0)),
            pl.BlockSpec((tm, LANES), lambda bi, i: (i, 0)),
        ],
        out_specs=(
            pl.BlockSpec((1, N_HEADS, tm, QK_DIM), lambda bi, i: (bi, 0, i, 0)),
            pl.BlockSpec((1, N_HEADS, tm, QK_DIM), lambda bi, i: (bi, 0, i, 0)),
            pl.BlockSpec((1, N_HEADS, tm, 2 * V_DIM), lambda bi, i: (bi, 0, i, 0)),
            pl.BlockSpec((1, tm, SSM_WIDTH), lambda bi, i: (bi, i, 0)),
            pl.BlockSpec((N_GROUPS, S5_SUB, S5_COLS), lambda bi, i: (0, i, bi)),
        ),
        scratch_shapes=[pltpu.VMEM((SSM_WIDTH // LANES, tm, LANES), F32)],
        compiler_params=_params(("parallel", "parallel")),
        name="in_proj",
    )(x, mod, g1, w_all, w_ukv, kv_g, q_gn, k_gn, q_gr, k_gr, cos_t, sin_t)


def _attn_kernel(q_ref, k_ref, v_ref, o_ref, *, tk):
    q = q_ref[0, 0]
    tq = q.shape[0]
    m = jnp.full((tq, 1), -jnp.inf, F32)
    acc = jnp.zeros((tq, 2 * V_DIM), F32)
    for j in range(k_ref.shape[2] // tk):
        kc = k_ref[0, 0, j * tk:(j + 1) * tk, :]
        vc = v_ref[0, 0, j * tk:(j + 1) * tk, :]
        s = lax.dot_general(q, kc, (((1,), (1,)), ((), ())), preferred_element_type=F32)
        m_new = jnp.maximum(m, jnp.max(s, axis=-1, keepdims=True))
        alpha = jnp.exp2(m - m_new)
        p = jnp.exp2(s - m_new).astype(BF16)
        acc = alpha * acc + _dot(p, vc)
        m = m_new
    o_ref[0] = (acc[:, 0:V_DIM] / acc[:, V_DIM:2 * V_DIM]).astype(o_ref.dtype)


def attention(q, k, v):
    b, h, length, _ = q.shape
    tq = min(ATT_TQ, length)
    tk = min(ATT_TK, length)
    return pl.pallas_call(
        functools.partial(_attn_kernel, tk=tk),
        out_shape=jax.ShapeDtypeStruct((b, length, h * V_DIM), BF16),
        grid=(b, h, length // tq),
        in_specs=[
            pl.BlockSpec((1, 1, tq, QK_DIM), lambda bi, hi, i: (bi, hi, i, 0)),
            pl.BlockSpec((1, 1, length, QK_DIM), lambda bi, hi, i: (bi, hi, 0, 0)),
            pl.BlockSpec((1, 1, length, 2 * V_DIM), lambda bi, hi, i: (bi, hi, 0, 0)),
        ],
        out_specs=pl.BlockSpec((1, tq, V_DIM), lambda bi, hi, i: (bi, i, hi)),
        compiler_params=_params(("parallel", "parallel", "parallel")),
        name="attention",
    )(q, k, v)


def _s5_steps():
    g = np.arange(N_GROUPS)[:, None] % GROUPS_PER_COL
    slot = np.arange(S5_CHUNK)[None, :]
    return GROUPS_PER_COL * (slot // GROUPS_PER_COL) + (slot % GROUPS_PER_COL - g) % GROUPS_PER_COL


def s5_matrices(lam_re, lam_im, log_dt, b_re, b_im, c_re, c_im):
    t = S5_CHUNK
    dt = jnp.exp(log_dt)[..., None]
    la, lb = lam_re * dt, lam_im * dt
    lb_re, lb_im = jnp.exp(la) * jnp.cos(lb), jnp.exp(la) * jnp.sin(lb)
    den = lam_re * lam_re + lam_im * lam_im
    n_re, n_im = lb_re - 1.0, lb_im
    cf_re = (n_re * lam_re + n_im * lam_im) / den
    cf_im = (n_im * lam_re - n_re * lam_im) / den
    bb_re = cf_re[..., None] * b_re[None] - cf_im[..., None] * b_im[None]
    bb_im = cf_re[..., None] * b_im[None] + cf_im[..., None] * b_re[None]

    steps = _s5_steps().astype(np.float32)

    def power(d, expo):
        e = jnp.asarray(expo)[:, None, :]
        mag = jnp.exp(la[d][..., None] * e)
        return mag * jnp.cos(lb[d][..., None] * e), mag * jnp.sin(lb[d][..., None] * e)

    cat = jnp.concatenate

    def rows_op(pw, xs1, xs2):
        n = len(xs1)
        pr = cat([jnp.swapaxes(p[0], 1, 2) for p in pw], axis=-1)[:, :, None, :]
        pi = cat([jnp.swapaxes(p[1], 1, 2) for p in pw], axis=-1)[:, :, None, :]
        x1 = cat([jnp.swapaxes(x, 1, 2) for x in xs1], axis=-1)[:, None]
        x2 = cat([jnp.swapaxes(x, 1, 2) for x in xs2], axis=-1)[:, None]
        return (pr * x1 + pi * x2).reshape(N_GROUPS, S5_COLS, n * SSM_STATE)

    def cols_op(pw, xs1, xs2):
        n = len(xs1)
        pr = cat([p[0] for p in pw], axis=1)[..., None]
        pi = cat([p[1] for p in pw], axis=1)[..., None]
        x1 = cat(xs1, axis=1)[:, :, None, :]
        x2 = cat(xs2, axis=1)[:, :, None, :]
        return (pr * x1 + pi * x2).reshape(N_GROUPS, n * SSM_STATE, S5_COLS)

    bf_re, bf_im, bb_re_, bb_im_ = bb_re[0], bb_im[0], bb_re[1], bb_im[1]
    cf_re_, cf_im_, cb_re, cb_im = c_re[0], c_im[0], c_re[1], c_im[1]
    tr = lambda x: jnp.swapaxes(x, 1, 2)
    p_lf, p_lb, p_sf = power(0, -steps), power(1, steps), power(0, t - 1 - steps)
    p_rf, p_rb = power(0, steps), power(1, -steps)
    p_of, p_ob = power(0, steps + 1), power(1, t - steps)

    l_f = rows_op([p_lf, p_lf], [bf_re, -bf_im], [-bf_im, -bf_re])
    l_b = rows_op([p_lb, p_lb], [bb_re_, -bb_im_], [-bb_im_, -bb_re_])
    m_st = rows_op([p_sf, p_lb, p_sf, p_lb], [bf_re, bb_re_, bf_im, bb_im_],
                   [-bf_im, -bb_im_, bf_re, bb_re_]).astype(BF16)
    r_f = cols_op([p_rf, p_rf], [tr(cf_re_), tr(cf_im_)], [-tr(cf_im_), tr(cf_re_)])
    r_b = cols_op([p_rb, p_rb], [tr(cb_re), tr(cb_im)], [-tr(cb_im), tr(cb_re)])
    m_out = cols_op([p_of, p_ob, p_of, p_ob], [tr(cf_re_), tr(cb_re), -tr(cf_im_), -tr(cb_im)],
                    [-tr(cf_im_), -tr(cb_im), -tr(cf_re_), -tr(cb_re)]).astype(BF16)
    a_re = cat([jnp.exp(t * la[0]) * jnp.cos(t * lb[0]), jnp.exp(t * la[1]) * jnp.cos(t * lb[1])], axis=-1)[:, None, :]
    a_im = cat([jnp.exp(t * la[0]) * jnp.sin(t * lb[0]), jnp.exp(t * la[1]) * jnp.sin(t * lb[1])], axis=-1)[:, None, :]
    return l_f, r_f, l_b, r_b, m_st, m_out, a_re, a_im


def _s5_kernel(*refs, batches):
    n_in = len(batches)
    u_refs = refs[:n_in]
    lf_ref, rf_ref, lb_ref, rb_ref, mst_ref, mout_ref, are_ref, aim_ref = refs[n_in:n_in + 8]
    y_refs = refs[n_in + 8:2 * n_in + 8]
    st_ref, xin_ref = refs[2 * n_in + 8:]
    n_chunks = u_refs[0].shape[1]

    def step_of(idx):
        k = pl.program_id(0) % GROUPS_PER_COL
        d = lax.shift_right_logical(idx, LANES.bit_length() - 1)
        pos = lax.shift_right_logical(idx, SSM_GROUP.bit_length() - 1) & (GROUPS_PER_COL - 1)
        return d * GROUPS_PER_COL + ((pos - k) & (GROUPS_PER_COL - 1))

    s_row = step_of(lax.broadcasted_iota(jnp.int32, (S5_COLS, S5_COLS), 0))
    t_col = step_of(lax.broadcasted_iota(jnp.int32, (S5_COLS, S5_COLS), 1))
    toep = (jnp.where(t_col >= s_row, _dot_f32(lf_ref[0], rf_ref[0]), 0.0)
            + jnp.where(t_col <= s_row, _dot_f32(lb_ref[0], rb_ref[0]), 0.0)).astype(BF16)
    seqs = [(u_ref, y_ref, b) for u_ref, y_ref, nb in zip(u_refs, y_refs, batches) for b in range(nb)]

    def u_rows(u_ref, b):
        return u_ref[0, :, b * S5_COLS:(b + 1) * S5_COLS].astype(BF16)

    for slot in range(S5_ROWS):
        rows = pl.ds(slot, n_chunks, stride=S5_ROWS)
        if slot < len(seqs):
            st = _dot(u_rows(seqs[slot][0], seqs[slot][2]), mst_ref[0])
        else:
            st = jnp.zeros((n_chunks, S5_STATE_COLS), F32)
        st_ref[0, rows, :] = st[:, 0:LANES]
        st_ref[1, rows, :] = st[:, LANES:2 * LANES]

    a_re = jnp.broadcast_to(are_ref[0], (S5_ROWS, LANES))
    a_im = jnp.broadcast_to(aim_ref[0], (S5_ROWS, LANES))
    is_fwd = lax.broadcasted_iota(jnp.int32, (S5_ROWS, LANES), 1) < SSM_STATE
    half = SSM_STATE

    def body(i, carry):
        x_re, x_im = carry
        rf = pl.multiple_of(i * S5_ROWS, S5_ROWS)
        rb = pl.multiple_of((n_chunks - 1 - i) * S5_ROWS, S5_ROWS)
        xin_ref[0, pl.ds(rf, S5_ROWS), 0:half] = x_re[:, 0:half]
        xin_ref[0, pl.ds(rb, S5_ROWS), half:LANES] = x_re[:, half:LANES]
        xin_ref[1, pl.ds(rf, S5_ROWS), 0:half] = x_im[:, 0:half]
        xin_ref[1, pl.ds(rb, S5_ROWS), half:LANES] = x_im[:, half:LANES]
        s_re = jnp.where(is_fwd, st_ref[0, pl.ds(rf, S5_ROWS), :], st_ref[0, pl.ds(rb, S5_ROWS), :])
        s_im = jnp.where(is_fwd, st_ref[1, pl.ds(rf, S5_ROWS), :], st_ref[1, pl.ds(rb, S5_ROWS), :])
        n_re = a_re * x_re - a_im * x_im + s_re
        n_im = a_re * x_im + a_im * x_re + s_im
        return n_re, n_im

    zero = jnp.zeros((S5_ROWS, LANES), F32)
    lax.fori_loop(0, n_chunks, body, (zero, zero), unroll=4)

    for slot, (u_ref, y_ref, b) in enumerate(seqs):
        rows = pl.ds(slot, n_chunks, stride=S5_ROWS)
        x_in = jnp.concatenate([xin_ref[0, rows, :], xin_ref[1, rows, :]], axis=1).astype(BF16)
        y_ref[0, :, b * S5_COLS:(b + 1) * S5_COLS] = _dot(u_rows(u_ref, b), toep) + _dot(x_in, mout_ref[0])


def s5_scan(u_rows, l_f, r_f, l_b, r_b, m_st, m_out, a_re, a_im):
    g, n_chunks, _ = u_rows[0].shape
    batches = tuple(u.shape[2] // S5_COLS for u in u_rows)
    assert sum(batches) <= S5_ROWS
    gmap = lambda gi: (gi, 0, 0)
    io_specs = [pl.BlockSpec((1, n_chunks, u.shape[2]), gmap) for u in u_rows]
    return pl.pallas_call(
        functools.partial(_s5_kernel, batches=batches),
        out_shape=tuple(jax.ShapeDtypeStruct(u.shape, F32) for u in u_rows),
        grid=(g,),
        in_specs=io_specs + [
            pl.BlockSpec((1, S5_COLS, 2 * SSM_STATE), gmap),
            pl.BlockSpec((1, 2 * SSM_STATE, S5_COLS), gmap),
            pl.BlockSpec((1, S5_COLS, 2 * SSM_STATE), gmap),
            pl.BlockSpec((1, 2 * SSM_STATE, S5_COLS), gmap),
            pl.BlockSpec((1, S5_COLS, S5_STATE_COLS), gmap),
            pl.BlockSpec((1, S5_STATE_COLS, S5_COLS), gmap),
            pl.BlockSpec((1, 1, LANES), gmap),
            pl.BlockSpec((1, 1, LANES), gmap),
        ],
        out_specs=tuple(io_specs),
        scratch_shapes=[pltpu.VMEM((2, n_chunks * S5_ROWS, LANES), F32),
                        pltpu.VMEM((2, n_chunks * S5_ROWS, LANES), F32)],
        compiler_params=_params(("parallel",)),
        name="s5_scan",
    )(*u_rows, l_f, r_f, l_b, r_b, m_st, m_out, a_re, a_im)


def _gelu_tanh(x):
    return 0.5 * x * (1.0 + jnp.tanh(math.sqrt(2.0 / math.pi) * (x + 0.044715 * (x * x * x))))


def _post_mix_kernel(att_ref, ys_ref, u_ref, x_ref, mod_ref, dsk_ref, wglu_ref, bglu_ref,
                     ag_ref, sg_ref, wo_ref, g2_ref, x1_ref, h2_ref, y_cols):
    gate1 = mod_ref[0, 2:3, :]
    shift2 = mod_ref[0, 3:4, :]
    scale2 = mod_ref[0, 4:5, :]
    _from_chunk_rows(ys_ref, y_cols)
    y = jnp.concatenate([y_cols[j] for j in range(SSM_WIDTH // LANES)], axis=1)
    y = y + dsk_ref[...] * u_ref[0]
    y = _gelu_tanh(y)
    z = _dot(y.astype(BF16), wglu_ref[...]) + bglu_ref[...]
    ssm = y * jax.nn.sigmoid(z)
    att = att_ref[0].astype(F32)
    att_n = (att * _rms(att, ATT_WIDTH) * ag_ref[...]).astype(BF16)
    ssm_n = (ssm * _rms(ssm, SSM_WIDTH) * sg_ref[...]).astype(BF16)
    mix = _dot(att_n, wo_ref[0:ATT_WIDTH, :]) + _dot(ssm_n, wo_ref[ATT_WIDTH:2 * ATT_WIDTH, :])
    x1 = x_ref[0] + gate1 * mix
    x1_ref[0] = x1
    h2 = (x1 * _rms(x1, D_MODEL) * g2_ref[...]) * (1.0 + scale2) + shift2
    h2_ref[0] = h2.astype(BF16)


def post_mix(att, ys, u, x, mod, d_skip, w_glu, b_glu, att_g, ssm_g, w_o, g2):
    b, length, d = x.shape
    tm = S5_TILE
    const = lambda bi, i: (0, 0)
    tile = lambda w: pl.BlockSpec((1, tm, w), lambda bi, i: (bi, i, 0))
    return pl.pallas_call(
        _post_mix_kernel,
        out_shape=(jax.ShapeDtypeStruct((b, length, d), F32),
                   jax.ShapeDtypeStruct((b, length, d), BF16)),
        grid=(b, length // tm),
        in_specs=[
            tile(ATT_WIDTH),
            pl.BlockSpec((N_GROUPS, S5_SUB, S5_COLS), lambda bi, i: (0, i, bi)),
            tile(SSM_WIDTH), tile(d),
            pl.BlockSpec((1, 6, d), lambda bi, i: (bi, 0, 0)),
            pl.BlockSpec((1, SSM_WIDTH), const),
            pl.BlockSpec((SSM_WIDTH, SSM_WIDTH), const),
            pl.BlockSpec((1, SSM_WIDTH), const),
            pl.BlockSpec((1, ATT_WIDTH), const),
            pl.BlockSpec((1, SSM_WIDTH), const),
            pl.BlockSpec((d, d), const),
            pl.BlockSpec((1, d), const),
        ],
        out_specs=(tile(d), tile(d)),
        scratch_shapes=[pltpu.VMEM((SSM_WIDTH // LANES, tm, LANES), F32)],
        compiler_params=_params(("parallel", "parallel")),
        name="post_mix",
    )(att, ys, u, x, mod, d_skip, w_glu, b_glu, att_g, ssm_g, w_o, g2)


def _ffn_up_kernel(h_ref, w1_ref, w3_ref, o_ref):
    h = h_ref[0]
    a = _dot(h, w1_ref[...])
    g = _dot(h, w3_ref[...])
    o_ref[0] = (a * jax.nn.sigmoid(a) * g).astype(BF16)


def ffn_up(h2, w1, w3):
    b, length, d = h2.shape
    tm, tf = min(FFN_UP_TM, length), FFN_UP_TF
    return pl.pallas_call(
        _ffn_up_kernel,
        out_shape=jax.ShapeDtypeStruct((b, length, D_FF), BF16),
        grid=(D_FF // tf, b, length // tm),
        in_specs=[
            pl.BlockSpec((1, tm, d), lambda f, bi, i: (bi, i, 0)),
            pl.BlockSpec((d, tf), lambda f, bi, i: (0, f)),
            pl.BlockSpec((d, tf), lambda f, bi, i: (0, f)),
        ],
        out_specs=pl.BlockSpec((1, tm, tf), lambda f, bi, i: (bi, i, f)),
        compiler_params=_params(("parallel", "parallel", "parallel")),
        name="ffn_up",
    )(h2, w1, w3)


def _ffn_down_kernel(a_ref, x1_ref, mod_ref, w2_ref, o_ref):
    o_ref[0] = x1_ref[0] + mod_ref[0, 5:6, :] * _dot(a_ref[0], w2_ref[...])


def ffn_down(act, x1, mod, w2):
    b, length, d = x1.shape
    tm, tn = min(FFN_DOWN_TM, length), FFN_DOWN_TN
    return pl.pallas_call(
        _ffn_down_kernel,
        out_shape=jax.ShapeDtypeStruct((b, length, d), F32),
        grid=(d // tn, b, length // tm),
        in_specs=[
            pl.BlockSpec((1, tm, D_FF), lambda n, bi, i: (bi, i, 0)),
            pl.BlockSpec((1, tm, tn), lambda n, bi, i: (bi, i, n)),
            pl.BlockSpec((1, 6, tn), lambda n, bi, i: (bi, 0, n)),
            pl.BlockSpec((D_FF, tn), lambda n, bi, i: (0, n)),
        ],
        out_specs=pl.BlockSpec((1, tm, tn), lambda n, bi, i: (bi, i, n)),
        compiler_params=_params(("parallel", "parallel", "parallel")),
        name="ffn_down",
    )(act, x1, mod, w2)


def _cast_kernel(x_ref, o_ref):
    o_ref[...] = x_ref[...].astype(o_ref.dtype)


def cast_bf16(w):
    rows, cols = w.shape
    tr = rows
    while tr * cols * 4 > CAST_BLOCK_BYTES and tr % (4 * SUBLANES) == 0:
        tr //= 2
    return pl.pallas_call(
        _cast_kernel,
        out_shape=jax.ShapeDtypeStruct((rows, cols), BF16),
        grid=(rows // tr,),
        in_specs=[pl.BlockSpec((tr, cols), lambda i: (i, 0))],
        out_specs=pl.BlockSpec((tr, cols), lambda i: (i, 0)),
        compiler_params=_params(("parallel",)),
        name="cast_bf16",
    )(w)


def _swap_halves(a, axis=-1):
    lo, hi = jnp.split(a, 2, axis=axis)
    return jnp.concatenate([hi, lo], axis=axis)


def _prep_w_in_kernel(w_ref, o_ref):
    w = w_ref[...]
    q_cols = N_HEADS * QK_DIM
    half = ROPE_DIM // 2

    def with_swapped(r):
        return [r, r[:, half:], r[:, :half]]

    cols = [w[:, hd * QK_DIM:hd * QK_DIM + NOPE_DIM] for hd in range(N_HEADS)]
    for hd in range(N_HEADS):
        cols += with_swapped(w[:, hd * QK_DIM + NOPE_DIM:(hd + 1) * QK_DIM])
    cols.append(w[:, q_cols:q_cols + KV_RANK])
    cols += with_swapped(w[:, q_cols + KV_RANK:q_cols + KV_RANK + ROPE_DIM])
    cols.append(w[:, q_cols + KV_RANK + ROPE_DIM:])
    o_ref[...] = jnp.concatenate(cols, axis=1).astype(BF16)


def prep_w_in(w_in):
    d, n = w_in.shape
    return pl.pallas_call(
        _prep_w_in_kernel,
        out_shape=jax.ShapeDtypeStruct((d, C_END), BF16),
        grid=(d // W_PREP_ROWS,),
        in_specs=[pl.BlockSpec((W_PREP_ROWS, n), lambda i: (i, 0))],
        out_specs=pl.BlockSpec((W_PREP_ROWS, C_END), lambda i: (i, 0)),
        compiler_params=_params(("parallel",)),
        name="prep_w_in",
    )(w_in)


def prep_w_ukv(w_ukv):
    rank, n = w_ukv.shape
    src = lambda j: (0, 2 * (j % N_HEADS) + j // N_HEADS)
    return pl.pallas_call(
        _cast_kernel,
        out_shape=jax.ShapeDtypeStruct((rank, n), BF16),
        grid=(n // LANES,),
        in_specs=[pl.BlockSpec((rank, LANES), src)],
        out_specs=pl.BlockSpec((rank, LANES), lambda j: (0, j)),
        compiler_params=_params(("parallel",)),
        name="prep_w_ukv",
    )(w_ukv)


def _rope_gain(g):
    gr = g[NOPE_DIM:]
    return jnp.concatenate([gr, _swap_halves(gr)]).reshape(1, 2 * ROPE_DIM)


def _rope_tables(length):
    pos = jnp.arange(length, dtype=F32)
    inv_freq = ROPE_BASE ** (-jnp.arange(0, ROPE_DIM, 2, dtype=F32) / ROPE_DIM)
    ang = pos[:, None] * inv_freq[None, :]
    cos, sin = jnp.cos(ang), jnp.sin(ang)
    return jnp.tile(cos, (1, 4)), jnp.tile(jnp.concatenate([-sin, sin], axis=-1), (1, 2))


def kernel(x_prompt, x_sample, c_prompt, c_sample, w_ada, b_ada, norm_mix_g, w_in, kv_norm_g, w_ukv,
           q_norm_g, k_norm_g, lam_re, lam_im, log_dt, b_re, b_im, c_re, c_im, d_skip, w_glu, b_glu,
           att_out_g, ssm_out_g, w_o, norm_ffn_g, w1, w3, w2):
    assert w_ada.shape[0] == 1, "single-layer kernel"
    xs = (x_prompt, x_sample)
    batches = tuple(x.shape[0] for x in xs)
    length = x_prompt.shape[1]
    assert x_sample.shape[1] == length and sum(batches) <= S5_ROWS

    c_all = jnp.concatenate([c_prompt, c_sample], axis=0)
    c8 = jnp.pad(c_all, ((0, SUBLANES - c_all.shape[0]), (0, 0)))
    mod_all = ada_mod(c8, w_ada[0], b_ada[0]).reshape(SUBLANES, 6, D_MODEL)
    mods = (mod_all[:batches[0]], mod_all[batches[0]:batches[0] + batches[1]])

    w_all = prep_w_in(w_in[0])
    w_ukv_p = prep_w_ukv(w_ukv[0])
    row = lambda a: a.reshape(1, -1)
    q_gn, k_gn = row(q_norm_g[0][:NOPE_DIM]), row(k_norm_g[0][:NOPE_DIM])
    q_gr, k_gr = _rope_gain(q_norm_g[0]), _rope_gain(k_norm_g[0])
    cos_t, sin_t = _rope_tables(length)
    s5_ops = s5_matrices(lam_re[0], lam_im[0], log_dt[0], b_re[0], b_im[0], c_re[0], c_im[0])
    w_glu_b, w_o_b = cast_bf16(w_glu[0]), cast_bf16(w_o[0])
    w1_b, w3_b, w2_b = cast_bf16(w1[0]), cast_bf16(w3[0]), cast_bf16(w2[0])

    proj = [in_proj(x, m, row(norm_mix_g[0]), w_all, w_ukv_p, row(kv_norm_g[0]), q_gn, k_gn, q_gr, k_gr,
                    cos_t, sin_t) for x, m in zip(xs, mods)]
    atts = [attention(q, k, v) for q, k, v, _, _ in proj]
    y_rows = s5_scan([p[4] for p in proj], *s5_ops)

    outs = []
    for x, m, att, ys, p in zip(xs, mods, atts, y_rows, proj):
        x1, h2 = post_mix(att, ys, p[3], x, m, row(d_skip[0]), w_glu_b, row(b_glu[0]), row(att_out_g[0]),
                          row(ssm_out_g[0]), w_o_b, row(norm_ffn_g[0]))
        outs.append(ffn_down(ffn_up(h2, w1_b, w3_b), x1, m, w2_b))
    return tuple(outs)
```

```python
import functools
import math

import jax
import jax.numpy as jnp
import numpy as np
from jax import lax
from jax.experimental import pallas as pl
from jax.experimental.pallas import tpu as pltpu

F32 = jnp.float32
BF16 = jnp.bfloat16

D_MODEL = 2048
ATT_WIDTH = 1024
SSM_WIDTH = 1024
N_HEADS = 8
V_DIM = 128
NOPE_DIM = 128
ROPE_DIM = 64
QK_DIM = NOPE_DIM + ROPE_DIM
KV_RANK = 512
ROPE_BASE = 10000.0
SSM_GROUP = 16
N_GROUPS = SSM_WIDTH // SSM_GROUP
SSM_STATE = 64
D_FF = 5632
NORM_EPS = 1e-6

LANES = 128
SUBLANES = 8
VMEM_LIMIT = 56 * 1024 * 1024
CAST_BLOCK_BYTES = 4 * 1024 * 1024

ATT_TQ, ATT_TK = 1024, 256
FFN_UP_TM, FFN_UP_TF = 1024, 512
FFN_DOWN_TM, FFN_DOWN_TN = 512, 1024
W_PREP_ROWS = 256

C_QN = 0
C_QR = C_QN + N_HEADS * NOPE_DIM
C_KV = C_QR + N_HEADS * 2 * ROPE_DIM
C_KR = C_KV + KV_RANK
C_U = C_KR + 2 * ROPE_DIM
C_END = C_U + SSM_WIDTH

S5_CHUNK = 32
S5_ROWS = SUBLANES
S5_SUB = SUBLANES
S5_TILE = S5_SUB * S5_CHUNK
S5_COLS = S5_CHUNK * SSM_GROUP
S5_STATE_COLS = 4 * SSM_STATE
GROUPS_PER_COL = LANES // SSM_GROUP


def _params(sem):
    return pltpu.CompilerParams(dimension_semantics=sem, vmem_limit_bytes=VMEM_LIMIT)


def _dot(a, b):
    return jnp.dot(a, b, preferred_element_type=F32)


def _dot_f32(a, b):
    a_hi = a.astype(BF16)
    a_lo = (a - a_hi.astype(F32)).astype(BF16)
    b_hi = b.astype(BF16)
    b_lo = (b - b_hi.astype(F32)).astype(BF16)
    return _dot(a_hi, b_hi) + _dot(a_lo, b_hi) + _dot(a_hi, b_lo)


def _ada_kernel(c_ref, w_ref, b_ref, o_ref):
    c = c_ref[...]
    o_ref[...] = _dot_f32(c * jax.nn.sigmoid(c), w_ref[...]) + b_ref[...]


def ada_mod(c8, w_ada, b_ada):
    rows, d = c8.shape
    n = w_ada.shape[1]
    tn = 1024
    return pl.pallas_call(
        _ada_kernel,
        out_shape=jax.ShapeDtypeStruct((rows, n), F32),
        grid=(n // tn,),
        in_specs=[
            pl.BlockSpec((rows, d), lambda j: (0, 0)),
            pl.BlockSpec((d, tn), lambda j: (0, j)),
            pl.BlockSpec((1, tn), lambda j: (0, j)),
        ],
        out_specs=pl.BlockSpec((rows, tn), lambda j: (0, j)),
        compiler_params=_params(("arbitrary",)),
        name="ada_mod",
    )(c8, w_ada, b_ada.reshape(1, n))


def _rms(x, width):
    return lax.rsqrt(jnp.sum(x * x, axis=-1, keepdims=True) * (1.0 / width) + NORM_EPS)


def _granule_masks():
    pos = lax.broadcasted_iota(jnp.int32, (SUBLANES, LANES), 1) // SSM_GROUP
    return [pos == p for p in range(GROUPS_PER_COL)]


def _to_chunk_rows(u_cols, dst):
    masks = _granule_masks()
    for j in range(SSM_WIDTH // LANES):
        rolled = []
        for t in range(S5_CHUNK):
            v = u_cols[j, pl.ds(t, S5_SUB, stride=S5_CHUNK), :]
            r = t % GROUPS_PER_COL
            rolled.append(pltpu.roll(v, SSM_GROUP * r, axis=1) if r else v)
        for k in range(GROUPS_PER_COL):
            for d in range(S5_COLS // LANES):
                acc = rolled[GROUPS_PER_COL * d + GROUPS_PER_COL - 1]
                for rho in range(GROUPS_PER_COL - 2, -1, -1):
                    acc = jnp.where(masks[(k + rho) % GROUPS_PER_COL], rolled[GROUPS_PER_COL * d + rho], acc)
                dst[GROUPS_PER_COL * j + k, :, d * LANES:(d + 1) * LANES] = acc


def _from_chunk_rows(src, y_cols):
    masks = _granule_masks()
    for j in range(SSM_WIDTH // LANES):
        for d in range(S5_COLS // LANES):
            cols = [src[GROUPS_PER_COL * j + k, :, d * LANES:(d + 1) * LANES] for k in range(GROUPS_PER_COL)]
            for rho in range(GROUPS_PER_COL):
                acc = cols[GROUPS_PER_COL - 1]
                for k in range(GROUPS_PER_COL - 2, -1, -1):
                    acc = jnp.where(masks[(k + rho) % GROUPS_PER_COL], cols[k], acc)
                v = pltpu.roll(acc, LANES - SSM_GROUP * rho, axis=1) if rho else acc
                y_cols[j, pl.ds(GROUPS_PER_COL * d + rho, S5_SUB, stride=S5_CHUNK), :] = v


def _in_proj_kernel(x_ref, mod_ref, g1_ref, w_ref, wukv_ref, kvg_ref, qg_ref, kg_ref,
                    qgr_ref, kgr_ref, cos_ref, sin_ref, q_ref, k_ref, v_ref, u_ref, ur_ref, u_cols,
                    *, q_scale):
    x = x_ref[0]
    shift = mod_ref[0, 0:1, :]
    scale = mod_ref[0, 1:2, :]
    h = (x * _rms(x, D_MODEL) * g1_ref[...]) * (1.0 + scale) + shift
    h = h.astype(BF16)

    cos = cos_ref[...]
    sin = sin_ref[...]

    def rope(col, gain):
        cg = col * gain
        return cg * cos + pltpu.roll(cg, ROPE_DIM, axis=1) * sin

    qn = _dot(h, w_ref[:, C_QN:C_QR])
    qr = _dot(h, w_ref[:, C_QR:C_KV])
    qg_n = qg_ref[...]
    qg_r = qgr_ref[...]
    for hd in range(N_HEADS):
        qn_h = qn[:, hd * LANES:(hd + 1) * LANES]
        qr_h = qr[:, hd * LANES:(hd + 1) * LANES]
        ss = (jnp.sum(qn_h * qn_h, axis=-1, keepdims=True)
              + 0.5 * jnp.sum(qr_h * qr_h, axis=-1, keepdims=True))
        r = lax.rsqrt(ss * (1.0 / QK_DIM) + NORM_EPS) * q_scale
        q_ref[0, hd, :, 0:NOPE_DIM] = (qn_h * r * qg_n).astype(BF16)
        q_ref[0, hd, :, NOPE_DIM:QK_DIM] = (rope(qr_h, qg_r) * r)[:, 0:ROPE_DIM].astype(BF16)

    ckv = _dot(h, w_ref[:, C_KV:C_KR])
    ckv_n = (ckv * _rms(ckv, KV_RANK) * kvg_ref[...]).astype(BF16)
    kv = _dot(ckv_n, wukv_ref[...])
    kr = _dot(h, w_ref[:, C_KR:C_U])
    kr_ss = 0.5 * jnp.sum(kr * kr, axis=-1, keepdims=True)
    kr_rot = rope(kr, kgr_ref[...])
    kg_n = kg_ref[...]
    ones = jnp.ones((x.shape[0], LANES), BF16)
    for hd in range(N_HEADS):
        kn_h = kv[:, hd * LANES:(hd + 1) * LANES]
        ss = jnp.sum(kn_h * kn_h, axis=-1, keepdims=True) + kr_ss
        r = lax.rsqrt(ss * (1.0 / QK_DIM) + NORM_EPS)
        k_ref[0, hd, :, 0:NOPE_DIM] = (kn_h * r * kg_n).astype(BF16)
        k_ref[0, hd, :, NOPE_DIM:QK_DIM] = (kr_rot * r)[:, 0:ROPE_DIM].astype(BF16)
        v_h = kv[:, ATT_WIDTH + hd * LANES:ATT_WIDTH + (hd + 1) * LANES]
        v_ref[0, hd, :, 0:V_DIM] = v_h.astype(BF16)
        v_ref[0, hd, :, V_DIM:2 * V_DIM] = ones

    u = _dot(h, w_ref[:, C_U:C_END])
    u_ref[0] = u
    for j in range(SSM_WIDTH // LANES):
        u_cols[j] = u[:, j * LANES:(j + 1) * LANES]
    _to_chunk_rows(u_cols, ur_ref)


def in_proj(x, mod, g1, w_all, w_ukv, kv_g, q_gn, k_gn, q_gr, k_gr, cos_t, sin_t):
    b, length, d = x.shape
    tm = S5_TILE
    assert length % tm == 0
    const = lambda bi, i: (0, 0)
    kern = functools.partial(_in_proj_kernel, q_scale=math.log2(math.e) / math.sqrt(QK_DIM))
    return pl.pallas_call(
        kern,
        out_shape=(
            jax.ShapeDtypeStruct((b, N_HEADS, length, QK_DIM), BF16),
            jax.ShapeDtypeStruct((b, N_HEADS, length, QK_DIM), BF16),
            jax.ShapeDtypeStruct((b, N_HEADS, length, 2 * V_DIM), BF16),
            jax.ShapeDtypeStruct((b, length, SSM_WIDTH), F32),
            jax.ShapeDtypeStruct((N_GROUPS, length // S5_CHUNK, b * S5_COLS), F32),
        ),
        grid=(b, length // tm),
        in_specs=[
            pl.BlockSpec((1, tm, d), lambda bi, i: (bi, i, 0)),
            pl.BlockSpec((1, 6, d), lambda bi, i: (bi, 0, 0)),
            pl.BlockSpec((1, d), const),
            pl.BlockSpec((d, C_END), const),
            pl.BlockSpec((KV_RANK, 2 * ATT_WIDTH), const),
            pl.BlockSpec((1, KV_RANK), const),
            pl.BlockSpec((1, LANES), const),
            pl.BlockSpec((1, LANES), const),
            pl.BlockSpec((1, LANES), const),
            pl.BlockSpec((1, LANES), const),
            pl.BlockSpec((tm, LANES), lambda bi, i: (i, 0)),
            pl.BlockSpec((tm, LANES), lambda bi, i: (i, 0)),
        ],
        out_specs=(
            pl.BlockSpec((1, N_HEADS, tm, QK_DIM), lambda bi, i: (bi, 0, i, 0)),
            pl.BlockSpec((1, N_HEADS, tm, QK_DIM), lambda bi, i: (bi, 0, i, 0)),
            pl.BlockSpec((1, N_HEADS, tm, 2 * V_DIM), lambda bi, i: (bi, 0, i, 0)),
            pl.BlockSpec((1, tm, SSM_WIDTH), lambda bi, i: (bi, i, 0)),
            pl.BlockSpec((N_GROUPS, S5_SUB, S5_COLS), lambda bi, i: (0, i, bi)),
        ),
        scratch_shapes=[pltpu.VMEM((SSM_WIDTH // LANES, tm, LANES), F32)],
        compiler_params=_params(("parallel", "parallel")),
        name="in_proj",
    )(x, mod, g1, w_all, w_ukv, kv_g, q_gn, k_gn, q_gr, k_gr, cos_t, sin_t)


def _attn_kernel(q_ref, k_ref, v_ref, o_ref, *, tk):
    q = q_ref[0, 0]
    tq = q.shape[0]
    m = jnp.full((tq, 1), -jnp.inf, F32)
    acc = jnp.zeros((tq, 2 * V_DIM), F32)
    for j in range(k_ref.shape[2] // tk):
        kc = k_ref[0, 0, j * tk:(j + 1) * tk, :]
        vc = v_ref[0, 0, j * tk:(j + 1) * tk, :]
        s = lax.dot_general(q, kc, (((1,), (1,)), ((), ())), preferred_element_type=F32)
        m_new = jnp.maximum(m, jnp.max(s, axis=-1, keepdims=True))
        alpha = jnp.exp2(m - m_new)
        p = jnp.exp2(s - m_new).astype(BF16)
        acc = alpha * acc + _dot(p, vc)
        m = m_new
    o_ref[0] = (acc[:, 0:V_DIM] / acc[:, V_DIM:2 * V_DIM]).astype(o_ref.dtype)


def attention(q, k, v):
    b, h, length, _ = q.shape
    tq = min(ATT_TQ, length)
    tk = min(ATT_TK, length)
    return pl.pallas_call(
        functools.partial(_attn_kernel, tk=tk),
        out_shape=jax.ShapeDtypeStruct((b, length, h * V_DIM), BF16),
        grid=(b, h, length // tq),
        in_specs=[
            pl.BlockSpec((1, 1, tq, QK_DIM), lambda bi, hi, i: (bi, hi, i, 0)),
            pl.BlockSpec((1, 1, length, QK_DIM), lambda bi, hi, i: (bi, hi, 0, 0)),
            pl.BlockSpec((1, 1, length, 2 * V_DIM), lambda bi, hi, i: (bi, hi, 0, 0)),
        ],
        out_specs=pl.BlockSpec((1, tq, V_DIM), lambda bi, hi, i: (bi, i, hi)),
        compiler_params=_params(("parallel", "parallel", "parallel")),
        name="attention",
    )(q, k, v)


def _s5_steps():
    g = np.arange(N_GROUPS)[:, None] % GROUPS_PER_COL
    slot = np.arange(S5_CHUNK)[None, :]
    return GROUPS_PER_COL * (slot // GROUPS_PER_COL) + (slot % GROUPS_PER_COL - g) % GROUPS_PER_COL


def s5_matrices(lam_re, lam_im, log_dt, b_re, b_im, c_re, c_im):
    t = S5_CHUNK
    dt = jnp.exp(log_dt)[..., None]
    la, lb = lam_re * dt, lam_im * dt
    lb_re, lb_im = jnp.exp(la) * jnp.cos(lb), jnp.exp(la) * jnp.sin(lb)
    den = lam_re * lam_re + lam_im * lam_im
    n_re, n_im = lb_re - 1.0, lb_im
    cf_re = (n_re * lam_re + n_im * lam_im) / den
    cf_im = (n_im * lam_re - n_re * lam_im) / den
    bb_re = cf_re[..., None] * b_re[None] - cf_im[..., None] * b_im[None]
    bb_im = cf_re[..., None] * b_im[None] + cf_im[..., None] * b_re[None]

    steps = _s5_steps().astype(np.float32)

    def power(d, expo):
        e = jnp.asarray(expo)[:, None, :]
        mag = jnp.exp(la[d][..., None] * e)
        return mag * jnp.cos(lb[d][..., None] * e), mag * jnp.sin(lb[d][..., None] * e)

    cat = jnp.concatenate

    def rows_op(pw, xs1, xs2):
        n = len(xs1)
        pr = cat([jnp.swapaxes(p[0], 1, 2) for p in pw], axis=-1)[:, :, None, :]
        pi = cat([jnp.swapaxes(p[1], 1, 2) for p in pw], axis=-1)[:, :, None, :]
        x1 = cat([jnp.swapaxes(x, 1, 2) for x in xs1], axis=-1)[:, None]
        x2 = cat([jnp.swapaxes(x, 1, 2) for x in xs2], axis=-1)[:, None]
        return (pr * x1 + pi * x2).reshape(N_GROUPS, S5_COLS, n * SSM_STATE)

    def cols_op(pw, xs1, xs2):
        n = len(xs1)
        pr = cat([p[0] for p in pw], axis=1)[..., None]
        pi = cat([p[1] for p in pw], axis=1)[..., None]
        x1 = cat(xs1, axis=1)[:, :, None, :]
        x2 = cat(xs2, axis=1)[:, :, None, :]
        return (pr * x1 + pi * x2).reshape(N_GROUPS, n * SSM_STATE, S5_COLS)

    bf_re, bf_im, bb_re_, bb_im_ = bb_re[0], bb_im[0], bb_re[1], bb_im[1]
    cf_re_, cf_im_, cb_re, cb_im = c_re[0], c_im[0], c_re[1], c_im[1]
    tr = lambda x: jnp.swapaxes(x, 1, 2)
    p_sb, p_sf = power(1, steps), power(0, t - 1 - steps)
    p_of, p_ob = power(0, steps + 1), power(1, t - steps)

    lags = np.broadcast_to(np.arange(t, dtype=np.float32), (N_GROUPS, t))

    def lag_table(d):
        pr, pi = power(d, lags)
        pr, pi = jnp.swapaxes(pr, 1, 2)[:, :, None, :], jnp.swapaxes(pi, 1, 2)[:, :, None, :]
        cr, ci = c_re[d][:, None], c_im[d][:, None]
        hi = lax.Precision.HIGHEST
        k = (jnp.einsum("gkop,gpi->gkoi", pr * cr - pi * ci, bb_re[d], precision=hi)
             - jnp.einsum("gkop,gpi->gkoi", pr * ci + pi * cr, bb_im[d], precision=hi))
        return k

    flat = lambda k: k.transpose(0, 3, 1, 2).reshape(N_GROUPS, SSM_GROUP, S5_COLS)
    k_f = flat(lag_table(0))
    k_b = flat(lag_table(1)[:, ::-1])

    m_st = rows_op([p_sf, p_sb, p_sf, p_sb], [bf_re, bb_re_, bf_im, bb_im_],
                   [-bf_im, -bb_im_, bf_re, bb_re_]).astype(BF16)
    m_out = cols_op([p_of, p_ob, p_of, p_ob], [tr(cf_re_), tr(cb_re), -tr(cf_im_), -tr(cb_im)],
                    [-tr(cf_im_), -tr(cb_im), -tr(cf_re_), -tr(cb_re)]).astype(BF16)
    a_re = cat([jnp.exp(t * la[0]) * jnp.cos(t * lb[0]), jnp.exp(t * la[1]) * jnp.cos(t * lb[1])], axis=-1)[:, None, :]
    a_im = cat([jnp.exp(t * la[0]) * jnp.sin(t * lb[0]), jnp.exp(t * la[1]) * jnp.sin(t * lb[1])], axis=-1)[:, None, :]
    return k_f, k_b, m_st, m_out, a_re, a_im


def _s5_kernel(*refs, batches):
    n_in = len(batches)
    u_refs = refs[:n_in]
    kf_ref, kb_ref, mst_ref, mout_ref, are_ref, aim_ref = refs[n_in:n_in + 6]
    y_refs = refs[n_in + 6:2 * n_in + 6]
    st_ref, xin_ref, tnat_ref = refs[2 * n_in + 6:]
    n_chunks = u_refs[0].shape[1]

    lane = lax.broadcasted_iota(jnp.int32, (SSM_GROUP, S5_COLS), 1)
    k_f, k_b = kf_ref[0], kb_ref[0]
    for s in range(S5_CHUNK):
        lo, hi = s * SSM_GROUP, (s + 1) * SSM_GROUP
        fwd = jnp.where(lane >= lo, pltpu.roll(k_f, lo, axis=1) if lo else k_f, 0.0)
        bwd = jnp.where(lane < hi, pltpu.roll(k_b, hi, axis=1) if hi < S5_COLS else k_b, 0.0)
        tnat_ref[lo:hi, :] = (fwd + bwd).astype(BF16)

    def natural_of(idx):
        k = pl.program_id(0) % GROUPS_PER_COL
        d = lax.shift_right_logical(idx, LANES.bit_length() - 1)
        pos = lax.shift_right_logical(idx, SSM_GROUP.bit_length() - 1) & (GROUPS_PER_COL - 1)
        step = d * GROUPS_PER_COL + ((pos - k) & (GROUPS_PER_COL - 1))
        return step * SSM_GROUP + (idx & (SSM_GROUP - 1))

    r_idx = lax.broadcasted_iota(jnp.int32, (S5_COLS, S5_COLS), 0)
    c_idx = lax.broadcasted_iota(jnp.int32, (S5_COLS, S5_COLS), 1)
    pick_rows = jnp.where(natural_of(r_idx) == c_idx, 1.0, 0.0).astype(BF16)
    pick_cols = jnp.where(r_idx == natural_of(c_idx), 1.0, 0.0).astype(BF16)
    toep = _dot(_dot(pick_rows, tnat_ref[...]).astype(BF16), pick_cols).astype(BF16)
    seqs = [(u_ref, y_ref, b) for u_ref, y_ref, nb in zip(u_refs, y_refs, batches) for b in range(nb)]

    def u_rows(u_ref, b):
        return u_ref[0, :, b * S5_COLS:(b + 1) * S5_COLS].astype(BF16)

    for slot in range(S5_ROWS):
        rows = pl.ds(slot, n_chunks, stride=S5_ROWS)
        if slot < len(seqs):
            st = _dot(u_rows(seqs[slot][0], seqs[slot][2]), mst_ref[0])
        else:
            st = jnp.zeros((n_chunks, S5_STATE_COLS), F32)
        st_ref[0, rows, :] = st[:, 0:LANES]
        st_ref[1, rows, :] = st[:, LANES:2 * LANES]

    a_re = jnp.broadcast_to(are_ref[0], (S5_ROWS, LANES))
    a_im = jnp.broadcast_to(aim_ref[0], (S5_ROWS, LANES))
    is_fwd = lax.broadcasted_iota(jnp.int32, (S5_ROWS, LANES), 1) < SSM_STATE
    half = SSM_STATE

    def body(i, carry):
        x_re, x_im = carry
        rf = pl.multiple_of(i * S5_ROWS, S5_ROWS)
        rb = pl.multiple_of((n_chunks - 1 - i) * S5_ROWS, S5_ROWS)
        xin_ref[0, pl.ds(rf, S5_ROWS), 0:half] = x_re[:, 0:half]
        xin_ref[0, pl.ds(rb, S5_ROWS), half:LANES] = x_re[:, half:LANES]
        xin_ref[1, pl.ds(rf, S5_ROWS), 0:half] = x_im[:, 0:half]
        xin_ref[1, pl.ds(rb, S5_ROWS), half:LANES] = x_im[:, half:LANES]
        s_re = jnp.where(is_fwd, st_ref[0, pl.ds(rf, S5_ROWS), :], st_ref[0, pl.ds(rb, S5_ROWS), :])
        s_im = jnp.where(is_fwd, st_ref[1, pl.ds(rf, S5_ROWS), :], st_ref[1, pl.ds(rb, S5_ROWS), :])
        n_re = a_re * x_re - a_im * x_im + s_re
        n_im = a_re * x_im + a_im * x_re + s_im
        return n_re, n_im

    zero = jnp.zeros((S5_ROWS, LANES), F32)
    lax.fori_loop(0, n_chunks, body, (zero, zero), unroll=4)

    for slot, (u_ref, y_ref, b) in enumerate(seqs):
        rows = pl.ds(slot, n_chunks, stride=S5_ROWS)
        x_in = jnp.concatenate([xin_ref[0, rows, :], xin_ref[1, rows, :]], axis=1).astype(BF16)
        y_ref[0, :, b * S5_COLS:(b + 1) * S5_COLS] = _dot(u_rows(u_ref, b), toep) + _dot(x_in, mout_ref[0])


def s5_scan(u_rows, k_f, k_b, m_st, m_out, a_re, a_im):
    g, n_chunks, _ = u_rows[0].shape
    batches = tuple(u.shape[2] // S5_COLS for u in u_rows)
    assert sum(batches) <= S5_ROWS
    gmap = lambda gi: (gi, 0, 0)
    io_specs = [pl.BlockSpec((1, n_chunks, u.shape[2]), gmap) for u in u_rows]
    return pl.pallas_call(
        functools.partial(_s5_kernel, batches=batches),
        out_shape=tuple(jax.ShapeDtypeStruct(u.shape, F32) for u in u_rows),
        grid=(g,),
        in_specs=io_specs + [
            pl.BlockSpec((1, SSM_GROUP, S5_COLS), gmap),
            pl.BlockSpec((1, SSM_GROUP, S5_COLS), gmap),
            pl.BlockSpec((1, S5_COLS, S5_STATE_COLS), gmap),
            pl.BlockSpec((1, S5_STATE_COLS, S5_COLS), gmap),
            pl.BlockSpec((1, 1, LANES), gmap),
            pl.BlockSpec((1, 1, LANES), gmap),
        ],
        out_specs=tuple(io_specs),
        scratch_shapes=[pltpu.VMEM((2, n_chunks * S5_ROWS, LANES), F32),
                        pltpu.VMEM((2, n_chunks * S5_ROWS, LANES), F32),
                        pltpu.VMEM((S5_COLS, S5_COLS), BF16)],
        compiler_params=_params(("parallel",)),
        name="s5_scan",
    )(*u_rows, k_f, k_b, m_st, m_out, a_re, a_im)


def _gelu_tanh(x):
    return 0.5 * x * (1.0 + jnp.tanh(math.sqrt(2.0 / math.pi) * (x + 0.044715 * (x * x * x))))


def _post_mix_kernel(att_ref, ys_ref, u_ref, x_ref, mod_ref, dsk_ref, wglu_ref, bglu_ref,
                     ag_ref, sg_ref, wo_ref, g2_ref, x1_ref, h2_ref, y_cols):
    gate1 = mod_ref[0, 2:3, :]
    shift2 = mod_ref[0, 3:4, :]
    scale2 = mod_ref[0, 4:5, :]
    _from_chunk_rows(ys_ref, y_cols)
    y = jnp.concatenate([y_cols[j] for j in range(SSM_WIDTH // LANES)], axis=1)
    y = y + dsk_ref[...] * u_ref[0]
    y = _gelu_tanh(y)
    z = _dot(y.astype(BF16), wglu_ref[...]) + bglu_ref[...]
    ssm = y * jax.nn.sigmoid(z)
    att = att_ref[0].astype(F32)
    att_n = (att * _rms(att, ATT_WIDTH) * ag_ref[...]).astype(BF16)
    ssm_n = (ssm * _rms(ssm, SSM_WIDTH) * sg_ref[...]).astype(BF16)
    mix = _dot(att_n, wo_ref[0:ATT_WIDTH, :]) + _dot(ssm_n, wo_ref[ATT_WIDTH:2 * ATT_WIDTH, :])
    x1 = x_ref[0] + gate1 * mix
    x1_ref[0] = x1
    h2 = (x1 * _rms(x1, D_MODEL) * g2_ref[...]) * (1.0 + scale2) + shift2
    h2_ref[0] = h2.astype(BF16)


def post_mix(att, ys, u, x, mod, d_skip, w_glu, b_glu, att_g, ssm_g, w_o, g2):
    b, length, d = x.shape
    tm = S5_TILE
    const = lambda bi, i: (0, 0)
    tile = lambda w: pl.BlockSpec((1, tm, w), lambda bi, i: (bi, i, 0))
    return pl.pallas_call(
        _post_mix_kernel,
        out_shape=(jax.ShapeDtypeStruct((b, length, d), F32),
                   jax.ShapeDtypeStruct((b, length, d), BF16)),
        grid=(b, length // tm),
        in_specs=[
            tile(ATT_WIDTH),
            pl.BlockSpec((N_GROUPS, S5_SUB, S5_COLS), lambda bi, i: (0, i, bi)),
            tile(SSM_WIDTH), tile(d),
            pl.BlockSpec((1, 6, d), lambda bi, i: (bi, 0, 0)),
            pl.BlockSpec((1, SSM_WIDTH), const),
            pl.BlockSpec((SSM_WIDTH, SSM_WIDTH), const),
            pl.BlockSpec((1, SSM_WIDTH), const),
            pl.BlockSpec((1, ATT_WIDTH), const),
            pl.BlockSpec((1, SSM_WIDTH), const),
            pl.BlockSpec((d, d), const),
            pl.BlockSpec((1, d), const),
        ],
        out_specs=(tile(d), tile(d)),
        scratch_shapes=[pltpu.VMEM((SSM_WIDTH // LANES, tm, LANES), F32)],
        compiler_params=_params(("parallel", "parallel")),
        name="post_mix",
    )(att, ys, u, x, mod, d_skip, w_glu, b_glu, att_g, ssm_g, w_o, g2)


def _ffn_up_kernel(h_ref, w1_ref, w3_ref, o_ref):
    h = h_ref[0]
    a = _dot(h, w1_ref[...])
    g = _dot(h, w3_ref[...])
    o_ref[0] = (a * jax.nn.sigmoid(a) * g).astype(BF16)


def ffn_up(h2, w1, w3):
    b, length, d = h2.shape
    tm, tf = min(FFN_UP_TM, length), FFN_UP_TF
    return pl.pallas_call(
        _ffn_up_kernel,
        out_shape=jax.ShapeDtypeStruct((b, length, D_FF), BF16),
        grid=(D_FF // tf, b, length // tm),
        in_specs=[
            pl.BlockSpec((1, tm, d), lambda f, bi, i: (bi, i, 0)),
            pl.BlockSpec((d, tf), lambda f, bi, i: (0, f)),
            pl.BlockSpec((d, tf), lambda f, bi, i: (0, f)),
        ],
        out_specs=pl.BlockSpec((1, tm, tf), lambda f, bi, i: (bi, i, f)),
        compiler_params=_params(("parallel", "parallel", "parallel")),
        name="ffn_up",
    )(h2, w1, w3)


def _ffn_down_kernel(a_ref, x1_ref, mod_ref, w2_ref, o_ref):
    o_ref[0] = x1_ref[0] + mod_ref[0, 5:6, :] * _dot(a_ref[0], w2_ref[...])


def ffn_down(act, x1, mod, w2):
    b, length, d = x1.shape
    tm, tn = min(FFN_DOWN_TM, length), FFN_DOWN_TN
    return pl.pallas_call(
        _ffn_down_kernel,
        out_shape=jax.ShapeDtypeStruct((b, length, d), F32),
        grid=(d // tn, b, length // tm),
        in_specs=[
            pl.BlockSpec((1, tm, D_FF), lambda n, bi, i: (bi, i, 0)),
            pl.BlockSpec((1, tm, tn), lambda n, bi, i: (bi, i, n)),
            pl.BlockSpec((1, 6, tn), lambda n, bi, i: (bi, 0, n)),
            pl.BlockSpec((D_FF, tn), lambda n, bi, i: (0, n)),
        ],
        out_specs=pl.BlockSpec((1, tm, tn), lambda n, bi, i: (bi, i, n)),
        compiler_params=_params(("parallel", "parallel", "parallel")),
        name="ffn_down",
    )(act, x1, mod, w2)


def _cast_kernel(x_ref, o_ref):
    o_ref[...] = x_ref[...].astype(o_ref.dtype)


def cast_bf16(w):
    rows, cols = w.shape
    tr = rows
    while tr * cols * 4 > CAST_BLOCK_BYTES and tr % (4 * SUBLANES) == 0:
        tr //= 2
    return pl.pallas_call(
        _cast_kernel,
        out_shape=jax.ShapeDtypeStruct((rows, cols), BF16),
        grid=(rows // tr,),
        in_specs=[pl.BlockSpec((tr, cols), lambda i: (i, 0))],
        out_specs=pl.BlockSpec((tr, cols), lambda i: (i, 0)),
        compiler_params=_params(("parallel",)),
        name="cast_bf16",
    )(w)


def _swap_halves(a, axis=-1):
    lo, hi = jnp.split(a, 2, axis=axis)
    return jnp.concatenate([hi, lo], axis=axis)


def _prep_w_in_kernel(w_ref, o_ref):
    w = w_ref[...]
    q_cols = N_HEADS * QK_DIM
    half = ROPE_DIM // 2

    def with_swapped(r):
        return [r, r[:, half:], r[:, :half]]

    cols = [w[:, hd * QK_DIM:hd * QK_DIM + NOPE_DIM] for hd in range(N_HEADS)]
    for hd in range(N_HEADS):
        cols += with_swapped(w[:, hd * QK_DIM + NOPE_DIM:(hd + 1) * QK_DIM])
    cols.append(w[:, q_cols:q_cols + KV_RANK])
    cols += with_swapped(w[:, q_cols + KV_RANK:q_cols + KV_RANK + ROPE_DIM])
    cols.append(w[:, q_cols + KV_RANK + ROPE_DIM:])
    o_ref[...] = jnp.concatenate(cols, axis=1).astype(BF16)


def prep_w_in(w_in):
    d, n = w_in.shape
    return pl.pallas_call(
        _prep_w_in_kernel,
        out_shape=jax.ShapeDtypeStruct((d, C_END), BF16),
        grid=(d // W_PREP_ROWS,),
        in_specs=[pl.BlockSpec((W_PREP_ROWS, n), lambda i: (i, 0))],
        out_specs=pl.BlockSpec((W_PREP_ROWS, C_END), lambda i: (i, 0)),
        compiler_params=_params(("parallel",)),
        name="prep_w_in",
    )(w_in)


def prep_w_ukv(w_ukv):
    rank, n = w_ukv.shape
    src = lambda j: (0, 2 * (j % N_HEADS) + j // N_HEADS)
    return pl.pallas_call(
        _cast_kernel,
        out_shape=jax.ShapeDtypeStruct((rank, n), BF16),
        grid=(n // LANES,),
        in_specs=[pl.BlockSpec((rank, LANES), src)],
        out_specs=pl.BlockSpec((rank, LANES), lambda j: (0, j)),
        compiler_params=_params(("parallel",)),
        name="prep_w_ukv",
    )(w_ukv)


def _rope_gain(g):
    gr = g[NOPE_DIM:]
    return jnp.concatenate([gr, _swap_halves(gr)]).reshape(1, 2 * ROPE_DIM)


def _rope_tables(length):
    pos = jnp.arange(length, dtype=F32)
    inv_freq = ROPE_BASE ** (-jnp.arange(0, ROPE_DIM, 2, dtype=F32) / ROPE_DIM)
    ang = pos[:, None] * inv_freq[None, :]
    cos, sin = jnp.cos(ang), jnp.sin(ang)
    return jnp.tile(cos, (1, 4)), jnp.tile(jnp.concatenate([-sin, sin], axis=-1), (1, 2))


def kernel(x_prompt, x_sample, c_prompt, c_sample, w_ada, b_ada, norm_mix_g, w_in, kv_norm_g, w_ukv,
           q_norm_g, k_norm_g, lam_re, lam_im, log_dt, b_re, b_im, c_re, c_im, d_skip, w_glu, b_glu,
           att_out_g, ssm_out_g, w_o, norm_ffn_g, w1, w3, w2):
    assert w_ada.shape[0] == 1, "single-layer kernel"
    xs = (x_prompt, x_sample)
    batches = tuple(x.shape[0] for x in xs)
    length = x_prompt.shape[1]
    assert x_sample.shape[1] == length and sum(batches) <= S5_ROWS

    c_all = jnp.concatenate([c_prompt, c_sample], axis=0)
    c8 = jnp.pad(c_all, ((0, SUBLANES - c_all.shape[0]), (0, 0)))
    mod_all = ada_mod(c8, w_ada[0], b_ada[0]).reshape(SUBLANES, 6, D_MODEL)
    mods = (mod_all[:batches[0]], mod_all[batches[0]:batches[0] + batches[1]])

    w_all = prep_w_in(w_in[0])
    w_ukv_p = prep_w_ukv(w_ukv[0])
    row = lambda a: a.reshape(1, -1)
    q_gn, k_gn = row(q_norm_g[0][:NOPE_DIM]), row(k_norm_g[0][:NOPE_DIM])
    q_gr, k_gr = _rope_gain(q_norm_g[0]), _rope_gain(k_norm_g[0])
    cos_t, sin_t = _rope_tables(length)
    s5_ops = s5_matrices(lam_re[0], lam_im[0], log_dt[0], b_re[0], b_im[0], c_re[0], c_im[0])
    w_glu_b, w_o_b = cast_bf16(w_glu[0]), cast_bf16(w_o[0])
    w1_b, w3_b, w2_b = cast_bf16(w1[0]), cast_bf16(w3[0]), cast_bf16(w2[0])

    proj = [in_proj(x, m, row(norm_mix_g[0]), w_all, w_ukv_p, row(kv_norm_g[0]), q_gn, k_gn, q_gr, k_gr,
                    cos_t, sin_t) for x, m in zip(xs, mods)]
    atts = [attention(q, k, v) for q, k, v, _, _ in proj]
    y_rows = s5_scan([p[4] for p in proj], *s5_ops)

    outs = []
    for x, m, att, ys, p in zip(xs, mods, atts, y_rows, proj):
        x1, h2 = post_mix(att, ys, p[3], x, m, row(d_skip[0]), w_glu_b, row(b_glu[0]), row(att_out_g[0]),
                          row(ssm_out_g[0]), w_o_b, row(norm_ffn_g[0]))
        outs.append(ffn_down(ffn_up(h2, w1_b, w3_b), x1, m, w2_b))
    return tuple(outs)
```

```python
import functools
import math

import jax
import jax.numpy as jnp
import numpy as np
from jax import lax
from jax.experimental import pallas as pl
from jax.experimental.pallas import tpu as pltpu

F32 = jnp.float32
BF16 = jnp.bfloat16

D_MODEL = 2048
ATT_WIDTH = 1024
SSM_WIDTH = 1024
N_HEADS = 8
V_DIM = 128
NOPE_DIM = 128
ROPE_DIM = 64
QK_DIM = NOPE_DIM + ROPE_DIM
KV_RANK = 512
ROPE_BASE = 10000.0
SSM_GROUP = 16
N_GROUPS = SSM_WIDTH // SSM_GROUP
SSM_STATE = 64
D_FF = 5632
NORM_EPS = 1e-6

LANES = 128
SUBLANES = 8
VMEM_LIMIT = 56 * 1024 * 1024
CAST_BLOCK_BYTES = 4 * 1024 * 1024

ATT_TQ, ATT_TK = 1024, 256
FFN_UP_TM, FFN_UP_TF = 1024, 512
FFN_DOWN_TM, FFN_DOWN_TN = 512, 1024
W_PREP_ROWS = 256

C_QN = 0
C_QR = C_QN + N_HEADS * NOPE_DIM
C_KV = C_QR + N_HEADS * 2 * ROPE_DIM
C_KR = C_KV + KV_RANK
C_U = C_KR + 2 * ROPE_DIM
C_END = C_U + SSM_WIDTH

S5_CHUNK = 32
S5_ROWS = SUBLANES
S5_SUB = SUBLANES
S5_TILE = S5_SUB * S5_CHUNK
S5_COLS = S5_CHUNK * SSM_GROUP
S5_STATE_COLS = 4 * SSM_STATE
GROUPS_PER_COL = LANES // SSM_GROUP


def _params(sem):
    return pltpu.CompilerParams(dimension_semantics=sem, vmem_limit_bytes=VMEM_LIMIT)


def _dot(a, b):
    return jnp.dot(a, b, preferred_element_type=F32)


def _dot_f32(a, b):
    a_hi = a.astype(BF16)
    a_lo = (a - a_hi.astype(F32)).astype(BF16)
    b_hi = b.astype(BF16)
    b_lo = (b - b_hi.astype(F32)).astype(BF16)
    return _dot(a_hi, b_hi) + _dot(a_lo, b_hi) + _dot(a_hi, b_lo)


def _ada_kernel(c_ref, w_ref, b_ref, o_ref):
    c = c_ref[...]
    o_ref[...] = _dot_f32(c * jax.nn.sigmoid(c), w_ref[...]) + b_ref[...]


def ada_mod(c8, w_ada, b_ada):
    rows, d = c8.shape
    n = w_ada.shape[1]
    tn = 1024
    return pl.pallas_call(
        _ada_kernel,
        out_shape=jax.ShapeDtypeStruct((rows, n), F32),
        grid=(n // tn,),
        in_specs=[
            pl.BlockSpec((rows, d), lambda j: (0, 0)),
            pl.BlockSpec((d, tn), lambda j: (0, j)),
            pl.BlockSpec((1, tn), lambda j: (0, j)),
        ],
        out_specs=pl.BlockSpec((rows, tn), lambda j: (0, j)),
        compiler_params=_params(("arbitrary",)),
        name="ada_mod",
    )(c8, w_ada, b_ada.reshape(1, n))


def _rms(x, width):
    return lax.rsqrt(jnp.sum(x * x, axis=-1, keepdims=True) * (1.0 / width) + NORM_EPS)


def _granule_masks():
    pos = lax.broadcasted_iota(jnp.int32, (SUBLANES, LANES), 1) // SSM_GROUP
    return [pos == p for p in range(GROUPS_PER_COL)]


def _to_chunk_rows(u_cols, dst):
    masks = _granule_masks()
    for j in range(SSM_WIDTH // LANES):
        rolled = []
        for t in range(S5_CHUNK):
            v = u_cols[j, pl.ds(t, S5_SUB, stride=S5_CHUNK), :]
            r = t % GROUPS_PER_COL
            rolled.append(pltpu.roll(v, SSM_GROUP * r, axis=1) if r else v)
        for k in range(GROUPS_PER_COL):
            for d in range(S5_COLS // LANES):
                acc = rolled[GROUPS_PER_COL * d + GROUPS_PER_COL - 1]
                for rho in range(GROUPS_PER_COL - 2, -1, -1):
                    acc = jnp.where(masks[(k + rho) % GROUPS_PER_COL], rolled[GROUPS_PER_COL * d + rho], acc)
                dst[GROUPS_PER_COL * j + k, :, d * LANES:(d + 1) * LANES] = acc


def _from_chunk_rows(src, y_cols):
    masks = _granule_masks()
    for j in range(SSM_WIDTH // LANES):
        for d in range(S5_COLS // LANES):
            cols = [src[GROUPS_PER_COL * j + k, :, d * LANES:(d + 1) * LANES] for k in range(GROUPS_PER_COL)]
            for rho in range(GROUPS_PER_COL):
                acc = cols[GROUPS_PER_COL - 1]
                for k in range(GROUPS_PER_COL - 2, -1, -1):
                    acc = jnp.where(masks[(k + rho) % GROUPS_PER_COL], cols[k], acc)
                v = pltpu.roll(acc, LANES - SSM_GROUP * rho, axis=1) if rho else acc
                y_cols[j, pl.ds(GROUPS_PER_COL * d + rho, S5_SUB, stride=S5_CHUNK), :] = v


def _in_proj_kernel(x_ref, mod_ref, g1_ref, w_ref, wukv_ref, kvg_ref, qg_ref, kg_ref,
                    qgr_ref, kgr_ref, cos_ref, sin_ref, q_ref, k_ref, v_ref, u_ref, ur_ref, u_cols,
                    *, q_scale):
    x = x_ref[0]
    shift = mod_ref[0, 0:1, :]
    scale = mod_ref[0, 1:2, :]
    h = (x * _rms(x, D_MODEL) * g1_ref[...]) * (1.0 + scale) + shift
    h = h.astype(BF16)

    cos = cos_ref[...]
    sin = sin_ref[...]

    def rope(col, gain):
        cg = col * gain
        return cg * cos + pltpu.roll(cg, ROPE_DIM, axis=1) * sin

    qn = _dot(h, w_ref[:, C_QN:C_QR])
    qr = _dot(h, w_ref[:, C_QR:C_KV])
    qg_n = qg_ref[...]
    qg_r = qgr_ref[...]
    for hd in range(N_HEADS):
        qn_h = qn[:, hd * LANES:(hd + 1) * LANES]
        qr_h = qr[:, hd * LANES:(hd + 1) * LANES]
        ss = (jnp.sum(qn_h * qn_h, axis=-1, keepdims=True)
              + 0.5 * jnp.sum(qr_h * qr_h, axis=-1, keepdims=True))
        r = lax.rsqrt(ss * (1.0 / QK_DIM) + NORM_EPS) * q_scale
        q_ref[0, hd, :, 0:NOPE_DIM] = (qn_h * r * qg_n).astype(BF16)
        q_ref[0, hd, :, NOPE_DIM:QK_DIM] = (rope(qr_h, qg_r) * r)[:, 0:ROPE_DIM].astype(BF16)

    ckv = _dot(h, w_ref[:, C_KV:C_KR])
    ckv_n = (ckv * _rms(ckv, KV_RANK) * kvg_ref[...]).astype(BF16)
    kv = _dot(ckv_n, wukv_ref[...])
    kr = _dot(h, w_ref[:, C_KR:C_U])
    kr_ss = 0.5 * jnp.sum(kr * kr, axis=-1, keepdims=True)
    kr_rot = rope(kr, kgr_ref[...])
    kg_n = kg_ref[...]
    ones = jnp.ones((x.shape[0], LANES), BF16)
    for hd in range(N_HEADS):
        kn_h = kv[:, hd * LANES:(hd + 1) * LANES]
        ss = jnp.sum(kn_h * kn_h, axis=-1, keepdims=True) + kr_ss
        r = lax.rsqrt(ss * (1.0 / QK_DIM) + NORM_EPS)
        k_ref[0, hd, :, 0:NOPE_DIM] = (kn_h * r * kg_n).astype(BF16)
        k_ref[0, hd, :, NOPE_DIM:QK_DIM] = (kr_rot * r)[:, 0:ROPE_DIM].astype(BF16)
        v_h = kv[:, ATT_WIDTH + hd * LANES:ATT_WIDTH + (hd + 1) * LANES]
        v_ref[0, hd, :, 0:V_DIM] = v_h.astype(BF16)
        v_ref[0, hd, :, V_DIM:2 * V_DIM] = ones

    u = _dot(h, w_ref[:, C_U:C_END])
    u_ref[0] = u
    for j in range(SSM_WIDTH // LANES):
        u_cols[j] = u[:, j * LANES:(j + 1) * LANES]
    _to_chunk_rows(u_cols, ur_ref)


def in_proj(x, mod, g1, w_all, w_ukv, kv_g, q_gn, k_gn, q_gr, k_gr, cos_t, sin_t):
    b, length, d = x.shape
    tm = S5_TILE
    assert length % tm == 0
    const = lambda bi, i: (0, 0)
    kern = functools.partial(_in_proj_kernel, q_scale=math.log2(math.e) / math.sqrt(QK_DIM))
    return pl.pallas_call(
        kern,
        out_shape=(
            jax.ShapeDtypeStruct((b, N_HEADS, length, QK_DIM), BF16),
            jax.ShapeDtypeStruct((b, N_HEADS, length, QK_DIM), BF16),
            jax.ShapeDtypeStruct((b, N_HEADS, length, 2 * V_DIM), BF16),
            jax.ShapeDtypeStruct((b, length, SSM_WIDTH), F32),
            jax.ShapeDtypeStruct((N_GROUPS, length // S5_CHUNK, b * S5_COLS), F32),
        ),
        grid=(b, length // tm),
        in_specs=[
            pl.BlockSpec((1, tm, d), lambda bi, i: (bi, i, 0)),
            pl.BlockSpec((1, 6, d), lambda bi, i: (bi, 0, 0)),
            pl.BlockSpec((1, d), const),
            pl.BlockSpec((d, C_END), const),
            pl.BlockSpec((KV_RANK, 2 * ATT_WIDTH), const),
            pl.BlockSpec((1, KV_RANK), const),
            pl.BlockSpec((1, LANES), const),
            pl.BlockSpec((1, LANES), const),
            pl.BlockSpec((1, LANES), const),
            pl.BlockSpec((1, LANES), const),
            pl.BlockSpec((tm, LANES), lambda bi, i: (i, 0)),
            pl.BlockSpec((tm, LANES), lambda bi, i: (i, 0)),
        ],
        out_specs=(
            pl.BlockSpec((1, N_HEADS, tm, QK_DIM), lambda bi, i: (bi, 0, i, 0)),
            pl.BlockSpec((1, N_HEADS, tm, QK_DIM), lambda bi, i: (bi, 0, i, 0)),
            pl.BlockSpec((1, N_HEADS, tm, 2 * V_DIM), lambda bi, i: (bi, 0, i, 0)),
            pl.BlockSpec((1, tm, SSM_WIDTH), lambda bi, i: (bi, i, 0)),
            pl.BlockSpec((N_GROUPS, S5_SUB, S5_COLS), lambda bi, i: (0, i, bi)),
        ),
        scratch_shapes=[pltpu.VMEM((SSM_WIDTH // LANES, tm, LANES), F32)],
        compiler_params=_params(("parallel", "parallel")),
        name="in_proj",
    )(x, mod, g1, w_all, w_ukv, kv_g, q_gn, k_gn, q_gr, k_gr, cos_t, sin_t)


def _attn_kernel(q_ref, k_ref, v_ref, o_ref, *, tk):
    q = q_ref[0, 0]
    tq = q.shape[0]
    m = jnp.full((tq, 1), -jnp.inf, F32)
    acc = jnp.zeros((tq, 2 * V_DIM), F32)
    for j in range(k_ref.shape[2] // tk):
        kc = k_ref[0, 0, j * tk:(j + 1) * tk, :]
        vc = v_ref[0, 0, j * tk:(j + 1) * tk, :]
        s = lax.dot_general(q, kc, (((1,), (1,)), ((), ())), preferred_element_type=F32)
        m_new = jnp.maximum(m, jnp.max(s, axis=-1, keepdims=True))
        alpha = jnp.exp2(m - m_new)
        p = jnp.exp2(s - m_new).astype(BF16)
        acc = alpha * acc + _dot(p, vc)
        m = m_new
    o_ref[0] = (acc[:, 0:V_DIM] / acc[:, V_DIM:2 * V_DIM]).astype(o_ref.dtype)


def attention(q, k, v):
    b, h, length, _ = q.shape
    tq = min(ATT_TQ, length)
    tk = min(ATT_TK, length)
    return pl.pallas_call(
        functools.partial(_attn_kernel, tk=tk),
        out_shape=jax.ShapeDtypeStruct((b, length, h * V_DIM), BF16),
        grid=(b, h, length // tq),
        in_specs=[
            pl.BlockSpec((1, 1, tq, QK_DIM), lambda bi, hi, i: (bi, hi, i, 0)),
            pl.BlockSpec((1, 1, length, QK_DIM), lambda bi, hi, i: (bi, hi, 0, 0)),
            pl.BlockSpec((1, 1, length, 2 * V_DIM), lambda bi, hi, i: (bi, hi, 0, 0)),
        ],
        out_specs=pl.BlockSpec((1, tq, V_DIM), lambda bi, hi, i: (bi, i, hi)),
        compiler_params=_params(("parallel", "parallel", "parallel")),
        name="attention",
    )(q, k, v)


def _s5_steps():
    g = np.arange(N_GROUPS)[:, None] % GROUPS_PER_COL
    slot = np.arange(S5_CHUNK)[None, :]
    return GROUPS_PER_COL * (slot // GROUPS_PER_COL) + (slot % GROUPS_PER_COL - g) % GROUPS_PER_COL


def s5_matrices(lam_re, lam_im, log_dt, b_re, b_im, c_re, c_im):
    t = S5_CHUNK
    dt = jnp.exp(log_dt)[..., None]
    la, lb = lam_re * dt, lam_im * dt
    lb_re, lb_im = jnp.exp(la) * jnp.cos(lb), jnp.exp(la) * jnp.sin(lb)
    den = lam_re * lam_re + lam_im * lam_im
    n_re, n_im = lb_re - 1.0, lb_im
    cf_re = (n_re * lam_re + n_im * lam_im) / den
    cf_im = (n_im * lam_re - n_re * lam_im) / den
    bb_re = cf_re[..., None] * b_re[None] - cf_im[..., None] * b_im[None]
    bb_im = cf_re[..., None] * b_im[None] + cf_im[..., None] * b_re[None]

    steps = _s5_steps().astype(np.float32)

    def power(d, expo):
        e = jnp.asarray(expo)[:, None, :]
        mag = jnp.exp(la[d][..., None] * e)
        return mag * jnp.cos(lb[d][..., None] * e), mag * jnp.sin(lb[d][..., None] * e)

    cat = jnp.concatenate

    def rows_op(pw, xs1, xs2):
        n = len(xs1)
        pr = cat([jnp.swapaxes(p[0], 1, 2) for p in pw], axis=-1)[:, :, None, :]
        pi = cat([jnp.swapaxes(p[1], 1, 2) for p in pw], axis=-1)[:, :, None, :]
        x1 = cat([jnp.swapaxes(x, 1, 2) for x in xs1], axis=-1)[:, None]
        x2 = cat([jnp.swapaxes(x, 1, 2) for x in xs2], axis=-1)[:, None]
        return (pr * x1 + pi * x2).reshape(N_GROUPS, S5_COLS, n * SSM_STATE)

    def cols_op(pw, xs1, xs2):
        n = len(xs1)
        pr = cat([p[0] for p in pw], axis=1)[..., None]
        pi = cat([p[1] for p in pw], axis=1)[..., None]
        x1 = cat(xs1, axis=1)[:, :, None, :]
        x2 = cat(xs2, axis=1)[:, :, None, :]
        return (pr * x1 + pi * x2).reshape(N_GROUPS, n * SSM_STATE, S5_COLS)

    bf_re, bf_im, bb_re_, bb_im_ = bb_re[0], bb_im[0], bb_re[1], bb_im[1]
    cf_re_, cf_im_, cb_re, cb_im = c_re[0], c_im[0], c_re[1], c_im[1]
    tr = lambda x: jnp.swapaxes(x, 1, 2)
    p_sb, p_sf = power(1, steps), power(0, t - 1 - steps)
    p_of, p_ob = power(0, steps + 1), power(1, t - steps)

    lags = np.broadcast_to(np.arange(t, dtype=np.float32), (N_GROUPS, t))
    p_kf, p_kb = power(0, lags), power(1, t - 1 - lags)
    cl_f = cols_op([p_kf, p_kf], [tr(cf_re_), tr(cf_im_)], [-tr(cf_im_), tr(cf_re_)])
    cl_b = cols_op([p_kb, p_kb], [tr(cb_re), tr(cb_im)], [-tr(cb_im), tr(cb_re)])
    bt_f = cat([tr(bf_re), -tr(bf_im)], axis=-1)
    bt_b = cat([tr(bb_re_), -tr(bb_im_)], axis=-1)
    m_st = rows_op([p_sf, p_sb, p_sf, p_sb], [bf_re, bb_re_, bf_im, bb_im_],
                   [-bf_im, -bb_im_, bf_re, bb_re_]).astype(BF16)
    m_out = cols_op([p_of, p_ob, p_of, p_ob], [tr(cf_re_), tr(cb_re), -tr(cf_im_), -tr(cb_im)],
                    [-tr(cf_im_), -tr(cb_im), -tr(cf_re_), -tr(cb_re)]).astype(BF16)
    a_re = cat([jnp.exp(t * la[0]) * jnp.cos(t * lb[0]), jnp.exp(t * la[1]) * jnp.cos(t * lb[1])], axis=-1)[:, None, :]
    a_im = cat([jnp.exp(t * la[0]) * jnp.sin(t * lb[0]), jnp.exp(t * la[1]) * jnp.sin(t * lb[1])], axis=-1)[:, None, :]
    return bt_f, cl_f, bt_b, cl_b, m_st, m_out, a_re, a_im


def _s5_kernel(*refs, batches):
    n_in = len(batches)
    u_refs = refs[:n_in]
    btf_ref, clf_ref, btb_ref, clb_ref, mst_ref, mout_ref, are_ref, aim_ref = refs[n_in:n_in + 8]
    y_refs = refs[n_in + 8:2 * n_in + 8]
    st_ref, xin_ref, tnat_ref = refs[2 * n_in + 8:]
    n_chunks = u_refs[0].shape[1]

    k_f = _dot_f32(btf_ref[0], clf_ref[0])
    k_b = _dot_f32(btb_ref[0], clb_ref[0])
    lane = lax.broadcasted_iota(jnp.int32, (SSM_GROUP, S5_COLS), 1)
    for s in range(S5_CHUNK):
        lo, hi = s * SSM_GROUP, (s + 1) * SSM_GROUP
        fwd = jnp.where(lane >= lo, pltpu.roll(k_f, lo, axis=1) if lo else k_f, 0.0)
        bwd = jnp.where(lane < hi, pltpu.roll(k_b, hi, axis=1) if hi < S5_COLS else k_b, 0.0)
        tnat_ref[lo:hi, :] = (fwd + bwd).astype(BF16)

    def natural_of(idx):
        k = pl.program_id(0) % GROUPS_PER_COL
        d = lax.shift_right_logical(idx, LANES.bit_length() - 1)
        pos = lax.shift_right_logical(idx, SSM_GROUP.bit_length() - 1) & (GROUPS_PER_COL - 1)
        step = d * GROUPS_PER_COL + ((pos - k) & (GROUPS_PER_COL - 1))
        return step * SSM_GROUP + (idx & (SSM_GROUP - 1))

    r_idx = lax.broadcasted_iota(jnp.int32, (S5_COLS, S5_COLS), 0)
    c_idx = lax.broadcasted_iota(jnp.int32, (S5_COLS, S5_COLS), 1)
    pick_rows = jnp.where(natural_of(r_idx) == c_idx, 1.0, 0.0).astype(BF16)
    pick_cols = jnp.where(r_idx == natural_of(c_idx), 1.0, 0.0).astype(BF16)
    toep = _dot(_dot(pick_rows, tnat_ref[...]).astype(BF16), pick_cols).astype(BF16)
    seqs = [(u_ref, y_ref, b) for u_ref, y_ref, nb in zip(u_refs, y_refs, batches) for b in range(nb)]

    def u_rows(u_ref, b):
        return u_ref[0, :, b * S5_COLS:(b + 1) * S5_COLS].astype(BF16)

    for slot in range(S5_ROWS):
        rows = pl.ds(slot, n_chunks, stride=S5_ROWS)
        if slot < len(seqs):
            st = _dot(u_rows(seqs[slot][0], seqs[slot][2]), mst_ref[0])
        else:
            st = jnp.zeros((n_chunks, S5_STATE_COLS), F32)
        st_ref[0, rows, :] = st[:, 0:LANES]
        st_ref[1, rows, :] = st[:, LANES:2 * LANES]

    a_re = jnp.broadcast_to(are_ref[0], (S5_ROWS, LANES))
    a_im = jnp.broadcast_to(aim_ref[0], (S5_ROWS, LANES))
    is_fwd = lax.broadcasted_iota(jnp.int32, (S5_ROWS, LANES), 1) < SSM_STATE
    half = SSM_STATE

    def body(i, carry):
        x_re, x_im = carry
        rf = pl.multiple_of(i * S5_ROWS, S5_ROWS)
        rb = pl.multiple_of((n_chunks - 1 - i) * S5_ROWS, S5_ROWS)
        xin_ref[0, pl.ds(rf, S5_ROWS), 0:half] = x_re[:, 0:half]
        xin_ref[0, pl.ds(rb, S5_ROWS), half:LANES] = x_re[:, half:LANES]
        xin_ref[1, pl.ds(rf, S5_ROWS), 0:half] = x_im[:, 0:half]
        xin_ref[1, pl.ds(rb, S5_ROWS), half:LANES] = x_im[:, half:LANES]
        s_re = jnp.where(is_fwd, st_ref[0, pl.ds(rf, S5_ROWS), :], st_ref[0, pl.ds(rb, S5_ROWS), :])
        s_im = jnp.where(is_fwd, st_ref[1, pl.ds(rf, S5_ROWS), :], st_ref[1, pl.ds(rb, S5_ROWS), :])
        n_re = a_re * x_re - a_im * x_im + s_re
        n_im = a_re * x_im + a_im * x_re + s_im
        return n_re, n_im

    zero = jnp.zeros((S5_ROWS, LANES), F32)
    lax.fori_loop(0, n_chunks, body, (zero, zero), unroll=4)

    for slot, (u_ref, y_ref, b) in enumerate(seqs):
        rows = pl.ds(slot, n_chunks, stride=S5_ROWS)
        x_in = jnp.concatenate([xin_ref[0, rows, :], xin_ref[1, rows, :]], axis=1).astype(BF16)
        y_ref[0, :, b * S5_COLS:(b + 1) * S5_COLS] = _dot(u_rows(u_ref, b), toep) + _dot(x_in, mout_ref[0])


def s5_scan(u_rows, bt_f, cl_f, bt_b, cl_b, m_st, m_out, a_re, a_im):
    g, n_chunks, _ = u_rows[0].shape
    batches = tuple(u.shape[2] // S5_COLS for u in u_rows)
    assert sum(batches) <= S5_ROWS
    gmap = lambda gi: (gi, 0, 0)
    io_specs = [pl.BlockSpec((1, n_chunks, u.shape[2]), gmap) for u in u_rows]
    return pl.pallas_call(
        functools.partial(_s5_kernel, batches=batches),
        out_shape=tuple(jax.ShapeDtypeStruct(u.shape, F32) for u in u_rows),
        grid=(g,),
        in_specs=io_specs + [
            pl.BlockSpec((1, SSM_GROUP, 2 * SSM_STATE), gmap),
            pl.BlockSpec((1, 2 * SSM_STATE, S5_COLS), gmap),
            pl.BlockSpec((1, SSM_GROUP, 2 * SSM_STATE), gmap),
            pl.BlockSpec((1, 2 * SSM_STATE, S5_COLS), gmap),
            pl.BlockSpec((1, S5_COLS, S5_STATE_COLS), gmap),
            pl.BlockSpec((1, S5_STATE_COLS, S5_COLS), gmap),
            pl.BlockSpec((1, 1, LANES), gmap),
            pl.BlockSpec((1, 1, LANES), gmap),
        ],
        out_specs=tuple(io_specs),
        scratch_shapes=[pltpu.VMEM((2, n_chunks * S5_ROWS, LANES), F32),
                        pltpu.VMEM((2, n_chunks * S5_ROWS, LANES), F32),
                        pltpu.VMEM((S5_COLS, S5_COLS), BF16)],
        compiler_params=_params(("parallel",)),
        name="s5_scan",
    )(*u_rows, bt_f, cl_f, bt_b, cl_b, m_st, m_out, a_re, a_im)


def _gelu_tanh(x):
    return 0.5 * x * (1.0 + jnp.tanh(math.sqrt(2.0 / math.pi) * (x + 0.044715 * (x * x * x))))


def _post_mix_kernel(att_ref, ys_ref, u_ref, x_ref, mod_ref, dsk_ref, wglu_ref, bglu_ref,
                     ag_ref, sg_ref, wo_ref, g2_ref, x1_ref, h2_ref, y_cols):
    gate1 = mod_ref[0, 2:3, :]
    shift2 = mod_ref[0, 3:4, :]
    scale2 = mod_ref[0, 4:5, :]
    _from_chunk_rows(ys_ref, y_cols)
    y = jnp.concatenate([y_cols[j] for j in range(SSM_WIDTH // LANES)], axis=1)
    y = y + dsk_ref[...] * u_ref[0]
    y = _gelu_tanh(y)
    z = _dot(y.astype(BF16), wglu_ref[...]) + bglu_ref[...]
    ssm = y * jax.nn.sigmoid(z)
    att = att_ref[0].astype(F32)
    att_n = (att * _rms(att, ATT_WIDTH) * ag_ref[...]).astype(BF16)
    ssm_n = (ssm * _rms(ssm, SSM_WIDTH) * sg_ref[...]).astype(BF16)
    mix = _dot(att_n, wo_ref[0:ATT_WIDTH, :]) + _dot(ssm_n, wo_ref[ATT_WIDTH:2 * ATT_WIDTH, :])
    x1 = x_ref[0] + gate1 * mix
    x1_ref[0] = x1
    h2 = (x1 * _rms(x1, D_MODEL) * g2_ref[...]) * (1.0 + scale2) + shift2
    h2_ref[0] = h2.astype(BF16)


def post_mix(att, ys, u, x, mod, d_skip, w_glu, b_glu, att_g, ssm_g, w_o, g2):
    b, length, d = x.shape
    tm = S5_TILE
    const = lambda bi, i: (0, 0)
    tile = lambda w: pl.BlockSpec((1, tm, w), lambda bi, i: (bi, i, 0))
    return pl.pallas_call(
        _post_mix_kernel,
        out_shape=(jax.ShapeDtypeStruct((b, length, d), F32),
                   jax.ShapeDtypeStruct((b, length, d), BF16)),
        grid=(b, length // tm),
        in_specs=[
            tile(ATT_WIDTH),
            pl.BlockSpec((N_GROUPS, S5_SUB, S5_COLS), lambda bi, i: (0, i, bi)),
            tile(SSM_WIDTH), tile(d),
            pl.BlockSpec((1, 6, d), lambda bi, i: (bi, 0, 0)),
            pl.BlockSpec((1, SSM_WIDTH), const),
            pl.BlockSpec((SSM_WIDTH, SSM_WIDTH), const),
            pl.BlockSpec((1, SSM_WIDTH), const),
            pl.BlockSpec((1, ATT_WIDTH), const),
            pl.BlockSpec((1, SSM_WIDTH), const),
            pl.BlockSpec((d, d), const),
            pl.BlockSpec((1, d), const),
        ],
        out_specs=(tile(d), tile(d)),
        scratch_shapes=[pltpu.VMEM((SSM_WIDTH // LANES, tm, LANES), F32)],
        compiler_params=_params(("parallel", "parallel")),
        name="post_mix",
    )(att, ys, u, x, mod, d_skip, w_glu, b_glu, att_g, ssm_g, w_o, g2)


def _ffn_up_kernel(h_ref, w1_ref, w3_ref, o_ref, w1_bf, w3_bf):
    @pl.when(jnp.logical_and(pl.program_id(1) == 0, pl.program_id(2) == 0))
    def _():
        w1_bf[...] = w1_ref[...].astype(BF16)
        w3_bf[...] = w3_ref[...].astype(BF16)

    h = h_ref[0]
    a = _dot(h, w1_bf[...])
    g = _dot(h, w3_bf[...])
    o_ref[0] = (a * jax.nn.sigmoid(a) * g).astype(BF16)


def ffn_up(h2, w1, w3):
    b, length, d = h2.shape
    tm, tf = min(FFN_UP_TM, length), FFN_UP_TF
    return pl.pallas_call(
        _ffn_up_kernel,
        out_shape=jax.ShapeDtypeStruct((b, length, D_FF), BF16),
        grid=(D_FF // tf, b, length // tm),
        in_specs=[
            pl.BlockSpec((1, tm, d), lambda f, bi, i: (bi, i, 0)),
            pl.BlockSpec((d, tf), lambda f, bi, i: (0, f)),
            pl.BlockSpec((d, tf), lambda f, bi, i: (0, f)),
        ],
        out_specs=pl.BlockSpec((1, tm, tf), lambda f, bi, i: (bi, i, f)),
        scratch_shapes=[pltpu.VMEM((d, tf), BF16), pltpu.VMEM((d, tf), BF16)],
        compiler_params=_params(("parallel", "arbitrary", "arbitrary")),
        name="ffn_up",
    )(h2, w1, w3)


def _ffn_down_kernel(a_ref, x1_ref, mod_ref, w2_ref, o_ref):
    o_ref[0] = x1_ref[0] + mod_ref[0, 5:6, :] * _dot(a_ref[0], w2_ref[...])


def ffn_down(act, x1, mod, w2):
    b, length, d = x1.shape
    tm, tn = min(FFN_DOWN_TM, length), FFN_DOWN_TN
    return pl.pallas_call(
        _ffn_down_kernel,
        out_shape=jax.ShapeDtypeStruct((b, length, d), F32),
        grid=(d // tn, b, length // tm),
        in_specs=[
            pl.BlockSpec((1, tm, D_FF), lambda n, bi, i: (bi, i, 0)),
            pl.BlockSpec((1, tm, tn), lambda n, bi, i: (bi, i, n)),
            pl.BlockSpec((1, 6, tn), lambda n, bi, i: (bi, 0, n)),
            pl.BlockSpec((D_FF, tn), lambda n, bi, i: (0, n)),
        ],
        out_specs=pl.BlockSpec((1, tm, tn), lambda n, bi, i: (bi, i, n)),
        compiler_params=_params(("parallel", "parallel", "parallel")),
        name="ffn_down",
    )(act, x1, mod, w2)


def _cast_kernel(x_ref, o_ref):
    o_ref[...] = x_ref[...].astype(o_ref.dtype)


def cast_bf16(w):
    rows, cols = w.shape
    tr = rows
    while tr * cols * 4 > CAST_BLOCK_BYTES and tr % (4 * SUBLANES) == 0:
        tr //= 2
    return pl.pallas_call(
        _cast_kernel,
        out_shape=jax.ShapeDtypeStruct((rows, cols), BF16),
        grid=(rows // tr,),
        in_specs=[pl.BlockSpec((tr, cols), lambda i: (i, 0))],
        out_specs=pl.BlockSpec((tr, cols), lambda i: (i, 0)),
        compiler_params=_params(("parallel",)),
        name="cast_bf16",
    )(w)


def _swap_halves(a, axis=-1):
    lo, hi = jnp.split(a, 2, axis=axis)
    return jnp.concatenate([hi, lo], axis=axis)


def _prep_w_in_kernel(w_ref, o_ref):
    w = w_ref[...]
    q_cols = N_HEADS * QK_DIM
    half = ROPE_DIM // 2

    def with_swapped(r):
        return [r, r[:, half:], r[:, :half]]

    cols = [w[:, hd * QK_DIM:hd * QK_DIM + NOPE_DIM] for hd in range(N_HEADS)]
    for hd in range(N_HEADS):
        cols += with_swapped(w[:, hd * QK_DIM + NOPE_DIM:(hd + 1) * QK_DIM])
    cols.append(w[:, q_cols:q_cols + KV_RANK])
    cols += with_swapped(w[:, q_cols + KV_RANK:q_cols + KV_RANK + ROPE_DIM])
    cols.append(w[:, q_cols + KV_RANK + ROPE_DIM:])
    o_ref[...] = jnp.concatenate(cols, axis=1).astype(BF16)


def prep_w_in(w_in):
    d, n = w_in.shape
    return pl.pallas_call(
        _prep_w_in_kernel,
        out_shape=jax.ShapeDtypeStruct((d, C_END), BF16),
        grid=(d // W_PREP_ROWS,),
        in_specs=[pl.BlockSpec((W_PREP_ROWS, n), lambda i: (i, 0))],
        out_specs=pl.BlockSpec((W_PREP_ROWS, C_END), lambda i: (i, 0)),
        compiler_params=_params(("parallel",)),
        name="prep_w_in",
    )(w_in)


def prep_w_ukv(w_ukv):
    rank, n = w_ukv.shape
    src = lambda j: (0, 2 * (j % N_HEADS) + j // N_HEADS)
    return pl.pallas_call(
        _cast_kernel,
        out_shape=jax.ShapeDtypeStruct((rank, n), BF16),
        grid=(n // LANES,),
        in_specs=[pl.BlockSpec((rank, LANES), src)],
        out_specs=pl.BlockSpec((rank, LANES), lambda j: (0, j)),
        compiler_params=_params(("parallel",)),
        name="prep_w_ukv",
    )(w_ukv)


def _rope_gain(g):
    gr = g[NOPE_DIM:]
    return jnp.concatenate([gr, _swap_halves(gr)]).reshape(1, 2 * ROPE_DIM)


def _rope_tables(length):
    pos = jnp.arange(length, dtype=F32)
    inv_freq = ROPE_BASE ** (-jnp.arange(0, ROPE_DIM, 2, dtype=F32) / ROPE_DIM)
    ang = pos[:, None] * inv_freq[None, :]
    cos, sin = jnp.cos(ang), jnp.sin(ang)
    return jnp.tile(cos, (1, 4)), jnp.tile(jnp.concatenate([-sin, sin], axis=-1), (1, 2))


def kernel(x_prompt, x_sample, c_prompt, c_sample, w_ada, b_ada, norm_mix_g, w_in, kv_norm_g, w_ukv,
           q_norm_g, k_norm_g, lam_re, lam_im, log_dt, b_re, b_im, c_re, c_im, d_skip, w_glu, b_glu,
           att_out_g, ssm_out_g, w_o, norm_ffn_g, w1, w3, w2):
    assert w_ada.shape[0] == 1, "single-layer kernel"
    xs = (x_prompt, x_sample)
    batches = tuple(x.shape[0] for x in xs)
    length = x_prompt.shape[1]
    assert x_sample.shape[1] == length and sum(batches) <= S5_ROWS

    c_all = jnp.concatenate([c_prompt, c_sample], axis=0)
    c8 = jnp.pad(c_all, ((0, SUBLANES - c_all.shape[0]), (0, 0)))
    mod_all = ada_mod(c8, w_ada[0], b_ada[0]).reshape(SUBLANES, 6, D_MODEL)
    mods = (mod_all[:batches[0]], mod_all[batches[0]:batches[0] + batches[1]])

    w_all = prep_w_in(w_in[0])
    w_ukv_p = prep_w_ukv(w_ukv[0])
    row = lambda a: a.reshape(1, -1)
    q_gn, k_gn = row(q_norm_g[0][:NOPE_DIM]), row(k_norm_g[0][:NOPE_DIM])
    q_gr, k_gr = _rope_gain(q_norm_g[0]), _rope_gain(k_norm_g[0])
    cos_t, sin_t = _rope_tables(length)
    s5_ops = s5_matrices(lam_re[0], lam_im[0], log_dt[0], b_re[0], b_im[0], c_re[0], c_im[0])
    w_glu_b, w_o_b = cast_bf16(w_glu[0]), cast_bf16(w_o[0])
    w2_b = cast_bf16(w2[0])

    proj = [in_proj(x, m, row(norm_mix_g[0]), w_all, w_ukv_p, row(kv_norm_g[0]), q_gn, k_gn, q_gr, k_gr,
                    cos_t, sin_t) for x, m in zip(xs, mods)]
    atts = [attention(q, k, v) for q, k, v, _, _ in proj]
    y_rows = s5_scan([p[4] for p in proj], *s5_ops)

    outs = []
    for x, m, att, ys, p in zip(xs, mods, atts, y_rows, proj):
        x1, h2 = post_mix(att, ys, p[3], x, m, row(d_skip[0]), w_glu_b, row(b_glu[0]), row(att_out_g[0]),
                          row(ssm_out_g[0]), w_o_b, row(norm_ffn_g[0]))
        outs.append(ffn_down(ffn_up(h2, w1[0], w3[0]), x1, m, w2_b))
    return tuple(outs)
```

```python
import functools
import math

import jax
import jax.numpy as jnp
import numpy as np
from jax import lax
from jax.experimental import pallas as pl
from jax.experimental.pallas import tpu as pltpu

F32 = jnp.float32
BF16 = jnp.bfloat16

D_MODEL = 2048
ATT_WIDTH = 1024
SSM_WIDTH = 1024
N_HEADS = 8
V_DIM = 128
NOPE_DIM = 128
ROPE_DIM = 64
QK_DIM = NOPE_DIM + ROPE_DIM
KV_RANK = 512
ROPE_BASE = 10000.0
SSM_GROUP = 16
N_GROUPS = SSM_WIDTH // SSM_GROUP
SSM_STATE = 64
D_FF = 5632
NORM_EPS = 1e-6

LANES = 128
SUBLANES = 8
VMEM_LIMIT = 56 * 1024 * 1024
CAST_BLOCK_BYTES = 4 * 1024 * 1024

ATT_TQ, ATT_TK = 1024, 256
FFN_UP_TM, FFN_UP_TF = 1024, 512
FFN_DOWN_TM, FFN_DOWN_TN = 512, 1024
W_PREP_ROWS = 256

C_QN = 0
C_QR = C_QN + N_HEADS * NOPE_DIM
C_KV = C_QR + N_HEADS * 2 * ROPE_DIM
C_KR = C_KV + KV_RANK
C_U = C_KR + 2 * ROPE_DIM
C_END = C_U + SSM_WIDTH

S5_CHUNK = 32
S5_ROWS = SUBLANES
S5_SUB = SUBLANES
S5_TILE = S5_SUB * S5_CHUNK
S5_COLS = S5_CHUNK * SSM_GROUP
S5_STATE_COLS = 4 * SSM_STATE
GROUPS_PER_COL = LANES // SSM_GROUP


def _params(sem):
    return pltpu.CompilerParams(dimension_semantics=sem, vmem_limit_bytes=VMEM_LIMIT)


def _dot(a, b):
    return jnp.dot(a, b, preferred_element_type=F32)


def _dot_f32(a, b):
    a_hi = a.astype(BF16)
    a_lo = (a - a_hi.astype(F32)).astype(BF16)
    b_hi = b.astype(BF16)
    b_lo = (b - b_hi.astype(F32)).astype(BF16)
    return _dot(a_hi, b_hi) + _dot(a_lo, b_hi) + _dot(a_hi, b_lo)


def _ada_kernel(c_ref, w_ref, b_ref, o_ref):
    c = c_ref[...]
    o_ref[...] = _dot_f32(c * jax.nn.sigmoid(c), w_ref[...]) + b_ref[...]


def ada_mod(c8, w_ada, b_ada):
    rows, d = c8.shape
    n = w_ada.shape[1]
    tn = 1024
    return pl.pallas_call(
        _ada_kernel,
        out_shape=jax.ShapeDtypeStruct((rows, n), F32),
        grid=(n // tn,),
        in_specs=[
            pl.BlockSpec((rows, d), lambda j: (0, 0)),
            pl.BlockSpec((d, tn), lambda j: (0, j)),
            pl.BlockSpec((1, tn), lambda j: (0, j)),
        ],
        out_specs=pl.BlockSpec((rows, tn), lambda j: (0, j)),
        compiler_params=_params(("arbitrary",)),
        name="ada_mod",
    )(c8, w_ada, b_ada.reshape(1, n))


def _rms(x, width):
    return lax.rsqrt(jnp.sum(x * x, axis=-1, keepdims=True) * (1.0 / width) + NORM_EPS)


def _granule_masks():
    pos = lax.broadcasted_iota(jnp.int32, (SUBLANES, LANES), 1) // SSM_GROUP
    return [pos == p for p in range(GROUPS_PER_COL)]


def _to_chunk_rows(u_cols, dst):
    masks = _granule_masks()
    for j in range(SSM_WIDTH // LANES):
        rolled = []
        for t in range(S5_CHUNK):
            v = u_cols[j, pl.ds(t, S5_SUB, stride=S5_CHUNK), :]
            r = t % GROUPS_PER_COL
            rolled.append(pltpu.roll(v, SSM_GROUP * r, axis=1) if r else v)
        for k in range(GROUPS_PER_COL):
            for d in range(S5_COLS // LANES):
                acc = rolled[GROUPS_PER_COL * d + GROUPS_PER_COL - 1]
                for rho in range(GROUPS_PER_COL - 2, -1, -1):
                    acc = jnp.where(masks[(k + rho) % GROUPS_PER_COL], rolled[GROUPS_PER_COL * d + rho], acc)
                dst[GROUPS_PER_COL * j + k, :, d * LANES:(d + 1) * LANES] = acc


def _from_chunk_rows(src, y_cols):
    masks = _granule_masks()
    for j in range(SSM_WIDTH // LANES):
        for d in range(S5_COLS // LANES):
            cols = [src[GROUPS_PER_COL * j + k, :, d * LANES:(d + 1) * LANES] for k in range(GROUPS_PER_COL)]
            for rho in range(GROUPS_PER_COL):
                acc = cols[GROUPS_PER_COL - 1]
                for k in range(GROUPS_PER_COL - 2, -1, -1):
                    acc = jnp.where(masks[(k + rho) % GROUPS_PER_COL], cols[k], acc)
                v = pltpu.roll(acc, LANES - SSM_GROUP * rho, axis=1) if rho else acc
                y_cols[j, pl.ds(GROUPS_PER_COL * d + rho, S5_SUB, stride=S5_CHUNK), :] = v


def _in_proj_kernel(x_ref, mod_ref, g1_ref, w_ref, wukv_ref, kvg_ref, qg_ref, kg_ref,
                    qgr_ref, kgr_ref, cos_ref, sin_ref, q_ref, k_ref, v_ref, u_ref, ur_ref, u_cols,
                    *, q_scale):
    x = x_ref[0]
    shift = mod_ref[0, 0:1, :]
    scale = mod_ref[0, 1:2, :]
    h = (x * _rms(x, D_MODEL) * g1_ref[...]) * (1.0 + scale) + shift
    h = h.astype(BF16)

    cos = cos_ref[...]
    sin = sin_ref[...]

    def rope(col, gain):
        cg = col * gain
        return cg * cos + pltpu.roll(cg, ROPE_DIM, axis=1) * sin

    qn = _dot(h, w_ref[:, C_QN:C_QR])
    qr = _dot(h, w_ref[:, C_QR:C_KV])
    qg_n = qg_ref[...]
    qg_r = qgr_ref[...]
    for hd in range(N_HEADS):
        qn_h = qn[:, hd * LANES:(hd + 1) * LANES]
        qr_h = qr[:, hd * LANES:(hd + 1) * LANES]
        ss = (jnp.sum(qn_h * qn_h, axis=-1, keepdims=True)
              + 0.5 * jnp.sum(qr_h * qr_h, axis=-1, keepdims=True))
        r = lax.rsqrt(ss * (1.0 / QK_DIM) + NORM_EPS) * q_scale
        q_ref[0, hd, :, 0:NOPE_DIM] = (qn_h * r * qg_n).astype(BF16)
        q_ref[0, hd, :, NOPE_DIM:QK_DIM] = (rope(qr_h, qg_r) * r)[:, 0:ROPE_DIM].astype(BF16)

    ckv = _dot(h, w_ref[:, C_KV:C_KR])
    ckv_n = (ckv * _rms(ckv, KV_RANK) * kvg_ref[...]).astype(BF16)
    kv = _dot(ckv_n, wukv_ref[...])
    kr = _dot(h, w_ref[:, C_KR:C_U])
    kr_ss = 0.5 * jnp.sum(kr * kr, axis=-1, keepdims=True)
    kr_rot = rope(kr, kgr_ref[...])
    kg_n = kg_ref[...]
    ones = jnp.ones((x.shape[0], LANES), BF16)
    for hd in range(N_HEADS):
        kn_h = kv[:, hd * LANES:(hd + 1) * LANES]
        ss = jnp.sum(kn_h * kn_h, axis=-1, keepdims=True) + kr_ss
        r = lax.rsqrt(ss * (1.0 / QK_DIM) + NORM_EPS)
        k_ref[0, hd, :, 0:NOPE_DIM] = (kn_h * r * kg_n).astype(BF16)
        k_ref[0, hd, :, NOPE_DIM:QK_DIM] = (kr_rot * r)[:, 0:ROPE_DIM].astype(BF16)
        v_h = kv[:, ATT_WIDTH + hd * LANES:ATT_WIDTH + (hd + 1) * LANES]
        v_ref[0, hd, :, 0:V_DIM] = v_h.astype(BF16)
        v_ref[0, hd, :, V_DIM:2 * V_DIM] = ones

    u = _dot(h, w_ref[:, C_U:C_END])
    u_ref[0] = u
    for j in range(SSM_WIDTH // LANES):
        u_cols[j] = u[:, j * LANES:(j + 1) * LANES]
    _to_chunk_rows(u_cols, ur_ref)


def in_proj(x, mod, g1, w_all, w_ukv, kv_g, q_gn, k_gn, q_gr, k_gr, cos_t, sin_t):
    b, length, d = x.shape
    tm = S5_TILE
    assert length % tm == 0
    const = lambda bi, i: (0, 0)
    kern = functools.partial(_in_proj_kernel, q_scale=math.log2(math.e) / math.sqrt(QK_DIM))
    return pl.pallas_call(
        kern,
        out_shape=(
            jax.ShapeDtypeStruct((b, N_HEADS, length, QK_DIM), BF16),
            jax.ShapeDtypeStruct((b, N_HEADS, length, QK_DIM), BF16),
            jax.ShapeDtypeStruct((b, N_HEADS, length, 2 * V_DIM), BF16),
            jax.ShapeDtypeStruct((b, length, SSM_WIDTH), F32),
            jax.ShapeDtypeStruct((N_GROUPS, length // S5_CHUNK, b * S5_COLS), F32),
        ),
        grid=(b, length // tm),
        in_specs=[
            pl.BlockSpec((1, tm, d), lambda bi, i: (bi, i, 0)),
            pl.BlockSpec((1, 6, d), lambda bi, i: (bi, 0, 0)),
            pl.BlockSpec((1, d), const),
            pl.BlockSpec((d, C_END), const),
            pl.BlockSpec((KV_RANK, 2 * ATT_WIDTH), const),
            pl.BlockSpec((1, KV_RANK), const),
            pl.BlockSpec((1, LANES), const),
            pl.BlockSpec((1, LANES), const),
            pl.BlockSpec((1, LANES), const),
            pl.BlockSpec((1, LANES), const),
            pl.BlockSpec((tm, LANES), lambda bi, i: (i, 0)),
            pl.BlockSpec((tm, LANES), lambda bi, i: (i, 0)),
        ],
        out_specs=(
            pl.BlockSpec((1, N_HEADS, tm, QK_DIM), lambda bi, i: (bi, 0, i, 0)),
            pl.BlockSpec((1, N_HEADS, tm, QK_DIM), lambda bi, i: (bi, 0, i, 0)),
            pl.BlockSpec((1, N_HEADS, tm, 2 * V_DIM), lambda bi, i: (bi, 0, i, 0)),
            pl.BlockSpec((1, tm, SSM_WIDTH), lambda bi, i: (bi, i, 0)),
            pl.BlockSpec((N_GROUPS, S5_SUB, S5_COLS), lambda bi, i: (0, i, bi)),
        ),
        scratch_shapes=[pltpu.VMEM((SSM_WIDTH // LANES, tm, LANES), F32)],
        compiler_params=_params(("parallel", "parallel")),
        name="in_proj",
    )(x, mod, g1, w_all, w_ukv, kv_g, q_gn, k_gn, q_gr, k_gr, cos_t, sin_t)


def _attn_kernel(q_ref, k_ref, v_ref, o_ref, *, tk):
    q = q_ref[0, 0]
    tq = q.shape[0]
    m = jnp.full((tq, 1), -jnp.inf, F32)
    acc = jnp.zeros((tq, 2 * V_DIM), F32)
    for j in range(k_ref.shape[2] // tk):
        kc = k_ref[0, 0, j * tk:(j + 1) * tk, :]
        vc = v_ref[0, 0, j * tk:(j + 1) * tk, :]
        s = lax.dot_general(q, kc, (((1,), (1,)), ((), ())), preferred_element_type=F32)
        m_new = jnp.maximum(m, jnp.max(s, axis=-1, keepdims=True))
        alpha = jnp.exp2(m - m_new)
        p = jnp.exp2(s - m_new).astype(BF16)
        acc = alpha * acc + _dot(p, vc)
        m = m_new
    o_ref[0] = (acc[:, 0:V_DIM] / acc[:, V_DIM:2 * V_DIM]).astype(o_ref.dtype)


def attention(q, k, v):
    b, h, length, _ = q.shape
    tq = min(ATT_TQ, length)
    tk = min(ATT_TK, length)
    return pl.pallas_call(
        functools.partial(_attn_kernel, tk=tk),
        out_shape=jax.ShapeDtypeStruct((b, length, h * V_DIM), BF16),
        grid=(b, h, length // tq),
        in_specs=[
            pl.BlockSpec((1, 1, tq, QK_DIM), lambda bi, hi, i: (bi, hi, i, 0)),
            pl.BlockSpec((1, 1, length, QK_DIM), lambda bi, hi, i: (bi, hi, 0, 0)),
            pl.BlockSpec((1, 1, length, 2 * V_DIM), lambda bi, hi, i: (bi, hi, 0, 0)),
        ],
        out_specs=pl.BlockSpec((1, tq, V_DIM), lambda bi, hi, i: (bi, i, hi)),
        compiler_params=_params(("parallel", "parallel", "parallel")),
        name="attention",
    )(q, k, v)


def _s5_steps():
    g = np.arange(N_GROUPS)[:, None] % GROUPS_PER_COL
    slot = np.arange(S5_CHUNK)[None, :]
    return GROUPS_PER_COL * (slot // GROUPS_PER_COL) + (slot % GROUPS_PER_COL - g) % GROUPS_PER_COL


def s5_matrices(lam_re, lam_im, log_dt, b_re, b_im, c_re, c_im):
    t = S5_CHUNK
    dt = jnp.exp(log_dt)[..., None]
    la, lb = lam_re * dt, lam_im * dt
    lb_re, lb_im = jnp.exp(la) * jnp.cos(lb), jnp.exp(la) * jnp.sin(lb)
    den = lam_re * lam_re + lam_im * lam_im
    n_re, n_im = lb_re - 1.0, lb_im
    cf_re = (n_re * lam_re + n_im * lam_im) / den
    cf_im = (n_im * lam_re - n_re * lam_im) / den
    bb_re = cf_re[..., None] * b_re[None] - cf_im[..., None] * b_im[None]
    bb_im = cf_re[..., None] * b_im[None] + cf_im[..., None] * b_re[None]

    steps = _s5_steps().astype(np.float32)

    def power(d, expo):
        e = jnp.asarray(expo)[:, None, :]
        mag = jnp.exp(la[d][..., None] * e)
        return mag * jnp.cos(lb[d][..., None] * e), mag * jnp.sin(lb[d][..., None] * e)

    cat = jnp.concatenate

    def rows_op(pw, xs1, xs2):
        n = len(xs1)
        pr = cat([jnp.swapaxes(p[0], 1, 2) for p in pw], axis=-1)[:, :, None, :]
        pi = cat([jnp.swapaxes(p[1], 1, 2) for p in pw], axis=-1)[:, :, None, :]
        x1 = cat([jnp.swapaxes(x, 1, 2) for x in xs1], axis=-1)[:, None]
        x2 = cat([jnp.swapaxes(x, 1, 2) for x in xs2], axis=-1)[:, None]
        return (pr * x1 + pi * x2).reshape(N_GROUPS, S5_COLS, n * SSM_STATE)

    def cols_op(pw, xs1, xs2):
        pr = jnp.repeat(cat([p[0] for p in pw], axis=1), SSM_GROUP, axis=2)
        pi = jnp.repeat(cat([p[1] for p in pw], axis=1), SSM_GROUP, axis=2)
        x1 = jnp.tile(cat(xs1, axis=1), (1, 1, S5_CHUNK))
        x2 = jnp.tile(cat(xs2, axis=1), (1, 1, S5_CHUNK))
        return pr * x1 + pi * x2

    bf_re, bf_im, bb_re_, bb_im_ = bb_re[0], bb_im[0], bb_re[1], bb_im[1]
    cf_re_, cf_im_, cb_re, cb_im = c_re[0], c_im[0], c_re[1], c_im[1]
    tr = lambda x: jnp.swapaxes(x, 1, 2)
    p_sb, p_sf = power(1, steps), power(0, t - 1 - steps)
    p_of, p_ob = power(0, steps + 1), power(1, t - steps)

    lags = np.broadcast_to(np.arange(t, dtype=np.float32), (N_GROUPS, t))
    p_kf, p_kb = power(0, lags), power(1, t - 1 - lags)
    cl_f = cols_op([p_kf, p_kf], [tr(cf_re_), tr(cf_im_)], [-tr(cf_im_), tr(cf_re_)])
    cl_b = cols_op([p_kb, p_kb], [tr(cb_re), tr(cb_im)], [-tr(cb_im), tr(cb_re)])
    bt_f = cat([tr(bf_re), -tr(bf_im)], axis=-1)
    bt_b = cat([tr(bb_re_), -tr(bb_im_)], axis=-1)
    m_st = rows_op([p_sf, p_sb, p_sf, p_sb], [bf_re, bb_re_, bf_im, bb_im_],
                   [-bf_im, -bb_im_, bf_re, bb_re_]).astype(BF16)
    m_out = cols_op([p_of, p_ob, p_of, p_ob], [tr(cf_re_), tr(cb_re), -tr(cf_im_), -tr(cb_im)],
                    [-tr(cf_im_), -tr(cb_im), -tr(cf_re_), -tr(cb_re)]).astype(BF16)
    a_re = cat([jnp.exp(t * la[0]) * jnp.cos(t * lb[0]), jnp.exp(t * la[1]) * jnp.cos(t * lb[1])], axis=-1)[:, None, :]
    a_im = cat([jnp.exp(t * la[0]) * jnp.sin(t * lb[0]), jnp.exp(t * la[1]) * jnp.sin(t * lb[1])], axis=-1)[:, None, :]
    return bt_f, cl_f, bt_b, cl_b, m_st, m_out, a_re, a_im


def _s5_kernel(*refs, batches):
    n_in = len(batches)
    u_refs = refs[:n_in]
    btf_ref, clf_ref, btb_ref, clb_ref, mst_ref, mout_ref, are_ref, aim_ref = refs[n_in:n_in + 8]
    y_refs = refs[n_in + 8:2 * n_in + 8]
    st_ref, xin_ref, tnat_ref = refs[2 * n_in + 8:]
    n_chunks = u_refs[0].shape[1]

    k_f = _dot_f32(btf_ref[0], clf_ref[0])
    k_b = _dot_f32(btb_ref[0], clb_ref[0])
    lane = lax.broadcasted_iota(jnp.int32, (SSM_GROUP, S5_COLS), 1)
    for s in range(S5_CHUNK):
        lo, hi = s * SSM_GROUP, (s + 1) * SSM_GROUP
        fwd = jnp.where(lane >= lo, pltpu.roll(k_f, lo, axis=1) if lo else k_f, 0.0)
        bwd = jnp.where(lane < hi, pltpu.roll(k_b, hi, axis=1) if hi < S5_COLS else k_b, 0.0)
        tnat_ref[lo:hi, :] = (fwd + bwd).astype(BF16)

    def natural_of(idx):
        k = pl.program_id(0) % GROUPS_PER_COL
        d = lax.shift_right_logical(idx, LANES.bit_length() - 1)
        pos = lax.shift_right_logical(idx, SSM_GROUP.bit_length() - 1) & (GROUPS_PER_COL - 1)
        step = d * GROUPS_PER_COL + ((pos - k) & (GROUPS_PER_COL - 1))
        return step * SSM_GROUP + (idx & (SSM_GROUP - 1))

    r_idx = lax.broadcasted_iota(jnp.int32, (S5_COLS, S5_COLS), 0)
    c_idx = lax.broadcasted_iota(jnp.int32, (S5_COLS, S5_COLS), 1)
    pick_rows = jnp.where(natural_of(r_idx) == c_idx, 1.0, 0.0).astype(BF16)
    pick_cols = jnp.where(r_idx == natural_of(c_idx), 1.0, 0.0).astype(BF16)
    toep = _dot(_dot(pick_rows, tnat_ref[...]).astype(BF16), pick_cols).astype(BF16)
    seqs = [(u_ref, y_ref, b) for u_ref, y_ref, nb in zip(u_refs, y_refs, batches) for b in range(nb)]

    def u_rows(u_ref, b):
        return u_ref[0, :, b * S5_COLS:(b + 1) * S5_COLS].astype(BF16)

    for slot in range(S5_ROWS):
        rows = pl.ds(slot, n_chunks, stride=S5_ROWS)
        if slot < len(seqs):
            st = _dot(u_rows(seqs[slot][0], seqs[slot][2]), mst_ref[0])
        else:
            st = jnp.zeros((n_chunks, S5_STATE_COLS), F32)
        st_ref[0, rows, :] = st[:, 0:LANES]
        st_ref[1, rows, :] = st[:, LANES:2 * LANES]

    a_re = jnp.broadcast_to(are_ref[0], (S5_ROWS, LANES))
    a_im = jnp.broadcast_to(aim_ref[0], (S5_ROWS, LANES))
    is_fwd = lax.broadcasted_iota(jnp.int32, (S5_ROWS, LANES), 1) < SSM_STATE
    half = SSM_STATE

    def body(i, carry):
        x_re, x_im = carry
        rf = pl.multiple_of(i * S5_ROWS, S5_ROWS)
        rb = pl.multiple_of((n_chunks - 1 - i) * S5_ROWS, S5_ROWS)
        xin_ref[0, pl.ds(rf, S5_ROWS), 0:half] = x_re[:, 0:half]
        xin_ref[0, pl.ds(rb, S5_ROWS), half:LANES] = x_re[:, half:LANES]
        xin_ref[1, pl.ds(rf, S5_ROWS), 0:half] = x_im[:, 0:half]
        xin_ref[1, pl.ds(rb, S5_ROWS), half:LANES] = x_im[:, half:LANES]
        s_re = jnp.where(is_fwd, st_ref[0, pl.ds(rf, S5_ROWS), :], st_ref[0, pl.ds(rb, S5_ROWS), :])
        s_im = jnp.where(is_fwd, st_ref[1, pl.ds(rf, S5_ROWS), :], st_ref[1, pl.ds(rb, S5_ROWS), :])
        n_re = a_re * x_re - a_im * x_im + s_re
        n_im = a_re * x_im + a_im * x_re + s_im
        return n_re, n_im

    zero = jnp.zeros((S5_ROWS, LANES), F32)
    lax.fori_loop(0, n_chunks, body, (zero, zero), unroll=4)

    for slot, (u_ref, y_ref, b) in enumerate(seqs):
        rows = pl.ds(slot, n_chunks, stride=S5_ROWS)
        x_in = jnp.concatenate([xin_ref[0, rows, :], xin_ref[1, rows, :]], axis=1).astype(BF16)
        y_ref[0, :, b * S5_COLS:(b + 1) * S5_COLS] = _dot(u_rows(u_ref, b), toep) + _dot(x_in, mout_ref[0])


def s5_scan(u_rows, bt_f, cl_f, bt_b, cl_b, m_st, m_out, a_re, a_im):
    g, n_chunks, _ = u_rows[0].shape
    batches = tuple(u.shape[2] // S5_COLS for u in u_rows)
    assert sum(batches) <= S5_ROWS
    gmap = lambda gi: (gi, 0, 0)
    io_specs = [pl.BlockSpec((1, n_chunks, u.shape[2]), gmap) for u in u_rows]
    return pl.pallas_call(
        functools.partial(_s5_kernel, batches=batches),
        out_shape=tuple(jax.ShapeDtypeStruct(u.shape, F32) for u in u_rows),
        grid=(g,),
        in_specs=io_specs + [
            pl.BlockSpec((1, SSM_GROUP, 2 * SSM_STATE), gmap),
            pl.BlockSpec((1, 2 * SSM_STATE, S5_COLS), gmap),
            pl.BlockSpec((1, SSM_GROUP, 2 * SSM_STATE), gmap),
            pl.BlockSpec((1, 2 * SSM_STATE, S5_COLS), gmap),
            pl.BlockSpec((1, S5_COLS, S5_STATE_COLS), gmap),
            pl.BlockSpec((1, S5_STATE_COLS, S5_COLS), gmap),
            pl.BlockSpec((1, 1, LANES), gmap),
            pl.BlockSpec((1, 1, LANES), gmap),
        ],
        out_specs=tuple(io_specs),
        scratch_shapes=[pltpu.VMEM((2, n_chunks * S5_ROWS, LANES), F32),
                        pltpu.VMEM((2, n_chunks * S5_ROWS, LANES), F32),
                        pltpu.VMEM((S5_COLS, S5_COLS), BF16)],
        compiler_params=_params(("parallel",)),
        name="s5_scan",
    )(*u_rows, bt_f, cl_f, bt_b, cl_b, m_st, m_out, a_re, a_im)


def _gelu_tanh(x):
    return 0.5 * x * (1.0 + jnp.tanh(math.sqrt(2.0 / math.pi) * (x + 0.044715 * (x * x * x))))


def _post_mix_kernel(att_ref, ys_ref, u_ref, x_ref, mod_ref, dsk_ref, wglu_ref, bglu_ref,
                     ag_ref, sg_ref, wo_ref, g2_ref, x1_ref, h2_ref, y_cols):
    gate1 = mod_ref[0, 2:3, :]
    shift2 = mod_ref[0, 3:4, :]
    scale2 = mod_ref[0, 4:5, :]
    _from_chunk_rows(ys_ref, y_cols)
    y = jnp.concatenate([y_cols[j] for j in range(SSM_WIDTH // LANES)], axis=1)
    y = y + dsk_ref[...] * u_ref[0]
    y = _gelu_tanh(y)
    z = _dot(y.astype(BF16), wglu_ref[...]) + bglu_ref[...]
    ssm = y * jax.nn.sigmoid(z)
    att = att_ref[0].astype(F32)
    att_n = (att * _rms(att, ATT_WIDTH) * ag_ref[...]).astype(BF16)
    ssm_n = (ssm * _rms(ssm, SSM_WIDTH) * sg_ref[...]).astype(BF16)
    mix = _dot(att_n, wo_ref[0:ATT_WIDTH, :]) + _dot(ssm_n, wo_ref[ATT_WIDTH:2 * ATT_WIDTH, :])
    x1 = x_ref[0] + gate1 * mix
    x1_ref[0] = x1
    h2 = (x1 * _rms(x1, D_MODEL) * g2_ref[...]) * (1.0 + scale2) + shift2
    h2_ref[0] = h2.astype(BF16)


def post_mix(att, ys, u, x, mod, d_skip, w_glu, b_glu, att_g, ssm_g, w_o, g2):
    b, length, d = x.shape
    tm = S5_TILE
    const = lambda bi, i: (0, 0)
    tile = lambda w: pl.BlockSpec((1, tm, w), lambda bi, i: (bi, i, 0))
    return pl.pallas_call(
        _post_mix_kernel,
        out_shape=(jax.ShapeDtypeStruct((b, length, d), F32),
                   jax.ShapeDtypeStruct((b, length, d), BF16)),
        grid=(b, length // tm),
        in_specs=[
            tile(ATT_WIDTH),
            pl.BlockSpec((N_GROUPS, S5_SUB, S5_COLS), lambda bi, i: (0, i, bi)),
            tile(SSM_WIDTH), tile(d),
            pl.BlockSpec((1, 6, d), lambda bi, i: (bi, 0, 0)),
            pl.BlockSpec((1, SSM_WIDTH), const),
            pl.BlockSpec((SSM_WIDTH, SSM_WIDTH), const),
            pl.BlockSpec((1, SSM_WIDTH), const),
            pl.BlockSpec((1, ATT_WIDTH), const),
            pl.BlockSpec((1, SSM_WIDTH), const),
            pl.BlockSpec((d, d), const),
            pl.BlockSpec((1, d), const),
        ],
        out_specs=(tile(d), tile(d)),
        scratch_shapes=[pltpu.VMEM((SSM_WIDTH // LANES, tm, LANES), F32)],
        compiler_params=_params(("parallel", "parallel")),
        name="post_mix",
    )(att, ys, u, x, mod, d_skip, w_glu, b_glu, att_g, ssm_g, w_o, g2)


def _ffn_up_kernel(h_ref, w1_ref, w3_ref, o_ref, w1_bf, w3_bf):
    @pl.when(jnp.logical_and(pl.program_id(1) == 0, pl.program_id(2) == 0))
    def _():
        w1_bf[...] = w1_ref[...].astype(BF16)
        w3_bf[...] = w3_ref[...].astype(BF16)

    h = h_ref[0]
    a = _dot(h, w1_bf[...])
    g = _dot(h, w3_bf[...])
    o_ref[0] = (a * jax.nn.sigmoid(a) * g).astype(BF16)


def ffn_up(h2, w1, w3):
    b, length, d = h2.shape
    tm, tf = min(FFN_UP_TM, length), FFN_UP_TF
    return pl.pallas_call(
        _ffn_up_kernel,
        out_shape=jax.ShapeDtypeStruct((b, length, D_FF), BF16),
        grid=(D_FF // tf, b, length // tm),
        in_specs=[
            pl.BlockSpec((1, tm, d), lambda f, bi, i: (bi, i, 0)),
            pl.BlockSpec((d, tf), lambda f, bi, i: (0, f)),
            pl.BlockSpec((d, tf), lambda f, bi, i: (0, f)),
        ],
        out_specs=pl.BlockSpec((1, tm, tf), lambda f, bi, i: (bi, i, f)),
        scratch_shapes=[pltpu.VMEM((d, tf), BF16), pltpu.VMEM((d, tf), BF16)],
        compiler_params=_params(("parallel", "arbitrary", "arbitrary")),
        name="ffn_up",
    )(h2, w1, w3)


def _ffn_down_kernel(a_ref, x1_ref, mod_ref, w2_ref, o_ref):
    o_ref[0] = x1_ref[0] + mod_ref[0, 5:6, :] * _dot(a_ref[0], w2_ref[...])


def ffn_down(act, x1, mod, w2):
    b, length, d = x1.shape
    tm, tn = min(FFN_DOWN_TM, length), FFN_DOWN_TN
    return pl.pallas_call(
        _ffn_down_kernel,
        out_shape=jax.ShapeDtypeStruct((b, length, d), F32),
        grid=(d // tn, b, length // tm),
        in_specs=[
            pl.BlockSpec((1, tm, D_FF), lambda n, bi, i: (bi, i, 0)),
            pl.BlockSpec((1, tm, tn), lambda n, bi, i: (bi, i, n)),
            pl.BlockSpec((1, 6, tn), lambda n, bi, i: (bi, 0, n)),
            pl.BlockSpec((D_FF, tn), lambda n, bi, i: (0, n)),
        ],
        out_specs=pl.BlockSpec((1, tm, tn), lambda n, bi, i: (bi, i, n)),
        compiler_params=_params(("parallel", "parallel", "parallel")),
        name="ffn_down",
    )(act, x1, mod, w2)


def _cast_kernel(x_ref, o_ref):
    o_ref[...] = x_ref[...].astype(o_ref.dtype)


def cast_bf16(w):
    rows, cols = w.shape
    tr = rows
    while tr * cols * 4 > CAST_BLOCK_BYTES and tr % (4 * SUBLANES) == 0:
        tr //= 2
    return pl.pallas_call(
        _cast_kernel,
        out_shape=jax.ShapeDtypeStruct((rows, cols), BF16),
        grid=(rows // tr,),
        in_specs=[pl.BlockSpec((tr, cols), lambda i: (i, 0))],
        out_specs=pl.BlockSpec((tr, cols), lambda i: (i, 0)),
        compiler_params=_params(("parallel",)),
        name="cast_bf16",
    )(w)


def _swap_halves(a, axis=-1):
    lo, hi = jnp.split(a, 2, axis=axis)
    return jnp.concatenate([hi, lo], axis=axis)


def _prep_w_in_kernel(w_ref, o_ref):
    w = w_ref[...]
    q_cols = N_HEADS * QK_DIM
    half = ROPE_DIM // 2

    def with_swapped(r):
        return [r, r[:, half:], r[:, :half]]

    cols = [w[:, hd * QK_DIM:hd * QK_DIM + NOPE_DIM] for hd in range(N_HEADS)]
    for hd in range(N_HEADS):
        cols += with_swapped(w[:, hd * QK_DIM + NOPE_DIM:(hd + 1) * QK_DIM])
    cols.append(w[:, q_cols:q_cols + KV_RANK])
    cols += with_swapped(w[:, q_cols + KV_RANK:q_cols + KV_RANK + ROPE_DIM])
    cols.append(w[:, q_cols + KV_RANK + ROPE_DIM:])
    o_ref[...] = jnp.concatenate(cols, axis=1).astype(BF16)


def prep_w_in(w_in):
    d, n = w_in.shape
    return pl.pallas_call(
        _prep_w_in_kernel,
        out_shape=jax.ShapeDtypeStruct((d, C_END), BF16),
        grid=(d // W_PREP_ROWS,),
        in_specs=[pl.BlockSpec((W_PREP_ROWS, n), lambda i: (i, 0))],
        out_specs=pl.BlockSpec((W_PREP_ROWS, C_END), lambda i: (i, 0)),
        compiler_params=_params(("parallel",)),
        name="prep_w_in",
    )(w_in)


def prep_w_ukv(w_ukv):
    rank, n = w_ukv.shape
    src = lambda j: (0, 2 * (j % N_HEADS) + j // N_HEADS)
    return pl.pallas_call(
        _cast_kernel,
        out_shape=jax.ShapeDtypeStruct((rank, n), BF16),
        grid=(n // LANES,),
        in_specs=[pl.BlockSpec((rank, LANES), src)],
        out_specs=pl.BlockSpec((rank, LANES), lambda j: (0, j)),
        compiler_params=_params(("parallel",)),
        name="prep_w_ukv",
    )(w_ukv)


def _rope_gain(g):
    gr = g[NOPE_DIM:]
    return jnp.concatenate([gr, _swap_halves(gr)]).reshape(1, 2 * ROPE_DIM)


def _rope_tables(length):
    pos = jnp.arange(length, dtype=F32)
    inv_freq = ROPE_BASE ** (-jnp.arange(0, ROPE_DIM, 2, dtype=F32) / ROPE_DIM)
    ang = pos[:, None] * inv_freq[None, :]
    cos, sin = jnp.cos(ang), jnp.sin(ang)
    return jnp.tile(cos, (1, 4)), jnp.tile(jnp.concatenate([-sin, sin], axis=-1), (1, 2))


def kernel(x_prompt, x_sample, c_prompt, c_sample, w_ada, b_ada, norm_mix_g, w_in, kv_norm_g, w_ukv,
           q_norm_g, k_norm_g, lam_re, lam_im, log_dt, b_re, b_im, c_re, c_im, d_skip, w_glu, b_glu,
           att_out_g, ssm_out_g, w_o, norm_ffn_g, w1, w3, w2):
    assert w_ada.shape[0] == 1, "single-layer kernel"
    xs = (x_prompt, x_sample)
    batches = tuple(x.shape[0] for x in xs)
    length = x_prompt.shape[1]
    assert x_sample.shape[1] == length and sum(batches) <= S5_ROWS

    c_all = jnp.concatenate([c_prompt, c_sample], axis=0)
    c8 = jnp.pad(c_all, ((0, SUBLANES - c_all.shape[0]), (0, 0)))
    mod_all = ada_mod(c8, w_ada[0], b_ada[0]).reshape(SUBLANES, 6, D_MODEL)
    mods = (mod_all[:batches[0]], mod_all[batches[0]:batches[0] + batches[1]])

    w_all = prep_w_in(w_in[0])
    w_ukv_p = prep_w_ukv(w_ukv[0])
    row = lambda a: a.reshape(1, -1)
    q_gn, k_gn = row(q_norm_g[0][:NOPE_DIM]), row(k_norm_g[0][:NOPE_DIM])
    q_gr, k_gr = _rope_gain(q_norm_g[0]), _rope_gain(k_norm_g[0])
    cos_t, sin_t = _rope_tables(length)
    s5_ops = s5_matrices(lam_re[0], lam_im[0], log_dt[0], b_re[0], b_im[0], c_re[0], c_im[0])
    w_glu_b, w_o_b = cast_bf16(w_glu[0]), cast_bf16(w_o[0])
    w2_b = cast_bf16(w2[0])

    proj = [in_proj(x, m, row(norm_mix_g[0]), w_all, w_ukv_p, row(kv_norm_g[0]), q_gn, k_gn, q_gr, k_gr,
                    cos_t, sin_t) for x, m in zip(xs, mods)]
    atts = [attention(q, k, v) for q, k, v, _, _ in proj]
    y_rows = s5_scan([p[4] for p in proj], *s5_ops)

    outs = []
    for x, m, att, ys, p in zip(xs, mods, atts, y_rows, proj):
        x1, h2 = post_mix(att, ys, p[3], x, m, row(d_skip[0]), w_glu_b, row(b_glu[0]), row(att_out_g[0]),
                          row(ssm_out_g[0]), w_o_b, row(norm_ffn_g[0]))
        outs.append(ffn_down(ffn_up(h2, w1[0], w3[0]), x1, m, w2_b))
    return tuple(outs)
```

```python
import functools
import math

import jax
import jax.numpy as jnp
import numpy as np
from jax import lax
from jax.experimental import pallas as pl
from jax.experimental.pallas import tpu as pltpu

F32 = jnp.float32
BF16 = jnp.bfloat16

D_MODEL = 2048
ATT_WIDTH = 1024
SSM_WIDTH = 1024
N_HEADS = 8
V_DIM = 128
NOPE_DIM = 128
ROPE_DIM = 64
QK_DIM = NOPE_DIM + ROPE_DIM
KV_RANK = 512
ROPE_BASE = 10000.0
SSM_GROUP = 16
N_GROUPS = SSM_WIDTH // SSM_GROUP
SSM_STATE = 64
D_FF = 5632
NORM_EPS = 1e-6

LANES = 128
SUBLANES = 8
VMEM_LIMIT = 56 * 1024 * 1024
CAST_BLOCK_BYTES = 4 * 1024 * 1024

ADA_TN = 1024
ATT_TQ, ATT_TK = 1024, 256
FFN_UP_TM, FFN_UP_TF = 1024, 512
FFN_DOWN_TM, FFN_DOWN_TN = 512, 1024
W_PREP_ROWS = 256

C_QN = 0
C_QR = C_QN + N_HEADS * NOPE_DIM
C_KV = C_QR + N_HEADS * 2 * ROPE_DIM
C_KR = C_KV + KV_RANK
C_U = C_KR + 2 * ROPE_DIM
C_END = C_U + SSM_WIDTH

S5_CHUNK = 32
S5_ROWS = SUBLANES
S5_SUB = SUBLANES
S5_TILE = S5_SUB * S5_CHUNK
S5_COLS = S5_CHUNK * SSM_GROUP
S5_STATE_COLS = 4 * SSM_STATE
GROUPS_PER_COL = LANES // SSM_GROUP


def _params(sem):
    return pltpu.CompilerParams(dimension_semantics=sem, vmem_limit_bytes=VMEM_LIMIT)


def _dot(a, b):
    return jnp.dot(a, b, preferred_element_type=F32)


def _dot_f32(a, b):
    a_hi = a.astype(BF16)
    a_lo = (a - a_hi.astype(F32)).astype(BF16)
    b_hi = b.astype(BF16)
    b_lo = (b - b_hi.astype(F32)).astype(BF16)
    return _dot(a_hi, b_hi) + _dot(a_lo, b_hi) + _dot(a_hi, b_lo)


def _ada_kernel(c_ref, w_ref, b_ref, o_ref):
    c = c_ref[...]
    o_ref[...] = _dot_f32(c * jax.nn.sigmoid(c), w_ref[...]) + b_ref[...]


def ada_mod(c8, w_ada, b_ada):
    rows, d = c8.shape
    n = w_ada.shape[1]
    tn = ADA_TN
    return pl.pallas_call(
        _ada_kernel,
        out_shape=jax.ShapeDtypeStruct((rows, n), F32),
        grid=(n // tn,),
        in_specs=[
            pl.BlockSpec((rows, d), lambda j: (0, 0)),
            pl.BlockSpec((d, tn), lambda j: (0, j)),
            pl.BlockSpec((1, tn), lambda j: (0, j)),
        ],
        out_specs=pl.BlockSpec((rows, tn), lambda j: (0, j)),
        compiler_params=_params(("arbitrary",)),
        name="ada_mod",
    )(c8, w_ada, b_ada.reshape(1, n))


def _rms(x, width):
    return lax.rsqrt(jnp.sum(x * x, axis=-1, keepdims=True) * (1.0 / width) + NORM_EPS)


def _granule_masks():
    pos = lax.broadcasted_iota(jnp.int32, (SUBLANES, LANES), 1) // SSM_GROUP
    return [pos == p for p in range(GROUPS_PER_COL)]


def _to_chunk_rows(u_cols, dst):
    masks = _granule_masks()
    for j in range(SSM_WIDTH // LANES):
        rolled = []
        for t in range(S5_CHUNK):
            v = u_cols[j, pl.ds(t, S5_SUB, stride=S5_CHUNK), :]
            r = t % GROUPS_PER_COL
            rolled.append(pltpu.roll(v, SSM_GROUP * r, axis=1) if r else v)
        for k in range(GROUPS_PER_COL):
            for d in range(S5_COLS // LANES):
                acc = rolled[GROUPS_PER_COL * d + GROUPS_PER_COL - 1]
                for rho in range(GROUPS_PER_COL - 2, -1, -1):
                    acc = jnp.where(masks[(k + rho) % GROUPS_PER_COL], rolled[GROUPS_PER_COL * d + rho], acc)
                dst[GROUPS_PER_COL * j + k, :, d * LANES:(d + 1) * LANES] = acc


def _from_chunk_rows(src, y_cols):
    masks = _granule_masks()
    for j in range(SSM_WIDTH // LANES):
        for d in range(S5_COLS // LANES):
            cols = [src[GROUPS_PER_COL * j + k, :, d * LANES:(d + 1) * LANES] for k in range(GROUPS_PER_COL)]
            for rho in range(GROUPS_PER_COL):
                acc = cols[GROUPS_PER_COL - 1]
                for k in range(GROUPS_PER_COL - 2, -1, -1):
                    acc = jnp.where(masks[(k + rho) % GROUPS_PER_COL], cols[k], acc)
                v = pltpu.roll(acc, LANES - SSM_GROUP * rho, axis=1) if rho else acc
                y_cols[j, pl.ds(GROUPS_PER_COL * d + rho, S5_SUB, stride=S5_CHUNK), :] = v


def _in_proj_kernel(x_ref, mod_ref, g1_ref, w_ref, wukv_ref, kvg_ref, qg_ref, kg_ref,
                    qgr_ref, kgr_ref, cos_ref, sin_ref, q_ref, k_ref, v_ref, u_ref, ur_ref, u_cols,
                    *, q_scale):
    x = x_ref[0]
    shift = mod_ref[0, 0:1, :]
    scale = mod_ref[0, 1:2, :]
    h = (x * _rms(x, D_MODEL) * g1_ref[...]) * (1.0 + scale) + shift
    h = h.astype(BF16)

    cos = cos_ref[...]
    sin = sin_ref[...]

    def rope(col, gain):
        cg = col * gain
        return cg * cos + pltpu.roll(cg, ROPE_DIM, axis=1) * sin

    qn = _dot(h, w_ref[:, C_QN:C_QR])
    qr = _dot(h, w_ref[:, C_QR:C_KV])
    qg_n = qg_ref[...]
    qg_r = qgr_ref[...]
    for hd in range(N_HEADS):
        qn_h = qn[:, hd * LANES:(hd + 1) * LANES]
        qr_h = qr[:, hd * LANES:(hd + 1) * LANES]
        ss = (jnp.sum(qn_h * qn_h, axis=-1, keepdims=True)
              + 0.5 * jnp.sum(qr_h * qr_h, axis=-1, keepdims=True))
        r = lax.rsqrt(ss * (1.0 / QK_DIM) + NORM_EPS) * q_scale
        q_ref[0, hd, :, 0:NOPE_DIM] = (qn_h * r * qg_n).astype(BF16)
        q_ref[0, hd, :, NOPE_DIM:QK_DIM] = (rope(qr_h, qg_r) * r)[:, 0:ROPE_DIM].astype(BF16)

    ckv = _dot(h, w_ref[:, C_KV:C_KR])
    ckv_n = (ckv * _rms(ckv, KV_RANK) * kvg_ref[...]).astype(BF16)
    kv = _dot(ckv_n, wukv_ref[...])
    kr = _dot(h, w_ref[:, C_KR:C_U])
    kr_ss = 0.5 * jnp.sum(kr * kr, axis=-1, keepdims=True)
    kr_rot = rope(kr, kgr_ref[...])
    kg_n = kg_ref[...]
    ones = jnp.ones((x.shape[0], LANES), BF16)
    for hd in range(N_HEADS):
        kn_h = kv[:, hd * LANES:(hd + 1) * LANES]
        ss = jnp.sum(kn_h * kn_h, axis=-1, keepdims=True) + kr_ss
        r = lax.rsqrt(ss * (1.0 / QK_DIM) + NORM_EPS)
        k_ref[0, hd, :, 0:NOPE_DIM] = (kn_h * r * kg_n).astype(BF16)
        k_ref[0, hd, :, NOPE_DIM:QK_DIM] = (kr_rot * r)[:, 0:ROPE_DIM].astype(BF16)
        v_h = kv[:, ATT_WIDTH + hd * LANES:ATT_WIDTH + (hd + 1) * LANES]
        v_ref[0, hd, :, 0:V_DIM] = v_h.astype(BF16)
        v_ref[0, hd, :, V_DIM:2 * V_DIM] = ones

    u = _dot(h, w_ref[:, C_U:C_END])
    u_ref[0] = u
    for j in range(SSM_WIDTH // LANES):
        u_cols[j] = u[:, j * LANES:(j + 1) * LANES]
    _to_chunk_rows(u_cols, ur_ref.at[:, 0])


def in_proj(x, mod, g1, w_all, w_ukv, kv_g, q_gn, k_gn, q_gr, k_gr, cos_t, sin_t):
    b, length, d = x.shape
    tm = S5_TILE
    assert length % tm == 0
    const = lambda bi, i: (0, 0)
    kern = functools.partial(_in_proj_kernel, q_scale=math.log2(math.e) / math.sqrt(QK_DIM))
    return pl.pallas_call(
        kern,
        out_shape=(
            jax.ShapeDtypeStruct((b, N_HEADS, length, QK_DIM), BF16),
            jax.ShapeDtypeStruct((b, N_HEADS, length, QK_DIM), BF16),
            jax.ShapeDtypeStruct((b, N_HEADS, length, 2 * V_DIM), BF16),
            jax.ShapeDtypeStruct((b, length, SSM_WIDTH), F32),
            jax.ShapeDtypeStruct((N_GROUPS, b, length // S5_CHUNK, S5_COLS), F32),
        ),
        grid=(b, length // tm),
        in_specs=[
            pl.BlockSpec((1, tm, d), lambda bi, i: (bi, i, 0)),
            pl.BlockSpec((1, 6, d), lambda bi, i: (bi, 0, 0)),
            pl.BlockSpec((1, d), const),
            pl.BlockSpec((d, C_END), const),
            pl.BlockSpec((KV_RANK, 2 * ATT_WIDTH), const),
            pl.BlockSpec((1, KV_RANK), const),
            pl.BlockSpec((1, LANES), const),
            pl.BlockSpec((1, LANES), const),
            pl.BlockSpec((1, LANES), const),
            pl.BlockSpec((1, LANES), const),
            pl.BlockSpec((tm, LANES), lambda bi, i: (i, 0)),
            pl.BlockSpec((tm, LANES), lambda bi, i: (i, 0)),
        ],
        out_specs=(
            pl.BlockSpec((1, N_HEADS, tm, QK_DIM), lambda bi, i: (bi, 0, i, 0)),
            pl.BlockSpec((1, N_HEADS, tm, QK_DIM), lambda bi, i: (bi, 0, i, 0)),
            pl.BlockSpec((1, N_HEADS, tm, 2 * V_DIM), lambda bi, i: (bi, 0, i, 0)),
            pl.BlockSpec((1, tm, SSM_WIDTH), lambda bi, i: (bi, i, 0)),
            pl.BlockSpec((N_GROUPS, 1, S5_SUB, S5_COLS), lambda bi, i: (0, bi, i, 0)),
        ),
        scratch_shapes=[pltpu.VMEM((SSM_WIDTH // LANES, tm, LANES), F32)],
        compiler_params=_params(("parallel", "parallel")),
        name="in_proj",
    )(x, mod, g1, w_all, w_ukv, kv_g, q_gn, k_gn, q_gr, k_gr, cos_t, sin_t)


def _attn_kernel(q_ref, k_ref, v_ref, o_ref, *, tk):
    q = q_ref[0, 0]
    tq = q.shape[0]
    m = jnp.full((tq, 1), -jnp.inf, F32)
    acc = jnp.zeros((tq, 2 * V_DIM), F32)
    for j in range(k_ref.shape[2] // tk):
        kc = k_ref[0, 0, j * tk:(j + 1) * tk, :]
        vc = v_ref[0, 0, j * tk:(j + 1) * tk, :]
        s = lax.dot_general(q, kc, (((1,), (1,)), ((), ())), preferred_element_type=F32)
        m_new = jnp.maximum(m, jnp.max(s, axis=-1, keepdims=True))
        alpha = jnp.exp2(m - m_new)
        p = jnp.exp2(s - m_new).astype(BF16)
        acc = alpha * acc + _dot(p, vc)
        m = m_new
    o_ref[0] = (acc[:, 0:V_DIM] / acc[:, V_DIM:2 * V_DIM]).astype(o_ref.dtype)


def attention(q, k, v):
    b, h, length, _ = q.shape
    tq = min(ATT_TQ, length)
    tk = min(ATT_TK, length)
    return pl.pallas_call(
        functools.partial(_attn_kernel, tk=tk),
        out_shape=jax.ShapeDtypeStruct((b, length, h * V_DIM), BF16),
        grid=(b, h, length // tq),
        in_specs=[
            pl.BlockSpec((1, 1, tq, QK_DIM), lambda bi, hi, i: (bi, hi, i, 0)),
            pl.BlockSpec((1, 1, length, QK_DIM), lambda bi, hi, i: (bi, hi, 0, 0)),
            pl.BlockSpec((1, 1, length, 2 * V_DIM), lambda bi, hi, i: (bi, hi, 0, 0)),
        ],
        out_specs=pl.BlockSpec((1, tq, V_DIM), lambda bi, hi, i: (bi, i, hi)),
        compiler_params=_params(("parallel", "parallel", "parallel")),
        name="attention",
    )(q, k, v)


def _s5_steps():
    g = np.arange(N_GROUPS)[:, None] % GROUPS_PER_COL
    slot = np.arange(S5_CHUNK)[None, :]
    return GROUPS_PER_COL * (slot // GROUPS_PER_COL) + (slot % GROUPS_PER_COL - g) % GROUPS_PER_COL


def s5_matrices(lam_re, lam_im, log_dt, b_re, b_im, c_re, c_im):
    t = S5_CHUNK
    dt = jnp.exp(log_dt)[..., None]
    la, lb = lam_re * dt, lam_im * dt
    lb_re, lb_im = jnp.exp(la) * jnp.cos(lb), jnp.exp(la) * jnp.sin(lb)
    den = lam_re * lam_re + lam_im * lam_im
    n_re, n_im = lb_re - 1.0, lb_im
    cf_re = (n_re * lam_re + n_im * lam_im) / den
    cf_im = (n_im * lam_re - n_re * lam_im) / den
    bb_re = cf_re[..., None] * b_re[None] - cf_im[..., None] * b_im[None]
    bb_im = cf_re[..., None] * b_im[None] + cf_im[..., None] * b_re[None]

    steps = _s5_steps().astype(np.float32)

    def power(d, expo):
        e = jnp.asarray(expo)[:, None, :]
        mag = jnp.exp(la[d][..., None] * e)
        return mag * jnp.cos(lb[d][..., None] * e), mag * jnp.sin(lb[d][..., None] * e)

    cat = jnp.concatenate

    def rows_op(pw, xs1, xs2):
        n = len(xs1)
        pr = cat([jnp.swapaxes(p[0], 1, 2) for p in pw], axis=-1)[:, :, None, :]
        pi = cat([jnp.swapaxes(p[1], 1, 2) for p in pw], axis=-1)[:, :, None, :]
        x1 = cat([jnp.swapaxes(x, 1, 2) for x in xs1], axis=-1)[:, None]
        x2 = cat([jnp.swapaxes(x, 1, 2) for x in xs2], axis=-1)[:, None]
        return (pr * x1 + pi * x2).reshape(N_GROUPS, S5_COLS, n * SSM_STATE)

    def cols_op(pw, xs1, xs2):
        pr = jnp.repeat(cat([p[0] for p in pw], axis=1), SSM_GROUP, axis=2)
        pi = jnp.repeat(cat([p[1] for p in pw], axis=1), SSM_GROUP, axis=2)
        x1 = jnp.tile(cat(xs1, axis=1), (1, 1, S5_CHUNK))
        x2 = jnp.tile(cat(xs2, axis=1), (1, 1, S5_CHUNK))
        return pr * x1 + pi * x2

    bf_re, bf_im, bb_re_, bb_im_ = bb_re[0], bb_im[0], bb_re[1], bb_im[1]
    cf_re_, cf_im_, cb_re, cb_im = c_re[0], c_im[0], c_re[1], c_im[1]
    tr = lambda x: jnp.swapaxes(x, 1, 2)
    p_sb, p_sf = power(1, steps), power(0, t - 1 - steps)
    p_of, p_ob = power(0, steps + 1), power(1, t - steps)

    lags = np.broadcast_to(np.arange(t, dtype=np.float32), (N_GROUPS, t))
    p_kf, p_kb = power(0, lags), power(1, t - 1 - lags)
    cl_f = cols_op([p_kf, p_kf], [tr(cf_re_), tr(cf_im_)], [-tr(cf_im_), tr(cf_re_)])
    cl_b = cols_op([p_kb, p_kb], [tr(cb_re), tr(cb_im)], [-tr(cb_im), tr(cb_re)])
    bt_f = cat([tr(bf_re), -tr(bf_im)], axis=-1)
    bt_b = cat([tr(bb_re_), -tr(bb_im_)], axis=-1)
    m_st = rows_op([p_sf, p_sb, p_sf, p_sb], [bf_re, bb_re_, bf_im, bb_im_],
                   [-bf_im, -bb_im_, bf_re, bb_re_]).astype(BF16)
    m_out = cols_op([p_of, p_ob, p_of, p_ob], [tr(cf_re_), tr(cb_re), -tr(cf_im_), -tr(cb_im)],
                    [-tr(cf_im_), -tr(cb_im), -tr(cf_re_), -tr(cb_re)]).astype(BF16)
    a_re = cat([jnp.exp(t * la[0]) * jnp.cos(t * lb[0]), jnp.exp(t * la[1]) * jnp.cos(t * lb[1])], axis=-1)[:, None, :]
    a_im = cat([jnp.exp(t * la[0]) * jnp.sin(t * lb[0]), jnp.exp(t * la[1]) * jnp.sin(t * lb[1])], axis=-1)[:, None, :]
    return bt_f, cl_f, bt_b, cl_b, m_st, m_out, a_re, a_im


def _s5_kernel(*refs, batches):
    n_in = len(batches)
    u_refs = refs[:n_in]
    btf_ref, clf_ref, btb_ref, clb_ref, mst_ref, mout_ref, are_ref, aim_ref = refs[n_in:n_in + 8]
    y_refs = refs[n_in + 8:2 * n_in + 8]
    st_ref, xin_ref, tnat_ref = refs[2 * n_in + 8:]
    n_chunks = u_refs[0].shape[2]

    k_f = _dot_f32(btf_ref[0], clf_ref[0])
    k_b = _dot_f32(btb_ref[0], clb_ref[0])
    lane = lax.broadcasted_iota(jnp.int32, (SSM_GROUP, S5_COLS), 1)
    for s in range(S5_CHUNK):
        lo, hi = s * SSM_GROUP, (s + 1) * SSM_GROUP
        fwd = jnp.where(lane >= lo, pltpu.roll(k_f, lo, axis=1) if lo else k_f, 0.0)
        bwd = jnp.where(lane < hi, pltpu.roll(k_b, hi, axis=1) if hi < S5_COLS else k_b, 0.0)
        tnat_ref[lo:hi, :] = (fwd + bwd).astype(BF16)

    def natural_of(idx):
        k = pl.program_id(0) % GROUPS_PER_COL
        d = lax.shift_right_logical(idx, LANES.bit_length() - 1)
        pos = lax.shift_right_logical(idx, SSM_GROUP.bit_length() - 1) & (GROUPS_PER_COL - 1)
        step = d * GROUPS_PER_COL + ((pos - k) & (GROUPS_PER_COL - 1))
        return step * SSM_GROUP + (idx & (SSM_GROUP - 1))

    r_idx = lax.broadcasted_iota(jnp.int32, (S5_COLS, S5_COLS), 0)
    c_idx = lax.broadcasted_iota(jnp.int32, (S5_COLS, S5_COLS), 1)
    pick_rows = jnp.where(natural_of(r_idx) == c_idx, 1.0, 0.0).astype(BF16)
    pick_cols = jnp.where(r_idx == natural_of(c_idx), 1.0, 0.0).astype(BF16)
    toep = _dot(_dot(pick_rows, tnat_ref[...]).astype(BF16), pick_cols).astype(BF16)
    seqs = [(u_ref, y_ref, b) for u_ref, y_ref, nb in zip(u_refs, y_refs, batches) for b in range(nb)]

    def u_rows(u_ref, b):
        return u_ref[0, b].astype(BF16)

    for slot in range(S5_ROWS):
        rows = pl.ds(slot, n_chunks, stride=S5_ROWS)
        if slot < len(seqs):
            st = _dot(u_rows(seqs[slot][0], seqs[slot][2]), mst_ref[0])
        else:
            st = jnp.zeros((n_chunks, S5_STATE_COLS), F32)
        st_ref[0, rows, :] = st[:, 0:LANES]
        st_ref[1, rows, :] = st[:, LANES:2 * LANES]

    a_re = jnp.broadcast_to(are_ref[0], (S5_ROWS, LANES))
    a_im = jnp.broadcast_to(aim_ref[0], (S5_ROWS, LANES))
    is_fwd = lax.broadcasted_iota(jnp.int32, (S5_ROWS, LANES), 1) < SSM_STATE
    half = SSM_STATE

    def body(i, carry):
        x_re, x_im = carry
        rf = pl.multiple_of(i * S5_ROWS, S5_ROWS)
        rb = pl.multiple_of((n_chunks - 1 - i) * S5_ROWS, S5_ROWS)
        xin_ref[0, pl.ds(rf, S5_ROWS), 0:half] = x_re[:, 0:half]
        xin_ref[0, pl.ds(rb, S5_ROWS), half:LANES] = x_re[:, half:LANES]
        xin_ref[1, pl.ds(rf, S5_ROWS), 0:half] = x_im[:, 0:half]
        xin_ref[1, pl.ds(rb, S5_ROWS), half:LANES] = x_im[:, half:LANES]
        s_re = jnp.where(is_fwd, st_ref[0, pl.ds(rf, S5_ROWS), :], st_ref[0, pl.ds(rb, S5_ROWS), :])
        s_im = jnp.where(is_fwd, st_ref[1, pl.ds(rf, S5_ROWS), :], st_ref[1, pl.ds(rb, S5_ROWS), :])
        n_re = a_re * x_re - a_im * x_im + s_re
        n_im = a_re * x_im + a_im * x_re + s_im
        return n_re, n_im

    zero = jnp.zeros((S5_ROWS, LANES), F32)
    lax.fori_loop(0, n_chunks, body, (zero, zero), unroll=4)

    for slot, (u_ref, y_ref, b) in enumerate(seqs):
        rows = pl.ds(slot, n_chunks, stride=S5_ROWS)
        x_in = jnp.concatenate([xin_ref[0, rows, :], xin_ref[1, rows, :]], axis=1).astype(BF16)
        y_ref[0, b] = _dot(u_rows(u_ref, b), toep) + _dot(x_in, mout_ref[0])


def s5_scan(u_rows, bt_f, cl_f, bt_b, cl_b, m_st, m_out, a_re, a_im):
    g, _, n_chunks, _ = u_rows[0].shape
    batches = tuple(u.shape[1] for u in u_rows)
    assert sum(batches) <= S5_ROWS
    gmap = lambda gi: (gi, 0, 0)
    io_specs = [pl.BlockSpec((1, nb, n_chunks, S5_COLS), lambda gi: (gi, 0, 0, 0)) for nb in batches]
    return pl.pallas_call(
        functools.partial(_s5_kernel, batches=batches),
        out_shape=tuple(jax.ShapeDtypeStruct(u.shape, F32) for u in u_rows),
        grid=(g,),
        in_specs=io_specs + [
            pl.BlockSpec((1, SSM_GROUP, 2 * SSM_STATE), gmap),
            pl.BlockSpec((1, 2 * SSM_STATE, S5_COLS), gmap),
            pl.BlockSpec((1, SSM_GROUP, 2 * SSM_STATE), gmap),
            pl.BlockSpec((1, 2 * SSM_STATE, S5_COLS), gmap),
            pl.BlockSpec((1, S5_COLS, S5_STATE_COLS), gmap),
            pl.BlockSpec((1, S5_STATE_COLS, S5_COLS), gmap),
            pl.BlockSpec((1, 1, LANES), gmap),
            pl.BlockSpec((1, 1, LANES), gmap),
        ],
        out_specs=tuple(io_specs),
        scratch_shapes=[pltpu.VMEM((2, n_chunks * S5_ROWS, LANES), F32),
                        pltpu.VMEM((2, n_chunks * S5_ROWS, LANES), F32),
                        pltpu.VMEM((S5_COLS, S5_COLS), BF16)],
        compiler_params=_params(("parallel",)),
        name="s5_scan",
    )(*u_rows, bt_f, cl_f, bt_b, cl_b, m_st, m_out, a_re, a_im)


def _gelu_tanh(x):
    return 0.5 * x * (1.0 + jnp.tanh(math.sqrt(2.0 / math.pi) * (x + 0.044715 * (x * x * x))))


def _post_mix_kernel(att_ref, ys_ref, u_ref, x_ref, mod_ref, dsk_ref, wglu_ref, bglu_ref,
                     ag_ref, sg_ref, wo_ref, g2_ref, x1_ref, h2_ref, y_cols):
    gate1 = mod_ref[0, 2:3, :]
    shift2 = mod_ref[0, 3:4, :]
    scale2 = mod_ref[0, 4:5, :]
    _from_chunk_rows(ys_ref.at[:, 0], y_cols)
    y = jnp.concatenate([y_cols[j] for j in range(SSM_WIDTH // LANES)], axis=1)
    y = y + dsk_ref[...] * u_ref[0]
    y = _gelu_tanh(y)
    z = _dot(y.astype(BF16), wglu_ref[...]) + bglu_ref[...]
    ssm = y * jax.nn.sigmoid(z)
    att = att_ref[0].astype(F32)
    att_n = (att * _rms(att, ATT_WIDTH) * ag_ref[...]).astype(BF16)
    ssm_n = (ssm * _rms(ssm, SSM_WIDTH) * sg_ref[...]).astype(BF16)
    mix = _dot(att_n, wo_ref[0:ATT_WIDTH, :]) + _dot(ssm_n, wo_ref[ATT_WIDTH:2 * ATT_WIDTH, :])
    x1 = x_ref[0] + gate1 * mix
    x1_ref[0] = x1
    h2 = (x1 * _rms(x1, D_MODEL) * g2_ref[...]) * (1.0 + scale2) + shift2
    h2_ref[0] = h2.astype(BF16)


def post_mix(att, ys, u, x, mod, d_skip, w_glu, b_glu, att_g, ssm_g, w_o, g2):
    b, length, d = x.shape
    tm = S5_TILE
    const = lambda bi, i: (0, 0)
    tile = lambda w: pl.BlockSpec((1, tm, w), lambda bi, i: (bi, i, 0))
    return pl.pallas_call(
        _post_mix_kernel,
        out_shape=(jax.ShapeDtypeStruct((b, length, d), F32),
                   jax.ShapeDtypeStruct((b, length, d), BF16)),
        grid=(b, length // tm),
        in_specs=[
            tile(ATT_WIDTH),
            pl.BlockSpec((N_GROUPS, 1, S5_SUB, S5_COLS), lambda bi, i: (0, bi, i, 0)),
            tile(SSM_WIDTH), tile(d),
            pl.BlockSpec((1, 6, d), lambda bi, i: (bi, 0, 0)),
            pl.BlockSpec((1, SSM_WIDTH), const),
            pl.BlockSpec((SSM_WIDTH, SSM_WIDTH), const),
            pl.BlockSpec((1, SSM_WIDTH), const),
            pl.BlockSpec((1, ATT_WIDTH), const),
            pl.BlockSpec((1, SSM_WIDTH), const),
            pl.BlockSpec((d, d), const),
            pl.BlockSpec((1, d), const),
        ],
        out_specs=(tile(d), tile(d)),
        scratch_shapes=[pltpu.VMEM((SSM_WIDTH // LANES, tm, LANES), F32)],
        compiler_params=_params(("parallel", "parallel")),
        name="post_mix",
    )(att, ys, u, x, mod, d_skip, w_glu, b_glu, att_g, ssm_g, w_o, g2)


def _ffn_up_kernel(h_ref, w1_ref, w3_ref, o_ref, w1_bf, w3_bf):
    @pl.when(jnp.logical_and(pl.program_id(1) == 0, pl.program_id(2) == 0))
    def _():
        w1_bf[...] = w1_ref[...].astype(BF16)
        w3_bf[...] = w3_ref[...].astype(BF16)

    h = h_ref[0]
    a = _dot(h, w1_bf[...])
    g = _dot(h, w3_bf[...])
    o_ref[0] = (a * jax.nn.sigmoid(a) * g).astype(BF16)


def ffn_up(h2, w1, w3):
    b, length, d = h2.shape
    tm, tf = min(FFN_UP_TM, length), FFN_UP_TF
    return pl.pallas_call(
        _ffn_up_kernel,
        out_shape=jax.ShapeDtypeStruct((b, length, D_FF), BF16),
        grid=(D_FF // tf, b, length // tm),
        in_specs=[
            pl.BlockSpec((1, tm, d), lambda f, bi, i: (bi, i, 0)),
            pl.BlockSpec((d, tf), lambda f, bi, i: (0, f)),
            pl.BlockSpec((d, tf), lambda f, bi, i: (0, f)),
        ],
        out_specs=pl.BlockSpec((1, tm, tf), lambda f, bi, i: (bi, i, f)),
        scratch_shapes=[pltpu.VMEM((d, tf), BF16), pltpu.VMEM((d, tf), BF16)],
        compiler_params=_params(("parallel", "arbitrary", "arbitrary")),
        name="ffn_up",
    )(h2, w1, w3)


def _ffn_down_kernel(a_ref, x1_ref, mod_ref, w2_ref, o_ref):
    o_ref[0] = x1_ref[0] + mod_ref[0, 5:6, :] * _dot(a_ref[0], w2_ref[...])


def ffn_down(act, x1, mod, w2):
    b, length, d = x1.shape
    tm, tn = min(FFN_DOWN_TM, length), FFN_DOWN_TN
    return pl.pallas_call(
        _ffn_down_kernel,
        out_shape=jax.ShapeDtypeStruct((b, length, d), F32),
        grid=(d // tn, b, length // tm),
        in_specs=[
            pl.BlockSpec((1, tm, D_FF), lambda n, bi, i: (bi, i, 0)),
            pl.BlockSpec((1, tm, tn), lambda n, bi, i: (bi, i, n)),
            pl.BlockSpec((1, 6, tn), lambda n, bi, i: (bi, 0, n)),
            pl.BlockSpec((D_FF, tn), lambda n, bi, i: (0, n)),
        ],
        out_specs=pl.BlockSpec((1, tm, tn), lambda n, bi, i: (bi, i, n)),
        compiler_params=_params(("parallel", "parallel", "parallel")),
        name="ffn_down",
    )(act, x1, mod, w2)


def _cast_kernel(x_ref, o_ref):
    o_ref[...] = x_ref[...].astype(o_ref.dtype)


def cast_bf16(w):
    rows, cols = w.shape
    tr = rows
    while tr * cols * 4 > CAST_BLOCK_BYTES and tr % (4 * SUBLANES) == 0:
        tr //= 2
    return pl.pallas_call(
        _cast_kernel,
        out_shape=jax.ShapeDtypeStruct((rows, cols), BF16),
        grid=(rows // tr,),
        in_specs=[pl.BlockSpec((tr, cols), lambda i: (i, 0))],
        out_specs=pl.BlockSpec((tr, cols), lambda i: (i, 0)),
        compiler_params=_params(("parallel",)),
        name="cast_bf16",
    )(w)


def _swap_halves(a, axis=-1):
    lo, hi = jnp.split(a, 2, axis=axis)
    return jnp.concatenate([hi, lo], axis=axis)


def _prep_w_in_kernel(w_ref, o_ref):
    w = w_ref[...]
    q_cols = N_HEADS * QK_DIM
    half = ROPE_DIM // 2

    def with_swapped(r):
        return [r, r[:, half:], r[:, :half]]

    cols = [w[:, hd * QK_DIM:hd * QK_DIM + NOPE_DIM] for hd in range(N_HEADS)]
    for hd in range(N_HEADS):
        cols += with_swapped(w[:, hd * QK_DIM + NOPE_DIM:(hd + 1) * QK_DIM])
    cols.append(w[:, q_cols:q_cols + KV_RANK])
    cols += with_swapped(w[:, q_cols + KV_RANK:q_cols + KV_RANK + ROPE_DIM])
    cols.append(w[:, q_cols + KV_RANK + ROPE_DIM:])
    o_ref[...] = jnp.concatenate(cols, axis=1).astype(BF16)


def prep_w_in(w_in):
    d, n = w_in.shape
    return pl.pallas_call(
        _prep_w_in_kernel,
        out_shape=jax.ShapeDtypeStruct((d, C_END), BF16),
        grid=(d // W_PREP_ROWS,),
        in_specs=[pl.BlockSpec((W_PREP_ROWS, n), lambda i: (i, 0))],
        out_specs=pl.BlockSpec((W_PREP_ROWS, C_END), lambda i: (i, 0)),
        compiler_params=_params(("parallel",)),
        name="prep_w_in",
    )(w_in)


def prep_w_ukv(w_ukv):
    rank, n = w_ukv.shape
    src = lambda j: (0, 2 * (j % N_HEADS) + j // N_HEADS)
    return pl.pallas_call(
        _cast_kernel,
        out_shape=jax.ShapeDtypeStruct((rank, n), BF16),
        grid=(n // LANES,),
        in_specs=[pl.BlockSpec((rank, LANES), src)],
        out_specs=pl.BlockSpec((rank, LANES), lambda j: (0, j)),
        compiler_params=_params(("parallel",)),
        name="prep_w_ukv",
    )(w_ukv)


def _rope_gain(g):
    gr = g[NOPE_DIM:]
    return jnp.concatenate([gr, _swap_halves(gr)]).reshape(1, 2 * ROPE_DIM)


def _rope_tables(length):
    pos = jnp.arange(length, dtype=F32)
    inv_freq = ROPE_BASE ** (-jnp.arange(0, ROPE_DIM, 2, dtype=F32) / ROPE_DIM)
    ang = pos[:, None] * inv_freq[None, :]
    cos, sin = jnp.cos(ang), jnp.sin(ang)
    return jnp.tile(cos, (1, 4)), jnp.tile(jnp.concatenate([-sin, sin], axis=-1), (1, 2))


def kernel(x_prompt, x_sample, c_prompt, c_sample, w_ada, b_ada, norm_mix_g, w_in, kv_norm_g, w_ukv,
           q_norm_g, k_norm_g, lam_re, lam_im, log_dt, b_re, b_im, c_re, c_im, d_skip, w_glu, b_glu,
           att_out_g, ssm_out_g, w_o, norm_ffn_g, w1, w3, w2):
    assert w_ada.shape[0] == 1, "single-layer kernel"
    xs = (x_prompt, x_sample)
    batches = tuple(x.shape[0] for x in xs)
    length = x_prompt.shape[1]
    assert x_sample.shape[1] == length and sum(batches) <= S5_ROWS

    c_all = jnp.concatenate([c_prompt, c_sample], axis=0)
    c8 = jnp.pad(c_all, ((0, SUBLANES - c_all.shape[0]), (0, 0)))
    mod_all = ada_mod(c8, w_ada[0], b_ada[0]).reshape(SUBLANES, 6, D_MODEL)
    mods = (mod_all[:batches[0]], mod_all[batches[0]:batches[0] + batches[1]])

    w_all = prep_w_in(w_in[0])
    w_ukv_p = prep_w_ukv(w_ukv[0])
    row = lambda a: a.reshape(1, -1)
    q_gn, k_gn = row(q_norm_g[0][:NOPE_DIM]), row(k_norm_g[0][:NOPE_DIM])
    q_gr, k_gr = _rope_gain(q_norm_g[0]), _rope_gain(k_norm_g[0])
    cos_t, sin_t = _rope_tables(length)
    s5_ops = s5_matrices(lam_re[0], lam_im[0], log_dt[0], b_re[0], b_im[0], c_re[0], c_im[0])
    w_glu_b, w_o_b = cast_bf16(w_glu[0]), cast_bf16(w_o[0])
    w2_b = cast_bf16(w2[0])

    proj = [in_proj(x, m, row(norm_mix_g[0]), w_all, w_ukv_p, row(kv_norm_g[0]), q_gn, k_gn, q_gr, k_gr,
                    cos_t, sin_t) for x, m in zip(xs, mods)]
    atts = [attention(q, k, v) for q, k, v, _, _ in proj]
    y_rows = s5_scan([p[4] for p in proj], *s5_ops)

    outs = []
    for x, m, att, ys, p in zip(xs, mods, atts, y_rows, proj):
        x1, h2 = post_mix(att, ys, p[3], x, m, row(d_skip[0]), w_glu_b, row(b_glu[0]), row(att_out_g[0]),
                          row(ssm_out_g[0]), w_o_b, row(norm_ffn_g[0]))
        outs.append(ffn_down(ffn_up(h2, w1[0], w3[0]), x1, m, w2_b))
    return tuple(outs)
```

```python
import functools
import math

import jax
import jax.numpy as jnp
import numpy as np
from jax import lax
from jax.experimental import pallas as pl
from jax.experimental.pallas import tpu as pltpu

F32 = jnp.float32
BF16 = jnp.bfloat16

D_MODEL = 2048
ATT_WIDTH = 1024
SSM_WIDTH = 1024
N_HEADS = 8
V_DIM = 128
NOPE_DIM = 128
ROPE_DIM = 64
QK_DIM = NOPE_DIM + ROPE_DIM
KV_RANK = 512
ROPE_BASE = 10000.0
SSM_GROUP = 16
N_GROUPS = SSM_WIDTH // SSM_GROUP
SSM_STATE = 64
D_FF = 5632
NORM_EPS = 1e-6

LANES = 128
SUBLANES = 8
VMEM_LIMIT = 56 * 1024 * 1024
CAST_BLOCK_BYTES = 4 * 1024 * 1024

ADA_TN = 1024
ATT_TQ, ATT_TK = 1024, 256
FFN_UP_TM, FFN_UP_TF = 1024, 512
FFN_DOWN_TM, FFN_DOWN_TN = 512, 1024
W_PREP_ROWS = 256

C_QN = 0
C_QR = C_QN + N_HEADS * NOPE_DIM
C_KV = C_QR + N_HEADS * ROPE_DIM
C_KR = C_KV + KV_RANK
C_U = C_KR + 2 * ROPE_DIM
C_END = C_U + SSM_WIDTH

S5_CHUNK = 32
S5_ROWS = SUBLANES
S5_SUB = SUBLANES
S5_TILE = S5_SUB * S5_CHUNK
S5_COLS = S5_CHUNK * SSM_GROUP
S5_STATE_COLS = 4 * SSM_STATE
GROUPS_PER_COL = LANES // SSM_GROUP


def _params(sem):
    return pltpu.CompilerParams(dimension_semantics=sem, vmem_limit_bytes=VMEM_LIMIT)


def _dot(a, b):
    return jnp.dot(a, b, preferred_element_type=F32)


def _dot_f32(a, b):
    a_hi = a.astype(BF16)
    a_lo = (a - a_hi.astype(F32)).astype(BF16)
    b_hi = b.astype(BF16)
    b_lo = (b - b_hi.astype(F32)).astype(BF16)
    return _dot(a_hi, b_hi) + _dot(a_lo, b_hi) + _dot(a_hi, b_lo)


def _ada_kernel(c_ref, w_ref, b_ref, o_ref):
    c = c_ref[...]
    o_ref[...] = _dot_f32(c * jax.nn.sigmoid(c), w_ref[...]) + b_ref[...]


def ada_mod(c8, w_ada, b_ada):
    rows, d = c8.shape
    n = w_ada.shape[1]
    tn = ADA_TN
    return pl.pallas_call(
        _ada_kernel,
        out_shape=jax.ShapeDtypeStruct((rows, n), F32),
        grid=(n // tn,),
        in_specs=[
            pl.BlockSpec((rows, d), lambda j: (0, 0)),
            pl.BlockSpec((d, tn), lambda j: (0, j)),
            pl.BlockSpec((1, tn), lambda j: (0, j)),
        ],
        out_specs=pl.BlockSpec((rows, tn), lambda j: (0, j)),
        compiler_params=_params(("arbitrary",)),
        name="ada_mod",
    )(c8, w_ada, b_ada.reshape(1, n))


def _rms(x, width):
    return lax.rsqrt(jnp.sum(x * x, axis=-1, keepdims=True) * (1.0 / width) + NORM_EPS)


def _granule_masks():
    pos = lax.broadcasted_iota(jnp.int32, (SUBLANES, LANES), 1) // SSM_GROUP
    return [pos == p for p in range(GROUPS_PER_COL)]


def _to_chunk_rows(u_cols, dst):
    masks = _granule_masks()
    for j in range(SSM_WIDTH // LANES):
        rolled = []
        for t in range(S5_CHUNK):
            v = u_cols[j, pl.ds(t, S5_SUB, stride=S5_CHUNK), :]
            r = t % GROUPS_PER_COL
            rolled.append(pltpu.roll(v, SSM_GROUP * r, axis=1) if r else v)
        for k in range(GROUPS_PER_COL):
            for d in range(S5_COLS // LANES):
                acc = rolled[GROUPS_PER_COL * d + GROUPS_PER_COL - 1]
                for rho in range(GROUPS_PER_COL - 2, -1, -1):
                    acc = jnp.where(masks[(k + rho) % GROUPS_PER_COL], rolled[GROUPS_PER_COL * d + rho], acc)
                dst[GROUPS_PER_COL * j + k, :, d * LANES:(d + 1) * LANES] = acc


def _from_chunk_rows(src, y_cols):
    masks = _granule_masks()
    for j in range(SSM_WIDTH // LANES):
        for d in range(S5_COLS // LANES):
            cols = [src[GROUPS_PER_COL * j + k, :, d * LANES:(d + 1) * LANES] for k in range(GROUPS_PER_COL)]
            for rho in range(GROUPS_PER_COL):
                acc = cols[GROUPS_PER_COL - 1]
                for k in range(GROUPS_PER_COL - 2, -1, -1):
                    acc = jnp.where(masks[(k + rho) % GROUPS_PER_COL], cols[k], acc)
                v = pltpu.roll(acc, LANES - SSM_GROUP * rho, axis=1) if rho else acc
                y_cols[j, pl.ds(GROUPS_PER_COL * d + rho, S5_SUB, stride=S5_CHUNK), :] = v


def _in_proj_kernel(x_ref, mod_ref, g1_ref, w_ref, wukv_ref, kvg_ref, qg_ref, kg_ref,
                    qgr_ref, kgr_ref, cos_ref, sin_ref, q_ref, k_ref, v_ref, u_ref, ur_ref, u_cols,
                    *, q_scale):
    x = x_ref[0]
    shift = mod_ref[0, 0:1, :]
    scale = mod_ref[0, 1:2, :]
    h = (x * _rms(x, D_MODEL) * g1_ref[...]) * (1.0 + scale) + shift
    h = h.astype(BF16)

    cos = cos_ref[...]
    sin = sin_ref[...]

    qn = _dot(h, w_ref[:, C_QN:C_QR])
    qr = _dot(h, w_ref[:, C_QR:C_KV])
    qg_n = qg_ref[...]
    qg_r = qgr_ref[...]
    lane = lax.broadcasted_iota(jnp.int32, (x.shape[0], LANES), 1)
    low_half = (lane & (ROPE_DIM - 1)) < ROPE_DIM // 2
    even_head = lane < ROPE_DIM
    for pair in range(N_HEADS // 2):
        v = qr[:, pair * LANES:(pair + 1) * LANES]
        cg = v * qg_r
        swapped = jnp.where(low_half, pltpu.roll(cg, LANES - ROPE_DIM // 2, axis=1),
                            pltpu.roll(cg, ROPE_DIM // 2, axis=1))
        rot = cg * cos + swapped * sin
        sq = v * v
        ss_even = jnp.sum(jnp.where(even_head, sq, 0.0), axis=-1, keepdims=True)
        ss_odd = jnp.sum(sq, axis=-1, keepdims=True) - ss_even
        for odd, ss_r in ((0, ss_even), (1, ss_odd)):
            hd = 2 * pair + odd
            qn_h = qn[:, hd * LANES:(hd + 1) * LANES]
            ss = jnp.sum(qn_h * qn_h, axis=-1, keepdims=True) + ss_r
            r = lax.rsqrt(ss * (1.0 / QK_DIM) + NORM_EPS) * q_scale
            q_ref[0, hd, :, 0:NOPE_DIM] = (qn_h * r * qg_n).astype(BF16)
            rot_h = pltpu.roll(rot, ROPE_DIM, axis=1) if odd else rot
            q_ref[0, hd, :, NOPE_DIM:QK_DIM] = (rot_h * r)[:, 0:ROPE_DIM].astype(BF16)

    ckv = _dot(h, w_ref[:, C_KV:C_KR])
    ckv_n = (ckv * _rms(ckv, KV_RANK) * kvg_ref[...]).astype(BF16)
    kv = _dot(ckv_n, wukv_ref[...])
    kr = _dot(h, w_ref[:, C_KR:C_U])
    kr_ss = 0.5 * jnp.sum(kr * kr, axis=-1, keepdims=True)
    kr_g = kr * kgr_ref[...]
    kr_rot = kr_g * cos + pltpu.roll(kr_g, ROPE_DIM, axis=1) * sin
    kg_n = kg_ref[...]
    ones = jnp.ones((x.shape[0], LANES), BF16)
    for hd in range(N_HEADS):
        kn_h = kv[:, hd * LANES:(hd + 1) * LANES]
        ss = jnp.sum(kn_h * kn_h, axis=-1, keepdims=True) + kr_ss
        r = lax.rsqrt(ss * (1.0 / QK_DIM) + NORM_EPS)
        k_ref[0, hd, :, 0:NOPE_DIM] = (kn_h * r * kg_n).astype(BF16)
        k_ref[0, hd, :, NOPE_DIM:QK_DIM] = (kr_rot * r)[:, 0:ROPE_DIM].astype(BF16)
        v_h = kv[:, ATT_WIDTH + hd * LANES:ATT_WIDTH + (hd + 1) * LANES]
        v_ref[0, hd, :, 0:V_DIM] = v_h.astype(BF16)
        v_ref[0, hd, :, V_DIM:2 * V_DIM] = ones

    u = _dot(h, w_ref[:, C_U:C_END])
    u_ref[0] = u
    for j in range(SSM_WIDTH // LANES):
        u_cols[j] = u[:, j * LANES:(j + 1) * LANES]
    _to_chunk_rows(u_cols, ur_ref.at[:, 0])


def in_proj(x, mod, g1, w_all, w_ukv, kv_g, q_gn, k_gn, q_gr, k_gr, cos_t, sin_t):
    b, length, d = x.shape
    tm = S5_TILE
    assert length % tm == 0
    const = lambda bi, i: (0, 0)
    kern = functools.partial(_in_proj_kernel, q_scale=math.log2(math.e) / math.sqrt(QK_DIM))
    return pl.pallas_call(
        kern,
        out_shape=(
            jax.ShapeDtypeStruct((b, N_HEADS, length, QK_DIM), BF16),
            jax.ShapeDtypeStruct((b, N_HEADS, length, QK_DIM), BF16),
            jax.ShapeDtypeStruct((b, N_HEADS, length, 2 * V_DIM), BF16),
            jax.ShapeDtypeStruct((b, length, SSM_WIDTH), F32),
            jax.ShapeDtypeStruct((N_GROUPS, b, length // S5_CHUNK, S5_COLS), F32),
        ),
        grid=(b, length // tm),
        in_specs=[
            pl.BlockSpec((1, tm, d), lambda bi, i: (bi, i, 0)),
            pl.BlockSpec((1, 6, d), lambda bi, i: (bi, 0, 0)),
            pl.BlockSpec((1, d), const),
            pl.BlockSpec((d, C_END), const),
            pl.BlockSpec((KV_RANK, 2 * ATT_WIDTH), const),
            pl.BlockSpec((1, KV_RANK), const),
            pl.BlockSpec((1, LANES), const),
            pl.BlockSpec((1, LANES), const),
            pl.BlockSpec((1, LANES), const),
            pl.BlockSpec((1, LANES), const),
            pl.BlockSpec((tm, LANES), lambda bi, i: (i, 0)),
            pl.BlockSpec((tm, LANES), lambda bi, i: (i, 0)),
        ],
        out_specs=(
            pl.BlockSpec((1, N_HEADS, tm, QK_DIM), lambda bi, i: (bi, 0, i, 0)),
            pl.BlockSpec((1, N_HEADS, tm, QK_DIM), lambda bi, i: (bi, 0, i, 0)),
            pl.BlockSpec((1, N_HEADS, tm, 2 * V_DIM), lambda bi, i: (bi, 0, i, 0)),
            pl.BlockSpec((1, tm, SSM_WIDTH), lambda bi, i: (bi, i, 0)),
            pl.BlockSpec((N_GROUPS, 1, S5_SUB, S5_COLS), lambda bi, i: (0, bi, i, 0)),
        ),
        scratch_shapes=[pltpu.VMEM((SSM_WIDTH // LANES, tm, LANES), F32)],
        compiler_params=_params(("parallel", "parallel")),
        name="in_proj",
    )(x, mod, g1, w_all, w_ukv, kv_g, q_gn, k_gn, q_gr, k_gr, cos_t, sin_t)


def _attn_kernel(q_ref, k_ref, v_ref, o_ref, *, tk):
    q = q_ref[0, 0]
    tq = q.shape[0]
    m = jnp.full((tq, 1), -jnp.inf, F32)
    acc = jnp.zeros((tq, 2 * V_DIM), F32)
    for j in range(k_ref.shape[2] // tk):
        kc = k_ref[0, 0, j * tk:(j + 1) * tk, :]
        vc = v_ref[0, 0, j * tk:(j + 1) * tk, :]
        s = lax.dot_general(q, kc, (((1,), (1,)), ((), ())), preferred_element_type=F32)
        m_new = jnp.maximum(m, jnp.max(s, axis=-1, keepdims=True))
        alpha = jnp.exp2(m - m_new)
        p = jnp.exp2(s - m_new).astype(BF16)
        acc = alpha * acc + _dot(p, vc)
        m = m_new
    o_ref[0] = (acc[:, 0:V_DIM] / acc[:, V_DIM:2 * V_DIM]).astype(o_ref.dtype)


def attention(q, k, v):
    b, h, length, _ = q.shape
    tq = min(ATT_TQ, length)
    tk = min(ATT_TK, length)
    return pl.pallas_call(
        functools.partial(_attn_kernel, tk=tk),
        out_shape=jax.ShapeDtypeStruct((b, length, h * V_DIM), BF16),
        grid=(b, h, length // tq),
        in_specs=[
            pl.BlockSpec((1, 1, tq, QK_DIM), lambda bi, hi, i: (bi, hi, i, 0)),
            pl.BlockSpec((1, 1, length, QK_DIM), lambda bi, hi, i: (bi, hi, 0, 0)),
            pl.BlockSpec((1, 1, length, 2 * V_DIM), lambda bi, hi, i: (bi, hi, 0, 0)),
        ],
        out_specs=pl.BlockSpec((1, tq, V_DIM), lambda bi, hi, i: (bi, i, hi)),
        compiler_params=_params(("parallel", "parallel", "parallel")),
        name="attention",
    )(q, k, v)


def _s5_steps():
    g = np.arange(N_GROUPS)[:, None] % GROUPS_PER_COL
    slot = np.arange(S5_CHUNK)[None, :]
    return GROUPS_PER_COL * (slot // GROUPS_PER_COL) + (slot % GROUPS_PER_COL - g) % GROUPS_PER_COL


def s5_matrices(lam_re, lam_im, log_dt, b_re, b_im, c_re, c_im):
    t = S5_CHUNK
    dt = jnp.exp(log_dt)[..., None]
    la, lb = lam_re * dt, lam_im * dt
    lb_re, lb_im = jnp.exp(la) * jnp.cos(lb), jnp.exp(la) * jnp.sin(lb)
    den = lam_re * lam_re + lam_im * lam_im
    n_re, n_im = lb_re - 1.0, lb_im
    cf_re = (n_re * lam_re + n_im * lam_im) / den
    cf_im = (n_im * lam_re - n_re * lam_im) / den
    bb_re = cf_re[..., None] * b_re[None] - cf_im[..., None] * b_im[None]
    bb_im = cf_re[..., None] * b_im[None] + cf_im[..., None] * b_re[None]

    steps = _s5_steps().astype(np.float32)

    def power(d, expo):
        e = jnp.asarray(expo)[:, None, :]
        mag = jnp.exp(la[d][..., None] * e)
        return mag * jnp.cos(lb[d][..., None] * e), mag * jnp.sin(lb[d][..., None] * e)

    cat = jnp.concatenate

    def rows_op(pw, xs1, xs2):
        n = len(xs1)
        pr = cat([jnp.swapaxes(p[0], 1, 2) for p in pw], axis=-1)[:, :, None, :]
        pi = cat([jnp.swapaxes(p[1], 1, 2) for p in pw], axis=-1)[:, :, None, :]
        x1 = cat([jnp.swapaxes(x, 1, 2) for x in xs1], axis=-1)[:, None]
        x2 = cat([jnp.swapaxes(x, 1, 2) for x in xs2], axis=-1)[:, None]
        return (pr * x1 + pi * x2).reshape(N_GROUPS, S5_COLS, n * SSM_STATE)

    def cols_op(pw, xs1, xs2):
        pr = jnp.repeat(cat([p[0] for p in pw], axis=1), SSM_GROUP, axis=2)
        pi = jnp.repeat(cat([p[1] for p in pw], axis=1), SSM_GROUP, axis=2)
        x1 = jnp.tile(cat(xs1, axis=1), (1, 1, S5_CHUNK))
        x2 = jnp.tile(cat(xs2, axis=1), (1, 1, S5_CHUNK))
        return pr * x1 + pi * x2

    bf_re, bf_im, bb_re_, bb_im_ = bb_re[0], bb_im[0], bb_re[1], bb_im[1]
    cf_re_, cf_im_, cb_re, cb_im = c_re[0], c_im[0], c_re[1], c_im[1]
    tr = lambda x: jnp.swapaxes(x, 1, 2)
    p_sb, p_sf = power(1, steps), power(0, t - 1 - steps)
    p_of, p_ob = power(0, steps + 1), power(1, t - steps)

    lags = np.broadcast_to(np.arange(t, dtype=np.float32), (N_GROUPS, t))
    p_kf, p_kb = power(0, lags), power(1, t - 1 - lags)
    cl_f = cols_op([p_kf, p_kf], [tr(cf_re_), tr(cf_im_)], [-tr(cf_im_), tr(cf_re_)])
    cl_b = cols_op([p_kb, p_kb], [tr(cb_re), tr(cb_im)], [-tr(cb_im), tr(cb_re)])
    bt_f = cat([tr(bf_re), -tr(bf_im)], axis=-1)
    bt_b = cat([tr(bb_re_), -tr(bb_im_)], axis=-1)
    m_st = rows_op([p_sf, p_sb, p_sf, p_sb], [bf_re, bb_re_, bf_im, bb_im_],
                   [-bf_im, -bb_im_, bf_re, bb_re_]).astype(BF16)
    m_out = cols_op([p_of, p_ob, p_of, p_ob], [tr(cf_re_), tr(cb_re), -tr(cf_im_), -tr(cb_im)],
                    [-tr(cf_im_), -tr(cb_im), -tr(cf_re_), -tr(cb_re)]).astype(BF16)
    a_re = cat([jnp.exp(t * la[0]) * jnp.cos(t * lb[0]), jnp.exp(t * la[1]) * jnp.cos(t * lb[1])], axis=-1)[:, None, :]
    a_im = cat([jnp.exp(t * la[0]) * jnp.sin(t * lb[0]), jnp.exp(t * la[1]) * jnp.sin(t * lb[1])], axis=-1)[:, None, :]
    return bt_f, cl_f, bt_b, cl_b, m_st, m_out, a_re, a_im


def _s5_kernel(*refs, batches):
    n_in = len(batches)
    u_refs = refs[:n_in]
    btf_ref, clf_ref, btb_ref, clb_ref, mst_ref, mout_ref, are_ref, aim_ref = refs[n_in:n_in + 8]
    y_refs = refs[n_in + 8:2 * n_in + 8]
    st_ref, xin_ref, tnat_ref = refs[2 * n_in + 8:]
    n_chunks = u_refs[0].shape[2]

    k_f = _dot_f32(btf_ref[0], clf_ref[0])
    k_b = _dot_f32(btb_ref[0], clb_ref[0])
    lane = lax.broadcasted_iota(jnp.int32, (SSM_GROUP, S5_COLS), 1)
    for s in range(S5_CHUNK):
        lo, hi = s * SSM_GROUP, (s + 1) * SSM_GROUP
        fwd = jnp.where(lane >= lo, pltpu.roll(k_f, lo, axis=1) if lo else k_f, 0.0)
        bwd = jnp.where(lane < hi, pltpu.roll(k_b, hi, axis=1) if hi < S5_COLS else k_b, 0.0)
        tnat_ref[lo:hi, :] = (fwd + bwd).astype(BF16)

    def natural_of(idx):
        k = pl.program_id(0) % GROUPS_PER_COL
        d = lax.shift_right_logical(idx, LANES.bit_length() - 1)
        pos = lax.shift_right_logical(idx, SSM_GROUP.bit_length() - 1) & (GROUPS_PER_COL - 1)
        step = d * GROUPS_PER_COL + ((pos - k) & (GROUPS_PER_COL - 1))
        return step * SSM_GROUP + (idx & (SSM_GROUP - 1))

    r_idx = lax.broadcasted_iota(jnp.int32, (S5_COLS, S5_COLS), 0)
    c_idx = lax.broadcasted_iota(jnp.int32, (S5_COLS, S5_COLS), 1)
    pick_rows = jnp.where(natural_of(r_idx) == c_idx, 1.0, 0.0).astype(BF16)
    pick_cols = jnp.where(r_idx == natural_of(c_idx), 1.0, 0.0).astype(BF16)
    toep = _dot(_dot(pick_rows, tnat_ref[...]).astype(BF16), pick_cols).astype(BF16)
    seqs = [(u_ref, y_ref, b) for u_ref, y_ref, nb in zip(u_refs, y_refs, batches) for b in range(nb)]

    def u_rows(u_ref, b):
        return u_ref[0, b].astype(BF16)

    for slot in range(S5_ROWS):
        rows = pl.ds(slot, n_chunks, stride=S5_ROWS)
        if slot < len(seqs):
            st = _dot(u_rows(seqs[slot][0], seqs[slot][2]), mst_ref[0])
        else:
            st = jnp.zeros((n_chunks, S5_STATE_COLS), F32)
        st_ref[0, rows, :] = st[:, 0:LANES]
        st_ref[1, rows, :] = st[:, LANES:2 * LANES]

    a_re = jnp.broadcast_to(are_ref[0], (S5_ROWS, LANES))
    a_im = jnp.broadcast_to(aim_ref[0], (S5_ROWS, LANES))
    is_fwd = lax.broadcasted_iota(jnp.int32, (S5_ROWS, LANES), 1) < SSM_STATE
    half = SSM_STATE

    def body(i, carry):
        x_re, x_im = carry
        rf = pl.multiple_of(i * S5_ROWS, S5_ROWS)
        rb = pl.multiple_of((n_chunks - 1 - i) * S5_ROWS, S5_ROWS)
        xin_ref[0, pl.ds(rf, S5_ROWS), 0:half] = x_re[:, 0:half]
        xin_ref[0, pl.ds(rb, S5_ROWS), half:LANES] = x_re[:, half:LANES]
        xin_ref[1, pl.ds(rf, S5_ROWS), 0:half] = x_im[:, 0:half]
        xin_ref[1, pl.ds(rb, S5_ROWS), half:LANES] = x_im[:, half:LANES]
        s_re = jnp.where(is_fwd, st_ref[0, pl.ds(rf, S5_ROWS), :], st_ref[0, pl.ds(rb, S5_ROWS), :])
        s_im = jnp.where(is_fwd, st_ref[1, pl.ds(rf, S5_ROWS), :], st_ref[1, pl.ds(rb, S5_ROWS), :])
        n_re = a_re * x_re - a_im * x_im + s_re
        n_im = a_re * x_im + a_im * x_re + s_im
        return n_re, n_im

    zero = jnp.zeros((S5_ROWS, LANES), F32)
    lax.fori_loop(0, n_chunks, body, (zero, zero), unroll=4)

    for slot, (u_ref, y_ref, b) in enumerate(seqs):
        rows = pl.ds(slot, n_chunks, stride=S5_ROWS)
        x_in = jnp.concatenate([xin_ref[0, rows, :], xin_ref[1, rows, :]], axis=1).astype(BF16)
        y_ref[0, b] = _dot(u_rows(u_ref, b), toep) + _dot(x_in, mout_ref[0])


def s5_scan(u_rows, bt_f, cl_f, bt_b, cl_b, m_st, m_out, a_re, a_im):
    g, _, n_chunks, _ = u_rows[0].shape
    batches = tuple(u.shape[1] for u in u_rows)
    assert sum(batches) <= S5_ROWS
    gmap = lambda gi: (gi, 0, 0)
    io_specs = [pl.BlockSpec((1, nb, n_chunks, S5_COLS), lambda gi: (gi, 0, 0, 0)) for nb in batches]
    return pl.pallas_call(
        functools.partial(_s5_kernel, batches=batches),
        out_shape=tuple(jax.ShapeDtypeStruct(u.shape, F32) for u in u_rows),
        grid=(g,),
        in_specs=io_specs + [
            pl.BlockSpec((1, SSM_GROUP, 2 * SSM_STATE), gmap),
            pl.BlockSpec((1, 2 * SSM_STATE, S5_COLS), gmap),
            pl.BlockSpec((1, SSM_GROUP, 2 * SSM_STATE), gmap),
            pl.BlockSpec((1, 2 * SSM_STATE, S5_COLS), gmap),
            pl.BlockSpec((1, S5_COLS, S5_STATE_COLS), gmap),
            pl.BlockSpec((1, S5_STATE_COLS, S5_COLS), gmap),
            pl.BlockSpec((1, 1, LANES), gmap),
            pl.BlockSpec((1, 1, LANES), gmap),
        ],
        out_specs=tuple(io_specs),
        scratch_shapes=[pltpu.VMEM((2, n_chunks * S5_ROWS, LANES), F32),
                        pltpu.VMEM((2, n_chunks * S5_ROWS, LANES), F32),
                        pltpu.VMEM((S5_COLS, S5_COLS), BF16)],
        compiler_params=_params(("parallel",)),
        name="s5_scan",
    )(*u_rows, bt_f, cl_f, bt_b, cl_b, m_st, m_out, a_re, a_im)


def _gelu_tanh(x):
    return 0.5 * x * (1.0 + jnp.tanh(math.sqrt(2.0 / math.pi) * (x + 0.044715 * (x * x * x))))


def _post_mix_kernel(att_ref, ys_ref, u_ref, x_ref, mod_ref, dsk_ref, wglu_ref, bglu_ref,
                     ag_ref, sg_ref, wo_ref, g2_ref, x1_ref, h2_ref, y_cols):
    gate1 = mod_ref[0, 2:3, :]
    shift2 = mod_ref[0, 3:4, :]
    scale2 = mod_ref[0, 4:5, :]
    _from_chunk_rows(ys_ref.at[:, 0], y_cols)
    y = jnp.concatenate([y_cols[j] for j in range(SSM_WIDTH // LANES)], axis=1)
    y = y + dsk_ref[...] * u_ref[0]
    y = _gelu_tanh(y)
    z = _dot(y.astype(BF16), wglu_ref[...]) + bglu_ref[...]
    ssm = y * jax.nn.sigmoid(z)
    att = att_ref[0].astype(F32)
    att_n = (att * _rms(att, ATT_WIDTH) * ag_ref[...]).astype(BF16)
    ssm_n = (ssm * _rms(ssm, SSM_WIDTH) * sg_ref[...]).astype(BF16)
    mix = _dot(att_n, wo_ref[0:ATT_WIDTH, :]) + _dot(ssm_n, wo_ref[ATT_WIDTH:2 * ATT_WIDTH, :])
    x1 = x_ref[0] + gate1 * mix
    x1_ref[0] = x1
    h2 = (x1 * _rms(x1, D_MODEL) * g2_ref[...]) * (1.0 + scale2) + shift2
    h2_ref[0] = h2.astype(BF16)


def post_mix(att, ys, u, x, mod, d_skip, w_glu, b_glu, att_g, ssm_g, w_o, g2):
    b, length, d = x.shape
    tm = S5_TILE
    const = lambda bi, i: (0, 0)
    tile = lambda w: pl.BlockSpec((1, tm, w), lambda bi, i: (bi, i, 0))
    return pl.pallas_call(
        _post_mix_kernel,
        out_shape=(jax.ShapeDtypeStruct((b, length, d), F32),
                   jax.ShapeDtypeStruct((b, length, d), BF16)),
        grid=(b, length // tm),
        in_specs=[
            tile(ATT_WIDTH),
            pl.BlockSpec((N_GROUPS, 1, S5_SUB, S5_COLS), lambda bi, i: (0, bi, i, 0)),
            tile(SSM_WIDTH), tile(d),
            pl.BlockSpec((1, 6, d), lambda bi, i: (bi, 0, 0)),
            pl.BlockSpec((1, SSM_WIDTH), const),
            pl.BlockSpec((SSM_WIDTH, SSM_WIDTH), const),
            pl.BlockSpec((1, SSM_WIDTH), const),
            pl.BlockSpec((1, ATT_WIDTH), const),
            pl.BlockSpec((1, SSM_WIDTH), const),
            pl.BlockSpec((d, d), const),
            pl.BlockSpec((1, d), const),
        ],
        out_specs=(tile(d), tile(d)),
        scratch_shapes=[pltpu.VMEM((SSM_WIDTH // LANES, tm, LANES), F32)],
        compiler_params=_params(("parallel", "parallel")),
        name="post_mix",
    )(att, ys, u, x, mod, d_skip, w_glu, b_glu, att_g, ssm_g, w_o, g2)


def _ffn_up_kernel(h_ref, w1_ref, w3_ref, o_ref, w1_bf, w3_bf):
    @pl.when(jnp.logical_and(pl.program_id(1) == 0, pl.program_id(2) == 0))
    def _():
        w1_bf[...] = w1_ref[...].astype(BF16)
        w3_bf[...] = w3_ref[...].astype(BF16)

    h = h_ref[0]
    a = _dot(h, w1_bf[...])
    g = _dot(h, w3_bf[...])
    o_ref[0] = (a * jax.nn.sigmoid(a) * g).astype(BF16)


def ffn_up(h2, w1, w3):
    b, length, d = h2.shape
    tm, tf = min(FFN_UP_TM, length), FFN_UP_TF
    return pl.pallas_call(
        _ffn_up_kernel,
        out_shape=jax.ShapeDtypeStruct((b, length, D_FF), BF16),
        grid=(D_FF // tf, b, length // tm),
        in_specs=[
            pl.BlockSpec((1, tm, d), lambda f, bi, i: (bi, i, 0)),
            pl.BlockSpec((d, tf), lambda f, bi, i: (0, f)),
            pl.BlockSpec((d, tf), lambda f, bi, i: (0, f)),
        ],
        out_specs=pl.BlockSpec((1, tm, tf), lambda f, bi, i: (bi, i, f)),
        scratch_shapes=[pltpu.VMEM((d, tf), BF16), pltpu.VMEM((d, tf), BF16)],
        compiler_params=_params(("parallel", "arbitrary", "arbitrary")),
        name="ffn_up",
    )(h2, w1, w3)


def _ffn_down_kernel(a_ref, x1_ref, mod_ref, w2_ref, o_ref):
    o_ref[0] = x1_ref[0] + mod_ref[0, 5:6, :] * _dot(a_ref[0], w2_ref[...])


def ffn_down(act, x1, mod, w2):
    b, length, d = x1.shape
    tm, tn = min(FFN_DOWN_TM, length), FFN_DOWN_TN
    return pl.pallas_call(
        _ffn_down_kernel,
        out_shape=jax.ShapeDtypeStruct((b, length, d), F32),
        grid=(d // tn, b, length // tm),
        in_specs=[
            pl.BlockSpec((1, tm, D_FF), lambda n, bi, i: (bi, i, 0)),
            pl.BlockSpec((1, tm, tn), lambda n, bi, i: (bi, i, n)),
            pl.BlockSpec((1, 6, tn), lambda n, bi, i: (bi, 0, n)),
            pl.BlockSpec((D_FF, tn), lambda n, bi, i: (0, n)),
        ],
        out_specs=pl.BlockSpec((1, tm, tn), lambda n, bi, i: (bi, i, n)),
        compiler_params=_params(("parallel", "parallel", "parallel")),
        name="ffn_down",
    )(act, x1, mod, w2)


def _cast_kernel(x_ref, o_ref):
    o_ref[...] = x_ref[...].astype(o_ref.dtype)


def cast_bf16(w):
    rows, cols = w.shape
    tr = rows
    while tr * cols * 4 > CAST_BLOCK_BYTES and tr % (4 * SUBLANES) == 0:
        tr //= 2
    return pl.pallas_call(
        _cast_kernel,
        out_shape=jax.ShapeDtypeStruct((rows, cols), BF16),
        grid=(rows // tr,),
        in_specs=[pl.BlockSpec((tr, cols), lambda i: (i, 0))],
        out_specs=pl.BlockSpec((tr, cols), lambda i: (i, 0)),
        compiler_params=_params(("parallel",)),
        name="cast_bf16",
    )(w)


def _swap_halves(a, axis=-1):
    lo, hi = jnp.split(a, 2, axis=axis)
    return jnp.concatenate([hi, lo], axis=axis)


def _prep_w_in_kernel(w_ref, o_ref):
    w = w_ref[...]
    q_cols = N_HEADS * QK_DIM
    half = ROPE_DIM // 2

    def with_swapped(r):
        return [r, r[:, half:], r[:, :half]]

    cols = [w[:, hd * QK_DIM:hd * QK_DIM + NOPE_DIM] for hd in range(N_HEADS)]
    cols += [w[:, hd * QK_DIM + NOPE_DIM:(hd + 1) * QK_DIM] for hd in range(N_HEADS)]
    cols.append(w[:, q_cols:q_cols + KV_RANK])
    cols += with_swapped(w[:, q_cols + KV_RANK:q_cols + KV_RANK + ROPE_DIM])
    cols.append(w[:, q_cols + KV_RANK + ROPE_DIM:])
    o_ref[...] = jnp.concatenate(cols, axis=1).astype(BF16)


def prep_w_in(w_in):
    d, n = w_in.shape
    return pl.pallas_call(
        _prep_w_in_kernel,
        out_shape=jax.ShapeDtypeStruct((d, C_END), BF16),
        grid=(d // W_PREP_ROWS,),
        in_specs=[pl.BlockSpec((W_PREP_ROWS, n), lambda i: (i, 0))],
        out_specs=pl.BlockSpec((W_PREP_ROWS, C_END), lambda i: (i, 0)),
        compiler_params=_params(("parallel",)),
        name="prep_w_in",
    )(w_in)


def prep_w_ukv(w_ukv):
    rank, n = w_ukv.shape
    src = lambda j: (0, 2 * (j % N_HEADS) + j // N_HEADS)
    return pl.pallas_call(
        _cast_kernel,
        out_shape=jax.ShapeDtypeStruct((rank, n), BF16),
        grid=(n // LANES,),
        in_specs=[pl.BlockSpec((rank, LANES), src)],
        out_specs=pl.BlockSpec((rank, LANES), lambda j: (0, j)),
        compiler_params=_params(("parallel",)),
        name="prep_w_ukv",
    )(w_ukv)


def _rope_gain(g):
    gr = g[NOPE_DIM:]
    return jnp.concatenate([gr, _swap_halves(gr)]).reshape(1, 2 * ROPE_DIM)


def _rope_tables(length):
    pos = jnp.arange(length, dtype=F32)
    inv_freq = ROPE_BASE ** (-jnp.arange(0, ROPE_DIM, 2, dtype=F32) / ROPE_DIM)
    ang = pos[:, None] * inv_freq[None, :]
    cos, sin = jnp.cos(ang), jnp.sin(ang)
    return jnp.tile(cos, (1, 4)), jnp.tile(jnp.concatenate([-sin, sin], axis=-1), (1, 2))


def kernel(x_prompt, x_sample, c_prompt, c_sample, w_ada, b_ada, norm_mix_g, w_in, kv_norm_g, w_ukv,
           q_norm_g, k_norm_g, lam_re, lam_im, log_dt, b_re, b_im, c_re, c_im, d_skip, w_glu, b_glu,
           att_out_g, ssm_out_g, w_o, norm_ffn_g, w1, w3, w2):
    assert w_ada.shape[0] == 1, "single-layer kernel"
    xs = (x_prompt, x_sample)
    batches = tuple(x.shape[0] for x in xs)
    length = x_prompt.shape[1]
    assert x_sample.shape[1] == length and sum(batches) <= S5_ROWS

    c_all = jnp.concatenate([c_prompt, c_sample], axis=0)
    c8 = jnp.pad(c_all, ((0, SUBLANES - c_all.shape[0]), (0, 0)))
    mod_all = ada_mod(c8, w_ada[0], b_ada[0]).reshape(SUBLANES, 6, D_MODEL)
    mods = (mod_all[:batches[0]], mod_all[batches[0]:batches[0] + batches[1]])

    w_all = prep_w_in(w_in[0])
    w_ukv_p = prep_w_ukv(w_ukv[0])
    row = lambda a: a.reshape(1, -1)
    q_gn, k_gn = row(q_norm_g[0][:NOPE_DIM]), row(k_norm_g[0][:NOPE_DIM])
    q_gr = jnp.tile(q_norm_g[0][NOPE_DIM:], LANES // ROPE_DIM).reshape(1, LANES)
    k_gr = _rope_gain(k_norm_g[0])
    cos_t, sin_t = _rope_tables(length)
    s5_ops = s5_matrices(lam_re[0], lam_im[0], log_dt[0], b_re[0], b_im[0], c_re[0], c_im[0])
    w_glu_b, w_o_b = cast_bf16(w_glu[0]), cast_bf16(w_o[0])
    w2_b = cast_bf16(w2[0])

    proj = [in_proj(x, m, row(norm_mix_g[0]), w_all, w_ukv_p, row(kv_norm_g[0]), q_gn, k_gn, q_gr, k_gr,
                    cos_t, sin_t) for x, m in zip(xs, mods)]
    atts = [attention(q, k, v) for q, k, v, _, _ in proj]
    y_rows = s5_scan([p[4] for p in proj], *s5_ops)

    outs = []
    for x, m, att, ys, p in zip(xs, mods, atts, y_rows, proj):
        x1, h2 = post_mix(att, ys, p[3], x, m, row(d_skip[0]), w_glu_b, row(b_glu[0]), row(att_out_g[0]),
                          row(ssm_out_g[0]), w_o_b, row(norm_ffn_g[0]))
        outs.append(ffn_down(ffn_up(h2, w1[0], w3[0]), x1, m, w2_b))
    return tuple(outs)
```

```python
import functools
import math

import jax
import jax.numpy as jnp
import numpy as np
from jax import lax
from jax.experimental import pallas as pl
from jax.experimental.pallas import tpu as pltpu

F32 = jnp.float32
BF16 = jnp.bfloat16

D_MODEL = 2048
ATT_WIDTH = 1024
SSM_WIDTH = 1024
N_HEADS = 8
V_DIM = 128
NOPE_DIM = 128
ROPE_DIM = 64
QK_DIM = NOPE_DIM + ROPE_DIM
KV_RANK = 512
ROPE_BASE = 10000.0
SSM_GROUP = 16
N_GROUPS = SSM_WIDTH // SSM_GROUP
SSM_STATE = 64
D_FF = 5632
NORM_EPS = 1e-6

LANES = 128
SUBLANES = 8
VMEM_LIMIT = 56 * 1024 * 1024
CAST_BLOCK_BYTES = 4 * 1024 * 1024

ADA_TN = 1024
ATT_TQ, ATT_TK = 1024, 256
FFN_UP_TM, FFN_UP_TF = 1024, 512
FFN_DOWN_TM, FFN_DOWN_TN = 512, 1024
W_PREP_ROWS = 256

C_QN = 0
C_QR = C_QN + N_HEADS * NOPE_DIM
C_KV = C_QR + N_HEADS * ROPE_DIM
C_KR = C_KV + KV_RANK
C_U = C_KR + 2 * ROPE_DIM
C_END = C_U + SSM_WIDTH

S5_CHUNK = 32
S5_ROWS = SUBLANES
S5_SUB = SUBLANES
S5_TILE = S5_SUB * S5_CHUNK
S5_COLS = S5_CHUNK * SSM_GROUP
S5_STATE_COLS = 4 * SSM_STATE
GROUPS_PER_COL = LANES // SSM_GROUP


def _params(sem):
    return pltpu.CompilerParams(dimension_semantics=sem, vmem_limit_bytes=VMEM_LIMIT)


def _dot(a, b):
    return jnp.dot(a, b, preferred_element_type=F32)


def _dot_f32(a, b):
    a_hi = a.astype(BF16)
    a_lo = (a - a_hi.astype(F32)).astype(BF16)
    b_hi = b.astype(BF16)
    b_lo = (b - b_hi.astype(F32)).astype(BF16)
    return _dot(a_hi, b_hi) + _dot(a_lo, b_hi) + _dot(a_hi, b_lo)


def _ada_kernel(c_ref, w_ref, b_ref, o_ref):
    c = c_ref[...]
    o_ref[...] = _dot_f32(c * jax.nn.sigmoid(c), w_ref[...]) + b_ref[...]


def ada_mod(c8, w_ada, b_ada):
    rows, d = c8.shape
    n = w_ada.shape[1]
    tn = ADA_TN
    return pl.pallas_call(
        _ada_kernel,
        out_shape=jax.ShapeDtypeStruct((rows, n), F32),
        grid=(n // tn,),
        in_specs=[
            pl.BlockSpec((rows, d), lambda j: (0, 0)),
            pl.BlockSpec((d, tn), lambda j: (0, j)),
            pl.BlockSpec((1, tn), lambda j: (0, j)),
        ],
        out_specs=pl.BlockSpec((rows, tn), lambda j: (0, j)),
        compiler_params=_params(("arbitrary",)),
        name="ada_mod",
    )(c8, w_ada, b_ada.reshape(1, n))


def _rms(x, width):
    return lax.rsqrt(jnp.sum(x * x, axis=-1, keepdims=True) * (1.0 / width) + NORM_EPS)


def _granule_masks():
    pos = lax.broadcasted_iota(jnp.int32, (SUBLANES, LANES), 1) // SSM_GROUP
    return [pos == p for p in range(GROUPS_PER_COL)]


def _to_chunk_rows(u_cols, dst):
    masks = _granule_masks()
    for j in range(SSM_WIDTH // LANES):
        rolled = []
        for t in range(S5_CHUNK):
            v = u_cols[j, pl.ds(t, S5_SUB, stride=S5_CHUNK), :]
            r = t % GROUPS_PER_COL
            rolled.append(pltpu.roll(v, SSM_GROUP * r, axis=1) if r else v)
        for k in range(GROUPS_PER_COL):
            for d in range(S5_COLS // LANES):
                acc = rolled[GROUPS_PER_COL * d + GROUPS_PER_COL - 1]
                for rho in range(GROUPS_PER_COL - 2, -1, -1):
                    acc = jnp.where(masks[(k + rho) % GROUPS_PER_COL], rolled[GROUPS_PER_COL * d + rho], acc)
                dst[GROUPS_PER_COL * j + k, :, d * LANES:(d + 1) * LANES] = acc


def _from_chunk_rows(src, y_cols):
    masks = _granule_masks()
    for j in range(SSM_WIDTH // LANES):
        for d in range(S5_COLS // LANES):
            cols = [src[GROUPS_PER_COL * j + k, :, d * LANES:(d + 1) * LANES] for k in range(GROUPS_PER_COL)]
            for rho in range(GROUPS_PER_COL):
                acc = cols[GROUPS_PER_COL - 1]
                for k in range(GROUPS_PER_COL - 2, -1, -1):
                    acc = jnp.where(masks[(k + rho) % GROUPS_PER_COL], cols[k], acc)
                v = pltpu.roll(acc, LANES - SSM_GROUP * rho, axis=1) if rho else acc
                y_cols[j, pl.ds(GROUPS_PER_COL * d + rho, S5_SUB, stride=S5_CHUNK), :] = v


def _in_proj_kernel(x_ref, mod_ref, g1_ref, w_ref, wukv_ref, kvg_ref, qg_ref, kg_ref,
                    qgr_ref, kgr_ref, cos_ref, sin_ref, q_ref, k_ref, v_ref, u_ref, ur_ref, u_cols,
                    *, q_scale):
    x = x_ref[0]
    shift = mod_ref[0, 0:1, :]
    scale = mod_ref[0, 1:2, :]
    h = (x * _rms(x, D_MODEL) * g1_ref[...]) * (1.0 + scale) + shift
    h = h.astype(BF16)

    cos = cos_ref[...]
    sin = sin_ref[...]

    qn = _dot(h, w_ref[:, C_QN:C_QR])
    qr = _dot(h, w_ref[:, C_QR:C_KV])
    qg_n = qg_ref[...]
    qg_r = qgr_ref[...]
    lane = lax.broadcasted_iota(jnp.int32, (x.shape[0], LANES), 1)
    low_half = (lane & (ROPE_DIM - 1)) < ROPE_DIM // 2
    even_head = lane < ROPE_DIM
    for pair in range(N_HEADS // 2):
        v = qr[:, pair * LANES:(pair + 1) * LANES]
        cg = v * qg_r
        swapped = jnp.where(low_half, pltpu.roll(cg, LANES - ROPE_DIM // 2, axis=1),
                            pltpu.roll(cg, ROPE_DIM // 2, axis=1))
        rot = cg * cos + swapped * sin
        sq = v * v
        ss_even = jnp.sum(jnp.where(even_head, sq, 0.0), axis=-1, keepdims=True)
        ss_odd = jnp.sum(sq, axis=-1, keepdims=True) - ss_even
        for odd, ss_r in ((0, ss_even), (1, ss_odd)):
            hd = 2 * pair + odd
            qn_h = qn[:, hd * LANES:(hd + 1) * LANES]
            ss = jnp.sum(qn_h * qn_h, axis=-1, keepdims=True) + ss_r
            r = lax.rsqrt(ss * (1.0 / QK_DIM) + NORM_EPS) * q_scale
            q_ref[0, hd, :, 0:NOPE_DIM] = (qn_h * r * qg_n).astype(BF16)
            rot_h = pltpu.roll(rot, ROPE_DIM, axis=1) if odd else rot
            q_ref[0, hd, :, NOPE_DIM:QK_DIM] = (rot_h * r)[:, 0:ROPE_DIM].astype(BF16)

    ckv = _dot(h, w_ref[:, C_KV:C_KR])
    ckv_n = (ckv * _rms(ckv, KV_RANK) * kvg_ref[...]).astype(BF16)
    kv = _dot(ckv_n, wukv_ref[...])
    kr = _dot(h, w_ref[:, C_KR:C_U])
    kr_ss = 0.5 * jnp.sum(kr * kr, axis=-1, keepdims=True)
    kr_g = kr * kgr_ref[...]
    kr_rot = kr_g * cos + pltpu.roll(kr_g, ROPE_DIM, axis=1) * sin
    kg_n = kg_ref[...]
    ones = jnp.ones((x.shape[0], LANES), BF16)
    for hd in range(N_HEADS):
        kn_h = kv[:, hd * LANES:(hd + 1) * LANES]
        ss = jnp.sum(kn_h * kn_h, axis=-1, keepdims=True) + kr_ss
        r = lax.rsqrt(ss * (1.0 / QK_DIM) + NORM_EPS)
        k_ref[0, hd, :, 0:NOPE_DIM] = (kn_h * r * kg_n).astype(BF16)
        k_ref[0, hd, :, NOPE_DIM:QK_DIM] = (kr_rot * r)[:, 0:ROPE_DIM].astype(BF16)
        v_h = kv[:, ATT_WIDTH + hd * LANES:ATT_WIDTH + (hd + 1) * LANES]
        v_ref[0, hd, :, 0:V_DIM] = v_h.astype(BF16)
        v_ref[0, hd, :, V_DIM:2 * V_DIM] = ones

    u = _dot(h, w_ref[:, C_U:C_END])
    u_ref[0] = u
    for j in range(SSM_WIDTH // LANES):
        u_cols[j] = u[:, j * LANES:(j + 1) * LANES]
    _to_chunk_rows(u_cols, ur_ref.at[:, 0])


def in_proj(x, mod, g1, w_all, w_ukv, kv_g, q_gn, k_gn, q_gr, k_gr, cos_t, sin_t):
    b, length, d = x.shape
    tm = S5_TILE
    assert length % tm == 0
    const = lambda bi, i: (0, 0)
    kern = functools.partial(_in_proj_kernel, q_scale=math.log2(math.e) / math.sqrt(QK_DIM))
    return pl.pallas_call(
        kern,
        out_shape=(
            jax.ShapeDtypeStruct((b, N_HEADS, length, QK_DIM), BF16),
            jax.ShapeDtypeStruct((b, N_HEADS, length, QK_DIM), BF16),
            jax.ShapeDtypeStruct((b, N_HEADS, length, 2 * V_DIM), BF16),
            jax.ShapeDtypeStruct((b, length, SSM_WIDTH), F32),
            jax.ShapeDtypeStruct((N_GROUPS, b, length // S5_CHUNK, S5_COLS), F32),
        ),
        grid=(b, length // tm),
        in_specs=[
            pl.BlockSpec((1, tm, d), lambda bi, i: (bi, i, 0)),
            pl.BlockSpec((1, 6, d), lambda bi, i: (bi, 0, 0)),
            pl.BlockSpec((1, d), const),
            pl.BlockSpec((d, C_END), const),
            pl.BlockSpec((KV_RANK, 2 * ATT_WIDTH), const),
            pl.BlockSpec((1, KV_RANK), const),
            pl.BlockSpec((1, LANES), const),
            pl.BlockSpec((1, LANES), const),
            pl.BlockSpec((1, LANES), const),
            pl.BlockSpec((1, LANES), const),
            pl.BlockSpec((tm, LANES), lambda bi, i: (i, 0)),
            pl.BlockSpec((tm, LANES), lambda bi, i: (i, 0)),
        ],
        out_specs=(
            pl.BlockSpec((1, N_HEADS, tm, QK_DIM), lambda bi, i: (bi, 0, i, 0)),
            pl.BlockSpec((1, N_HEADS, tm, QK_DIM), lambda bi, i: (bi, 0, i, 0)),
            pl.BlockSpec((1, N_HEADS, tm, 2 * V_DIM), lambda bi, i: (bi, 0, i, 0)),
            pl.BlockSpec((1, tm, SSM_WIDTH), lambda bi, i: (bi, i, 0)),
            pl.BlockSpec((N_GROUPS, 1, S5_SUB, S5_COLS), lambda bi, i: (0, bi, i, 0)),
        ),
        scratch_shapes=[pltpu.VMEM((SSM_WIDTH // LANES, tm, LANES), F32)],
        compiler_params=_params(("parallel", "parallel")),
        name="in_proj",
    )(x, mod, g1, w_all, w_ukv, kv_g, q_gn, k_gn, q_gr, k_gr, cos_t, sin_t)


def _attn_kernel(q_ref, k_ref, v_ref, o_ref, *, tk):
    q = q_ref[0, 0]
    tq = q.shape[0]
    m = jnp.full((tq, 1), -jnp.inf, F32)
    acc = jnp.zeros((tq, 2 * V_DIM), F32)
    for j in range(k_ref.shape[2] // tk):
        kc = k_ref[0, 0, j * tk:(j + 1) * tk, :]
        vc = v_ref[0, 0, j * tk:(j + 1) * tk, :]
        s = lax.dot_general(q, kc, (((1,), (1,)), ((), ())), preferred_element_type=F32)
        m_new = jnp.maximum(m, jnp.max(s, axis=-1, keepdims=True))
        alpha = jnp.exp2(m - m_new)
        p = jnp.exp2(s - m_new).astype(BF16)
        acc = alpha * acc + _dot(p, vc)
        m = m_new
    o_ref[0] = (acc[:, 0:V_DIM] / acc[:, V_DIM:2 * V_DIM]).astype(o_ref.dtype)


def attention(q, k, v):
    b, h, length, _ = q.shape
    tq = min(ATT_TQ, length)
    tk = min(ATT_TK, length)
    return pl.pallas_call(
        functools.partial(_attn_kernel, tk=tk),
        out_shape=jax.ShapeDtypeStruct((b, length, h * V_DIM), BF16),
        grid=(b, h, length // tq),
        in_specs=[
            pl.BlockSpec((1, 1, tq, QK_DIM), lambda bi, hi, i: (bi, hi, i, 0)),
            pl.BlockSpec((1, 1, length, QK_DIM), lambda bi, hi, i: (bi, hi, 0, 0)),
            pl.BlockSpec((1, 1, length, 2 * V_DIM), lambda bi, hi, i: (bi, hi, 0, 0)),
        ],
        out_specs=pl.BlockSpec((1, tq, V_DIM), lambda bi, hi, i: (bi, i, hi)),
        compiler_params=_params(("parallel", "parallel", "parallel")),
        name="attention",
    )(q, k, v)


def _s5_steps():
    g = np.arange(N_GROUPS)[:, None] % GROUPS_PER_COL
    slot = np.arange(S5_CHUNK)[None, :]
    return GROUPS_PER_COL * (slot // GROUPS_PER_COL) + (slot % GROUPS_PER_COL - g) % GROUPS_PER_COL


def s5_matrices(lam_re, lam_im, log_dt, b_re, b_im, c_re, c_im):
    t = S5_CHUNK
    dt = jnp.exp(log_dt)[..., None]
    la, lb = lam_re * dt, lam_im * dt
    lb_re, lb_im = jnp.exp(la) * jnp.cos(lb), jnp.exp(la) * jnp.sin(lb)
    den = lam_re * lam_re + lam_im * lam_im
    n_re, n_im = lb_re - 1.0, lb_im
    cf_re = (n_re * lam_re + n_im * lam_im) / den
    cf_im = (n_im * lam_re - n_re * lam_im) / den
    bb_re = cf_re[..., None] * b_re[None] - cf_im[..., None] * b_im[None]
    bb_im = cf_re[..., None] * b_im[None] + cf_im[..., None] * b_re[None]

    steps = _s5_steps().astype(np.float32)

    def power(d, expo):
        e = jnp.asarray(expo)[:, None, :]
        mag = jnp.exp(la[d][..., None] * e)
        return mag * jnp.cos(lb[d][..., None] * e), mag * jnp.sin(lb[d][..., None] * e)

    cat = jnp.concatenate

    def rows_op(pw, xs1, xs2):
        n = len(xs1)
        pr = cat([jnp.swapaxes(p[0], 1, 2) for p in pw], axis=-1)[:, :, None, :]
        pi = cat([jnp.swapaxes(p[1], 1, 2) for p in pw], axis=-1)[:, :, None, :]
        x1 = cat([jnp.swapaxes(x, 1, 2) for x in xs1], axis=-1)[:, None]
        x2 = cat([jnp.swapaxes(x, 1, 2) for x in xs2], axis=-1)[:, None]
        return (pr * x1 + pi * x2).reshape(N_GROUPS, S5_COLS, n * SSM_STATE)

    def cols_op(pw, xs1, xs2):
        pr = jnp.repeat(cat([p[0] for p in pw], axis=1), SSM_GROUP, axis=2)
        pi = jnp.repeat(cat([p[1] for p in pw], axis=1), SSM_GROUP, axis=2)
        x1 = jnp.tile(cat(xs1, axis=1), (1, 1, S5_CHUNK))
        x2 = jnp.tile(cat(xs2, axis=1), (1, 1, S5_CHUNK))
        return pr * x1 + pi * x2

    bf_re, bf_im, bb_re_, bb_im_ = bb_re[0], bb_im[0], bb_re[1], bb_im[1]
    cf_re_, cf_im_, cb_re, cb_im = c_re[0], c_im[0], c_re[1], c_im[1]
    tr = lambda x: jnp.swapaxes(x, 1, 2)
    p_sb, p_sf = power(1, steps), power(0, t - 1 - steps)
    p_of, p_ob = power(0, steps + 1), power(1, t - steps)

    lags = np.broadcast_to(np.arange(t, dtype=np.float32), (N_GROUPS, t))
    p_kf, p_kb = power(0, lags), power(1, t - 1 - lags)
    cl_f = cols_op([p_kf, p_kf], [tr(cf_re_), tr(cf_im_)], [-tr(cf_im_), tr(cf_re_)])
    cl_b = cols_op([p_kb, p_kb], [tr(cb_re), tr(cb_im)], [-tr(cb_im), tr(cb_re)])
    bt_f = cat([tr(bf_re), -tr(bf_im)], axis=-1)
    bt_b = cat([tr(bb_re_), -tr(bb_im_)], axis=-1)
    m_st = rows_op([p_sf, p_sb, p_sf, p_sb], [bf_re, bb_re_, bf_im, bb_im_],
                   [-bf_im, -bb_im_, bf_re, bb_re_]).astype(BF16)
    m_out = cols_op([p_of, p_ob, p_of, p_ob], [tr(cf_re_), tr(cb_re), -tr(cf_im_), -tr(cb_im)],
                    [-tr(cf_im_), -tr(cb_im), -tr(cf_re_), -tr(cb_re)]).astype(BF16)
    a_re = cat([jnp.exp(t * la[0]) * jnp.cos(t * lb[0]), jnp.exp(t * la[1]) * jnp.cos(t * lb[1])], axis=-1)[:, None, :]
    a_im = cat([jnp.exp(t * la[0]) * jnp.sin(t * lb[0]), jnp.exp(t * la[1]) * jnp.sin(t * lb[1])], axis=-1)[:, None, :]
    return bt_f, cl_f, bt_b, cl_b, m_st, m_out, a_re, a_im


def _s5_kernel(*refs, batches):
    n_in = len(batches)
    u_refs = refs[:n_in]
    btf_ref, clf_ref, btb_ref, clb_ref, mst_ref, mout_ref, are_ref, aim_ref = refs[n_in:n_in + 8]
    y_refs = refs[n_in + 8:2 * n_in + 8]
    st_ref, xin_ref, toep_ref = refs[2 * n_in + 8:]
    n_chunks = u_refs[0].shape[2]

    k_f = _dot_f32(btf_ref[0], clf_ref[0])
    k_b = _dot_f32(btb_ref[0], clb_ref[0])
    lane = lax.broadcasted_iota(jnp.int32, (SSM_GROUP, S5_COLS), 1)
    k_id = pl.program_id(0) % GROUPS_PER_COL
    for s in range(S5_CHUNK):
        lo, hi = s * SSM_GROUP, (s + 1) * SSM_GROUP
        fwd = jnp.where(lane >= lo, pltpu.roll(k_f, lo, axis=1) if lo else k_f, 0.0)
        bwd = jnp.where(lane < hi, pltpu.roll(k_b, hi, axis=1) if hi < S5_COLS else k_b, 0.0)
        blk = fwd + bwd
        blk = jnp.concatenate([pltpu.roll(blk[:, c * LANES:(c + 1) * LANES], k_id * SSM_GROUP, axis=1)
                               for c in range(S5_COLS // LANES)], axis=1)
        slot = (s // GROUPS_PER_COL) * GROUPS_PER_COL + ((s % GROUPS_PER_COL + k_id) & (GROUPS_PER_COL - 1))
        toep_ref[pl.ds(pl.multiple_of(slot * SSM_GROUP, SSM_GROUP), SSM_GROUP), :] = blk.astype(BF16)
    toep = toep_ref[...]
    seqs = [(u_ref, y_ref, b) for u_ref, y_ref, nb in zip(u_refs, y_refs, batches) for b in range(nb)]

    def u_rows(u_ref, b):
        return u_ref[0, b].astype(BF16)

    for slot in range(S5_ROWS):
        rows = pl.ds(slot, n_chunks, stride=S5_ROWS)
        if slot < len(seqs):
            st = _dot(u_rows(seqs[slot][0], seqs[slot][2]), mst_ref[0])
        else:
            st = jnp.zeros((n_chunks, S5_STATE_COLS), F32)
        st_ref[0, rows, :] = st[:, 0:LANES]
        st_ref[1, rows, :] = st[:, LANES:2 * LANES]

    a_re = jnp.broadcast_to(are_ref[0], (S5_ROWS, LANES))
    a_im = jnp.broadcast_to(aim_ref[0], (S5_ROWS, LANES))
    is_fwd = lax.broadcasted_iota(jnp.int32, (S5_ROWS, LANES), 1) < SSM_STATE
    half = SSM_STATE

    def body(i, carry):
        x_re, x_im = carry
        rf = pl.multiple_of(i * S5_ROWS, S5_ROWS)
        rb = pl.multiple_of((n_chunks - 1 - i) * S5_ROWS, S5_ROWS)
        xin_ref[0, pl.ds(rf, S5_ROWS), 0:half] = x_re[:, 0:half]
        xin_ref[0, pl.ds(rb, S5_ROWS), half:LANES] = x_re[:, half:LANES]
        xin_ref[1, pl.ds(rf, S5_ROWS), 0:half] = x_im[:, 0:half]
        xin_ref[1, pl.ds(rb, S5_ROWS), half:LANES] = x_im[:, half:LANES]
        s_re = jnp.where(is_fwd, st_ref[0, pl.ds(rf, S5_ROWS), :], st_ref[0, pl.ds(rb, S5_ROWS), :])
        s_im = jnp.where(is_fwd, st_ref[1, pl.ds(rf, S5_ROWS), :], st_ref[1, pl.ds(rb, S5_ROWS), :])
        n_re = a_re * x_re - a_im * x_im + s_re
        n_im = a_re * x_im + a_im * x_re + s_im
        return n_re, n_im

    zero = jnp.zeros((S5_ROWS, LANES), F32)
    lax.fori_loop(0, n_chunks, body, (zero, zero), unroll=4)

    for slot, (u_ref, y_ref, b) in enumerate(seqs):
        rows = pl.ds(slot, n_chunks, stride=S5_ROWS)
        x_in = jnp.concatenate([xin_ref[0, rows, :], xin_ref[1, rows, :]], axis=1).astype(BF16)
        y_ref[0, b] = _dot(u_rows(u_ref, b), toep) + _dot(x_in, mout_ref[0])


def s5_scan(u_rows, bt_f, cl_f, bt_b, cl_b, m_st, m_out, a_re, a_im):
    g, _, n_chunks, _ = u_rows[0].shape
    batches = tuple(u.shape[1] for u in u_rows)
    assert sum(batches) <= S5_ROWS
    gmap = lambda gi: (gi, 0, 0)
    io_specs = [pl.BlockSpec((1, nb, n_chunks, S5_COLS), lambda gi: (gi, 0, 0, 0)) for nb in batches]
    return pl.pallas_call(
        functools.partial(_s5_kernel, batches=batches),
        out_shape=tuple(jax.ShapeDtypeStruct(u.shape, F32) for u in u_rows),
        grid=(g,),
        in_specs=io_specs + [
            pl.BlockSpec((1, SSM_GROUP, 2 * SSM_STATE), gmap),
            pl.BlockSpec((1, 2 * SSM_STATE, S5_COLS), gmap),
            pl.BlockSpec((1, SSM_GROUP, 2 * SSM_STATE), gmap),
            pl.BlockSpec((1, 2 * SSM_STATE, S5_COLS), gmap),
            pl.BlockSpec((1, S5_COLS, S5_STATE_COLS), gmap),
            pl.BlockSpec((1, S5_STATE_COLS, S5_COLS), gmap),
            pl.BlockSpec((1, 1, LANES), gmap),
            pl.BlockSpec((1, 1, LANES), gmap),
        ],
        out_specs=tuple(io_specs),
        scratch_shapes=[pltpu.VMEM((2, n_chunks * S5_ROWS, LANES), F32),
                        pltpu.VMEM((2, n_chunks * S5_ROWS, LANES), F32),
                        pltpu.VMEM((S5_COLS, S5_COLS), BF16)],
        compiler_params=_params(("parallel",)),
        name="s5_scan",
    )(*u_rows, bt_f, cl_f, bt_b, cl_b, m_st, m_out, a_re, a_im)


def _gelu_tanh(x):
    return 0.5 * x * (1.0 + jnp.tanh(math.sqrt(2.0 / math.pi) * (x + 0.044715 * (x * x * x))))


def _post_mix_kernel(att_ref, ys_ref, u_ref, x_ref, mod_ref, dsk_ref, wglu_ref, bglu_ref,
                     ag_ref, sg_ref, wo_ref, g2_ref, x1_ref, h2_ref, y_cols):
    gate1 = mod_ref[0, 2:3, :]
    shift2 = mod_ref[0, 3:4, :]
    scale2 = mod_ref[0, 4:5, :]
    _from_chunk_rows(ys_ref.at[:, 0], y_cols)
    y = jnp.concatenate([y_cols[j] for j in range(SSM_WIDTH // LANES)], axis=1)
    y = y + dsk_ref[...] * u_ref[0]
    y = _gelu_tanh(y)
    z = _dot(y.astype(BF16), wglu_ref[...]) + bglu_ref[...]
    ssm = y * jax.nn.sigmoid(z)
    att = att_ref[0].astype(F32)
    att_n = (att * _rms(att, ATT_WIDTH) * ag_ref[...]).astype(BF16)
    ssm_n = (ssm * _rms(ssm, SSM_WIDTH) * sg_ref[...]).astype(BF16)
    mix = _dot(att_n, wo_ref[0:ATT_WIDTH, :]) + _dot(ssm_n, wo_ref[ATT_WIDTH:2 * ATT_WIDTH, :])
    x1 = x_ref[0] + gate1 * mix
    x1_ref[0] = x1
    h2 = (x1 * _rms(x1, D_MODEL) * g2_ref[...]) * (1.0 + scale2) + shift2
    h2_ref[0] = h2.astype(BF16)


def post_mix(att, ys, u, x, mod, d_skip, w_glu, b_glu, att_g, ssm_g, w_o, g2):
    b, length, d = x.shape
    tm = S5_TILE
    const = lambda bi, i: (0, 0)
    tile = lambda w: pl.BlockSpec((1, tm, w), lambda bi, i: (bi, i, 0))
    return pl.pallas_call(
        _post_mix_kernel,
        out_shape=(jax.ShapeDtypeStruct((b, length, d), F32),
                   jax.ShapeDtypeStruct((b, length, d), BF16)),
        grid=(b, length // tm),
        in_specs=[
            tile(ATT_WIDTH),
            pl.BlockSpec((N_GROUPS, 1, S5_SUB, S5_COLS), lambda bi, i: (0, bi, i, 0)),
            tile(SSM_WIDTH), tile(d),
            pl.BlockSpec((1, 6, d), lambda bi, i: (bi, 0, 0)),
            pl.BlockSpec((1, SSM_WIDTH), const),
            pl.BlockSpec((SSM_WIDTH, SSM_WIDTH), const),
            pl.BlockSpec((1, SSM_WIDTH), const),
            pl.BlockSpec((1, ATT_WIDTH), const),
            pl.BlockSpec((1, SSM_WIDTH), const),
            pl.BlockSpec((d, d), const),
            pl.BlockSpec((1, d), const),
        ],
        out_specs=(tile(d), tile(d)),
        scratch_shapes=[pltpu.VMEM((SSM_WIDTH // LANES, tm, LANES), F32)],
        compiler_params=_params(("parallel", "parallel")),
        name="post_mix",
    )(att, ys, u, x, mod, d_skip, w_glu, b_glu, att_g, ssm_g, w_o, g2)


def _ffn_up_kernel(h_ref, w1_ref, w3_ref, o_ref, w1_bf, w3_bf):
    @pl.when(jnp.logical_and(pl.program_id(1) == 0, pl.program_id(2) == 0))
    def _():
        w1_bf[...] = w1_ref[...].astype(BF16)
        w3_bf[...] = w3_ref[...].astype(BF16)

    h = h_ref[0]
    a = _dot(h, w1_bf[...])
    g = _dot(h, w3_bf[...])
    o_ref[0] = (a * jax.nn.sigmoid(a) * g).astype(BF16)


def ffn_up(h2, w1, w3):
    b, length, d = h2.shape
    tm, tf = min(FFN_UP_TM, length), FFN_UP_TF
    return pl.pallas_call(
        _ffn_up_kernel,
        out_shape=jax.ShapeDtypeStruct((b, length, D_FF), BF16),
        grid=(D_FF // tf, b, length // tm),
        in_specs=[
            pl.BlockSpec((1, tm, d), lambda f, bi, i: (bi, i, 0)),
            pl.BlockSpec((d, tf), lambda f, bi, i: (0, f)),
            pl.BlockSpec((d, tf), lambda f, bi, i: (0, f)),
        ],
        out_specs=pl.BlockSpec((1, tm, tf), lambda f, bi, i: (bi, i, f)),
        scratch_shapes=[pltpu.VMEM((d, tf), BF16), pltpu.VMEM((d, tf), BF16)],
        compiler_params=_params(("parallel", "arbitrary", "arbitrary")),
        name="ffn_up",
    )(h2, w1, w3)


def _ffn_down_kernel(a_ref, x1_ref, mod_ref, w2_ref, o_ref):
    o_ref[0] = x1_ref[0] + mod_ref[0, 5:6, :] * _dot(a_ref[0], w2_ref[...])


def ffn_down(act, x1, mod, w2):
    b, length, d = x1.shape
    tm, tn = min(FFN_DOWN_TM, length), FFN_DOWN_TN
    return pl.pallas_call(
        _ffn_down_kernel,
        out_shape=jax.ShapeDtypeStruct((b, length, d), F32),
        grid=(d // tn, b, length // tm),
        in_specs=[
            pl.BlockSpec((1, tm, D_FF), lambda n, bi, i: (bi, i, 0)),
            pl.BlockSpec((1, tm, tn), lambda n, bi, i: (bi, i, n)),
            pl.BlockSpec((1, 6, tn), lambda n, bi, i: (bi, 0, n)),
            pl.BlockSpec((D_FF, tn), lambda n, bi, i: (0, n)),
        ],
        out_specs=pl.BlockSpec((1, tm, tn), lambda n, bi, i: (bi, i, n)),
        compiler_params=_params(("parallel", "parallel", "parallel")),
        name="ffn_down",
    )(act, x1, mod, w2)


def _cast_kernel(x_ref, o_ref):
    o_ref[...] = x_ref[...].astype(o_ref.dtype)


def cast_bf16(w):
    rows, cols = w.shape
    tr = rows
    while tr * cols * 4 > CAST_BLOCK_BYTES and tr % (4 * SUBLANES) == 0:
        tr //= 2
    return pl.pallas_call(
        _cast_kernel,
        out_shape=jax.ShapeDtypeStruct((rows, cols), BF16),
        grid=(rows // tr,),
        in_specs=[pl.BlockSpec((tr, cols), lambda i: (i, 0))],
        out_specs=pl.BlockSpec((tr, cols), lambda i: (i, 0)),
        compiler_params=_params(("parallel",)),
        name="cast_bf16",
    )(w)


def _swap_halves(a, axis=-1):
    lo, hi = jnp.split(a, 2, axis=axis)
    return jnp.concatenate([hi, lo], axis=axis)


def _prep_w_in_kernel(w_ref, o_ref):
    w = w_ref[...]
    q_cols = N_HEADS * QK_DIM
    half = ROPE_DIM // 2

    def with_swapped(r):
        return [r, r[:, half:], r[:, :half]]

    cols = [w[:, hd * QK_DIM:hd * QK_DIM + NOPE_DIM] for hd in range(N_HEADS)]
    cols += [w[:, hd * QK_DIM + NOPE_DIM:(hd + 1) * QK_DIM] for hd in range(N_HEADS)]
    cols.append(w[:, q_cols:q_cols + KV_RANK])
    cols += with_swapped(w[:, q_cols + KV_RANK:q_cols + KV_RANK + ROPE_DIM])
    cols.append(w[:, q_cols + KV_RANK + ROPE_DIM:])
    o_ref[...] = jnp.concatenate(cols, axis=1).astype(BF16)


def prep_w_in(w_in):
    d, n = w_in.shape
    return pl.pallas_call(
        _prep_w_in_kernel,
        out_shape=jax.ShapeDtypeStruct((d, C_END), BF16),
        grid=(d // W_PREP_ROWS,),
        in_specs=[pl.BlockSpec((W_PREP_ROWS, n), lambda i: (i, 0))],
        out_specs=pl.BlockSpec((W_PREP_ROWS, C_END), lambda i: (i, 0)),
        compiler_params=_params(("parallel",)),
        name="prep_w_in",
    )(w_in)


def prep_w_ukv(w_ukv):
    rank, n = w_ukv.shape
    src = lambda j: (0, 2 * (j % N_HEADS) + j // N_HEADS)
    return pl.pallas_call(
        _cast_kernel,
        out_shape=jax.ShapeDtypeStruct((rank, n), BF16),
        grid=(n // LANES,),
        in_specs=[pl.BlockSpec((rank, LANES), src)],
        out_specs=pl.BlockSpec((rank, LANES), lambda j: (0, j)),
        compiler_params=_params(("parallel",)),
        name="prep_w_ukv",
    )(w_ukv)


def _rope_gain(g):
    gr = g[NOPE_DIM:]
    return jnp.concatenate([gr, _swap_halves(gr)]).reshape(1, 2 * ROPE_DIM)


def _rope_tables(length):
    pos = jnp.arange(length, dtype=F32)
    inv_freq = ROPE_BASE ** (-jnp.arange(0, ROPE_DIM, 2, dtype=F32) / ROPE_DIM)
    ang = pos[:, None] * inv_freq[None, :]
    cos, sin = jnp.cos(ang), jnp.sin(ang)
    return jnp.tile(cos, (1, 4)), jnp.tile(jnp.concatenate([-sin, sin], axis=-1), (1, 2))


def kernel(x_prompt, x_sample, c_prompt, c_sample, w_ada, b_ada, norm_mix_g, w_in, kv_norm_g, w_ukv,
           q_norm_g, k_norm_g, lam_re, lam_im, log_dt, b_re, b_im, c_re, c_im, d_skip, w_glu, b_glu,
           att_out_g, ssm_out_g, w_o, norm_ffn_g, w1, w3, w2):
    assert w_ada.shape[0] == 1, "single-layer kernel"
    xs = (x_prompt, x_sample)
    batches = tuple(x.shape[0] for x in xs)
    length = x_prompt.shape[1]
    assert x_sample.shape[1] == length and sum(batches) <= S5_ROWS

    c_all = jnp.concatenate([c_prompt, c_sample], axis=0)
    c8 = jnp.pad(c_all, ((0, SUBLANES - c_all.shape[0]), (0, 0)))
    mod_all = ada_mod(c8, w_ada[0], b_ada[0]).reshape(SUBLANES, 6, D_MODEL)
    mods = (mod_all[:batches[0]], mod_all[batches[0]:batches[0] + batches[1]])

    w_all = prep_w_in(w_in[0])
    w_ukv_p = prep_w_ukv(w_ukv[0])
    row = lambda a: a.reshape(1, -1)
    q_gn, k_gn = row(q_norm_g[0][:NOPE_DIM]), row(k_norm_g[0][:NOPE_DIM])
    q_gr = jnp.tile(q_norm_g[0][NOPE_DIM:], LANES // ROPE_DIM).reshape(1, LANES)
    k_gr = _rope_gain(k_norm_g[0])
    cos_t, sin_t = _rope_tables(length)
    s5_ops = s5_matrices(lam_re[0], lam_im[0], log_dt[0], b_re[0], b_im[0], c_re[0], c_im[0])
    w_glu_b, w_o_b = cast_bf16(w_glu[0]), cast_bf16(w_o[0])
    w2_b = cast_bf16(w2[0])

    proj = [in_proj(x, m, row(norm_mix_g[0]), w_all, w_ukv_p, row(kv_norm_g[0]), q_gn, k_gn, q_gr, k_gr,
                    cos_t, sin_t) for x, m in zip(xs, mods)]
    atts = [attention(q, k, v) for q, k, v, _, _ in proj]
    y_rows = s5_scan([p[4] for p in proj], *s5_ops)

    outs = []
    for x, m, att, ys, p in zip(xs, mods, atts, y_rows, proj):
        x1, h2 = post_mix(att, ys, p[3], x, m, row(d_skip[0]), w_glu_b, row(b_glu[0]), row(att_out_g[0]),
                          row(ssm_out_g[0]), w_o_b, row(norm_ffn_g[0]))
        outs.append(ffn_down(ffn_up(h2, w1[0], w3[0]), x1, m, w2_b))
    return tuple(outs)
```

```python
import functools
import math

import jax
import jax.numpy as jnp
import numpy as np
from jax import lax
from jax.experimental import pallas as pl
from jax.experimental.pallas import tpu as pltpu

F32 = jnp.float32
BF16 = jnp.bfloat16

D_MODEL = 2048
ATT_WIDTH = 1024
SSM_WIDTH = 1024
N_HEADS = 8
V_DIM = 128
NOPE_DIM = 128
ROPE_DIM = 64
QK_DIM = NOPE_DIM + ROPE_DIM
KV_RANK = 512
ROPE_BASE = 10000.0
SSM_GROUP = 16
N_GROUPS = SSM_WIDTH // SSM_GROUP
SSM_STATE = 64
D_FF = 5632
NORM_EPS = 1e-6

LANES = 128
SUBLANES = 8
VMEM_LIMIT = 56 * 1024 * 1024
CAST_BLOCK_BYTES = 4 * 1024 * 1024

ADA_TN = 1024
ATT_TQ, ATT_TK = 1024, 256
FFN_UP_TM, FFN_UP_TF = 1024, 512
FFN_DOWN_TM, FFN_DOWN_TN = 512, 1024
W_PREP_ROWS = 256
POST_MIX_SUBTILES = 2

C_QN = 0
C_QR = C_QN + N_HEADS * NOPE_DIM
C_KV = C_QR + N_HEADS * ROPE_DIM
C_KR = C_KV + KV_RANK
C_U = C_KR + 2 * ROPE_DIM
C_END = C_U + SSM_WIDTH

S5_CHUNK = 32
S5_ROWS = SUBLANES
S5_SUB = SUBLANES
S5_TILE = S5_SUB * S5_CHUNK
S5_COLS = S5_CHUNK * SSM_GROUP
S5_STATE_COLS = 4 * SSM_STATE
GROUPS_PER_COL = LANES // SSM_GROUP


def _params(sem):
    return pltpu.CompilerParams(dimension_semantics=sem, vmem_limit_bytes=VMEM_LIMIT)


def _dot(a, b):
    return jnp.dot(a, b, preferred_element_type=F32)


def _dot_f32(a, b):
    a_hi = a.astype(BF16)
    a_lo = (a - a_hi.astype(F32)).astype(BF16)
    b_hi = b.astype(BF16)
    b_lo = (b - b_hi.astype(F32)).astype(BF16)
    return _dot(a_hi, b_hi) + _dot(a_lo, b_hi) + _dot(a_hi, b_lo)


def _ada_kernel(c_ref, w_ref, b_ref, o_ref):
    c = c_ref[...]
    o_ref[...] = _dot_f32(c * jax.nn.sigmoid(c), w_ref[...]) + b_ref[...]


def ada_mod(c8, w_ada, b_ada):
    rows, d = c8.shape
    n = w_ada.shape[1]
    tn = ADA_TN
    return pl.pallas_call(
        _ada_kernel,
        out_shape=jax.ShapeDtypeStruct((rows, n), F32),
        grid=(n // tn,),
        in_specs=[
            pl.BlockSpec((rows, d), lambda j: (0, 0)),
            pl.BlockSpec((d, tn), lambda j: (0, j)),
            pl.BlockSpec((1, tn), lambda j: (0, j)),
        ],
        out_specs=pl.BlockSpec((rows, tn), lambda j: (0, j)),
        compiler_params=_params(("arbitrary",)),
        name="ada_mod",
    )(c8, w_ada, b_ada.reshape(1, n))


def _rms(x, width):
    return lax.rsqrt(jnp.sum(x * x, axis=-1, keepdims=True) * (1.0 / width) + NORM_EPS)


def _granule_masks():
    pos = lax.broadcasted_iota(jnp.int32, (SUBLANES, LANES), 1) // SSM_GROUP
    return [pos == p for p in range(GROUPS_PER_COL)]


def _to_chunk_rows(u_cols, dst):
    masks = _granule_masks()
    for j in range(SSM_WIDTH // LANES):
        rolled = []
        for t in range(S5_CHUNK):
            v = u_cols[j, pl.ds(t, S5_SUB, stride=S5_CHUNK), :]
            r = t % GROUPS_PER_COL
            rolled.append(pltpu.roll(v, SSM_GROUP * r, axis=1) if r else v)
        for k in range(GROUPS_PER_COL):
            for d in range(S5_COLS // LANES):
                acc = rolled[GROUPS_PER_COL * d + GROUPS_PER_COL - 1]
                for rho in range(GROUPS_PER_COL - 2, -1, -1):
                    acc = jnp.where(masks[(k + rho) % GROUPS_PER_COL], rolled[GROUPS_PER_COL * d + rho], acc)
                dst[GROUPS_PER_COL * j + k, :, d * LANES:(d + 1) * LANES] = acc


def _from_chunk_rows(src, y_cols):
    masks = _granule_masks()
    for j in range(SSM_WIDTH // LANES):
        for d in range(S5_COLS // LANES):
            cols = [src[GROUPS_PER_COL * j + k, :, d * LANES:(d + 1) * LANES] for k in range(GROUPS_PER_COL)]
            for rho in range(GROUPS_PER_COL):
                acc = cols[GROUPS_PER_COL - 1]
                for k in range(GROUPS_PER_COL - 2, -1, -1):
                    acc = jnp.where(masks[(k + rho) % GROUPS_PER_COL], cols[k], acc)
                v = pltpu.roll(acc, LANES - SSM_GROUP * rho, axis=1) if rho else acc
                y_cols[j, pl.ds(GROUPS_PER_COL * d + rho, S5_SUB, stride=S5_CHUNK), :] = v


def _in_proj_kernel(x_ref, mod_ref, g1_ref, w_ref, wukv_ref, kvg_ref, qg_ref, kg_ref,
                    qgr_ref, kgr_ref, cos_ref, sin_ref, q_ref, k_ref, v_ref, u_ref, ur_ref, u_cols,
                    *, q_scale):
    x = x_ref[0]
    shift = mod_ref[0, 0:1, :]
    scale = mod_ref[0, 1:2, :]
    h = (x * _rms(x, D_MODEL) * g1_ref[...]) * (1.0 + scale) + shift
    h = h.astype(BF16)

    cos = cos_ref[...]
    sin = sin_ref[...]

    qn = _dot(h, w_ref[:, C_QN:C_QR])
    qr = _dot(h, w_ref[:, C_QR:C_KV])
    qg_n = qg_ref[...]
    qg_r = qgr_ref[...]
    lane = lax.broadcasted_iota(jnp.int32, (x.shape[0], LANES), 1)
    low_half = (lane & (ROPE_DIM - 1)) < ROPE_DIM // 2
    even_head = lane < ROPE_DIM
    for pair in range(N_HEADS // 2):
        v = qr[:, pair * LANES:(pair + 1) * LANES]
        cg = v * qg_r
        swapped = jnp.where(low_half, pltpu.roll(cg, LANES - ROPE_DIM // 2, axis=1),
                            pltpu.roll(cg, ROPE_DIM // 2, axis=1))
        rot = cg * cos + swapped * sin
        sq = v * v
        ss_even = jnp.sum(jnp.where(even_head, sq, 0.0), axis=-1, keepdims=True)
        ss_odd = jnp.sum(sq, axis=-1, keepdims=True) - ss_even
        for odd, ss_r in ((0, ss_even), (1, ss_odd)):
            hd = 2 * pair + odd
            qn_h = qn[:, hd * LANES:(hd + 1) * LANES]
            ss = jnp.sum(qn_h * qn_h, axis=-1, keepdims=True) + ss_r
            r = lax.rsqrt(ss * (1.0 / QK_DIM) + NORM_EPS) * q_scale
            q_ref[0, hd, :, 0:NOPE_DIM] = (qn_h * r * qg_n).astype(BF16)
            rot_h = pltpu.roll(rot, ROPE_DIM, axis=1) if odd else rot
            q_ref[0, hd, :, NOPE_DIM:QK_DIM] = (rot_h * r)[:, 0:ROPE_DIM].astype(BF16)

    ckv = _dot(h, w_ref[:, C_KV:C_KR])
    ckv_n = (ckv * _rms(ckv, KV_RANK) * kvg_ref[...]).astype(BF16)
    kv = _dot(ckv_n, wukv_ref[...])
    kr = _dot(h, w_ref[:, C_KR:C_U])
    kr_ss = 0.5 * jnp.sum(kr * kr, axis=-1, keepdims=True)
    kr_g = kr * kgr_ref[...]
    kr_rot = kr_g * cos + pltpu.roll(kr_g, ROPE_DIM, axis=1) * sin
    kg_n = kg_ref[...]
    ones = jnp.ones((x.shape[0], LANES), BF16)
    for hd in range(N_HEADS):
        kn_h = kv[:, hd * LANES:(hd + 1) * LANES]
        ss = jnp.sum(kn_h * kn_h, axis=-1, keepdims=True) + kr_ss
        r = lax.rsqrt(ss * (1.0 / QK_DIM) + NORM_EPS)
        k_ref[0, hd, :, 0:NOPE_DIM] = (kn_h * r * kg_n).astype(BF16)
        k_ref[0, hd, :, NOPE_DIM:QK_DIM] = (kr_rot * r)[:, 0:ROPE_DIM].astype(BF16)
        v_h = kv[:, ATT_WIDTH + hd * LANES:ATT_WIDTH + (hd + 1) * LANES]
        v_ref[0, hd, :, 0:V_DIM] = v_h.astype(BF16)
        v_ref[0, hd, :, V_DIM:2 * V_DIM] = ones

    u = _dot(h, w_ref[:, C_U:C_END])
    u_ref[0] = u
    for j in range(SSM_WIDTH // LANES):
        u_cols[j] = u[:, j * LANES:(j + 1) * LANES]
    _to_chunk_rows(u_cols, ur_ref.at[:, 0])


def in_proj(x, mod, g1, w_all, w_ukv, kv_g, q_gn, k_gn, q_gr, k_gr, cos_t, sin_t):
    b, length, d = x.shape
    tm = S5_TILE
    assert length % tm == 0
    const = lambda bi, i: (0, 0)
    kern = functools.partial(_in_proj_kernel, q_scale=math.log2(math.e) / math.sqrt(QK_DIM))
    return pl.pallas_call(
        kern,
        out_shape=(
            jax.ShapeDtypeStruct((b, N_HEADS, length, QK_DIM), BF16),
            jax.ShapeDtypeStruct((b, N_HEADS, length, QK_DIM), BF16),
            jax.ShapeDtypeStruct((b, N_HEADS, length, 2 * V_DIM), BF16),
            jax.ShapeDtypeStruct((b, length, SSM_WIDTH), F32),
            jax.ShapeDtypeStruct((N_GROUPS, b, length // S5_CHUNK, S5_COLS), F32),
        ),
        grid=(b, length // tm),
        in_specs=[
            pl.BlockSpec((1, tm, d), lambda bi, i: (bi, i, 0)),
            pl.BlockSpec((1, 6, d), lambda bi, i: (bi, 0, 0)),
            pl.BlockSpec((1, d), const),
            pl.BlockSpec((d, C_END), const),
            pl.BlockSpec((KV_RANK, 2 * ATT_WIDTH), const),
            pl.BlockSpec((1, KV_RANK), const),
            pl.BlockSpec((1, LANES), const),
            pl.BlockSpec((1, LANES), const),
            pl.BlockSpec((1, LANES), const),
            pl.BlockSpec((1, LANES), const),
            pl.BlockSpec((tm, LANES), lambda bi, i: (i, 0)),
            pl.BlockSpec((tm, LANES), lambda bi, i: (i, 0)),
        ],
        out_specs=(
            pl.BlockSpec((1, N_HEADS, tm, QK_DIM), lambda bi, i: (bi, 0, i, 0)),
            pl.BlockSpec((1, N_HEADS, tm, QK_DIM), lambda bi, i: (bi, 0, i, 0)),
            pl.BlockSpec((1, N_HEADS, tm, 2 * V_DIM), lambda bi, i: (bi, 0, i, 0)),
            pl.BlockSpec((1, tm, SSM_WIDTH), lambda bi, i: (bi, i, 0)),
            pl.BlockSpec((N_GROUPS, 1, S5_SUB, S5_COLS), lambda bi, i: (0, bi, i, 0)),
        ),
        scratch_shapes=[pltpu.VMEM((SSM_WIDTH // LANES, tm, LANES), F32)],
        compiler_params=_params(("parallel", "parallel")),
        name="in_proj",
    )(x, mod, g1, w_all, w_ukv, kv_g, q_gn, k_gn, q_gr, k_gr, cos_t, sin_t)


def _attn_kernel(q_ref, k_ref, v_ref, o_ref, *, tk):
    q = q_ref[0, 0]
    tq = q.shape[0]
    m = jnp.full((tq, 1), -jnp.inf, F32)
    acc = jnp.zeros((tq, 2 * V_DIM), F32)
    for j in range(k_ref.shape[2] // tk):
        kc = k_ref[0, 0, j * tk:(j + 1) * tk, :]
        vc = v_ref[0, 0, j * tk:(j + 1) * tk, :]
        s = lax.dot_general(q, kc, (((1,), (1,)), ((), ())), preferred_element_type=F32)
        m_new = jnp.maximum(m, jnp.max(s, axis=-1, keepdims=True))
        alpha = jnp.exp2(m - m_new)
        p = jnp.exp2(s - m_new).astype(BF16)
        acc = alpha * acc + _dot(p, vc)
        m = m_new
    o_ref[0] = (acc[:, 0:V_DIM] / acc[:, V_DIM:2 * V_DIM]).astype(o_ref.dtype)


def attention(q, k, v):
    b, h, length, _ = q.shape
    tq = min(ATT_TQ, length)
    tk = min(ATT_TK, length)
    return pl.pallas_call(
        functools.partial(_attn_kernel, tk=tk),
        out_shape=jax.ShapeDtypeStruct((b, length, h * V_DIM), BF16),
        grid=(b, h, length // tq),
        in_specs=[
            pl.BlockSpec((1, 1, tq, QK_DIM), lambda bi, hi, i: (bi, hi, i, 0)),
            pl.BlockSpec((1, 1, length, QK_DIM), lambda bi, hi, i: (bi, hi, 0, 0)),
            pl.BlockSpec((1, 1, length, 2 * V_DIM), lambda bi, hi, i: (bi, hi, 0, 0)),
        ],
        out_specs=pl.BlockSpec((1, tq, V_DIM), lambda bi, hi, i: (bi, i, hi)),
        compiler_params=_params(("parallel", "parallel", "parallel")),
        name="attention",
    )(q, k, v)


def _s5_steps():
    g = np.arange(N_GROUPS)[:, None] % GROUPS_PER_COL
    slot = np.arange(S5_CHUNK)[None, :]
    return GROUPS_PER_COL * (slot // GROUPS_PER_COL) + (slot % GROUPS_PER_COL - g) % GROUPS_PER_COL


def s5_matrices(lam_re, lam_im, log_dt, b_re, b_im, c_re, c_im):
    t = S5_CHUNK
    dt = jnp.exp(log_dt)[..., None]
    la, lb = lam_re * dt, lam_im * dt
    lb_re, lb_im = jnp.exp(la) * jnp.cos(lb), jnp.exp(la) * jnp.sin(lb)
    den = lam_re * lam_re + lam_im * lam_im
    n_re, n_im = lb_re - 1.0, lb_im
    cf_re = (n_re * lam_re + n_im * lam_im) / den
    cf_im = (n_im * lam_re - n_re * lam_im) / den
    bb_re = cf_re[..., None] * b_re[None] - cf_im[..., None] * b_im[None]
    bb_im = cf_re[..., None] * b_im[None] + cf_im[..., None] * b_re[None]

    steps = _s5_steps().astype(np.float32)

    def power(d, expo):
        e = jnp.asarray(expo)[:, None, :]
        mag = jnp.exp(la[d][..., None] * e)
        return mag * jnp.cos(lb[d][..., None] * e), mag * jnp.sin(lb[d][..., None] * e)

    cat = jnp.concatenate

    def rows_op(pw, xs1, xs2):
        n = len(xs1)
        pr = cat([jnp.swapaxes(p[0], 1, 2) for p in pw], axis=-1)[:, :, None, :]
        pi = cat([jnp.swapaxes(p[1], 1, 2) for p in pw], axis=-1)[:, :, None, :]
        x1 = cat([jnp.swapaxes(x, 1, 2) for x in xs1], axis=-1)[:, None]
        x2 = cat([jnp.swapaxes(x, 1, 2) for x in xs2], axis=-1)[:, None]
        return (pr * x1 + pi * x2).reshape(N_GROUPS, S5_COLS, n * SSM_STATE)

    def cols_op(pw, xs1, xs2):
        pr = jnp.repeat(cat([p[0] for p in pw], axis=1), SSM_GROUP, axis=2)
        pi = jnp.repeat(cat([p[1] for p in pw], axis=1), SSM_GROUP, axis=2)
        x1 = jnp.tile(cat(xs1, axis=1), (1, 1, S5_CHUNK))
        x2 = jnp.tile(cat(xs2, axis=1), (1, 1, S5_CHUNK))
        return pr * x1 + pi * x2

    bf_re, bf_im, bb_re_, bb_im_ = bb_re[0], bb_im[0], bb_re[1], bb_im[1]
    cf_re_, cf_im_, cb_re, cb_im = c_re[0], c_im[0], c_re[1], c_im[1]
    tr = lambda x: jnp.swapaxes(x, 1, 2)
    p_sb, p_sf = power(1, steps), power(0, t - 1 - steps)
    p_of, p_ob = power(0, steps + 1), power(1, t - steps)

    lags = np.broadcast_to(np.arange(t, dtype=np.float32), (N_GROUPS, t))
    p_kf, p_kb = power(0, lags), power(1, t - 1 - lags)
    cl_f = cols_op([p_kf, p_kf], [tr(cf_re_), tr(cf_im_)], [-tr(cf_im_), tr(cf_re_)])
    cl_b = cols_op([p_kb, p_kb], [tr(cb_re), tr(cb_im)], [-tr(cb_im), tr(cb_re)])
    bt_f = cat([tr(bf_re), -tr(bf_im)], axis=-1)
    bt_b = cat([tr(bb_re_), -tr(bb_im_)], axis=-1)
    m_st = rows_op([p_sf, p_sb, p_sf, p_sb], [bf_re, bb_re_, bf_im, bb_im_],
                   [-bf_im, -bb_im_, bf_re, bb_re_]).astype(BF16)
    m_out = cols_op([p_of, p_ob, p_of, p_ob], [tr(cf_re_), tr(cb_re), -tr(cf_im_), -tr(cb_im)],
                    [-tr(cf_im_), -tr(cb_im), -tr(cf_re_), -tr(cb_re)]).astype(BF16)
    a_re = cat([jnp.exp(t * la[0]) * jnp.cos(t * lb[0]), jnp.exp(t * la[1]) * jnp.cos(t * lb[1])], axis=-1)[:, None, :]
    a_im = cat([jnp.exp(t * la[0]) * jnp.sin(t * lb[0]), jnp.exp(t * la[1]) * jnp.sin(t * lb[1])], axis=-1)[:, None, :]
    return bt_f, cl_f, bt_b, cl_b, m_st, m_out, a_re, a_im


def _s5_kernel(*refs, batches):
    n_in = len(batches)
    u_refs = refs[:n_in]
    btf_ref, clf_ref, btb_ref, clb_ref, mst_ref, mout_ref, are_ref, aim_ref = refs[n_in:n_in + 8]
    y_refs = refs[n_in + 8:2 * n_in + 8]
    st_ref, xin_ref, toep_ref = refs[2 * n_in + 8:]
    n_chunks = u_refs[0].shape[2]

    k_f = _dot_f32(btf_ref[0], clf_ref[0])
    k_b = _dot_f32(btb_ref[0], clb_ref[0])
    lane = lax.broadcasted_iota(jnp.int32, (SSM_GROUP, S5_COLS), 1)
    k_id = pl.program_id(0) % GROUPS_PER_COL
    for s in range(S5_CHUNK):
        lo, hi = s * SSM_GROUP, (s + 1) * SSM_GROUP
        fwd = jnp.where(lane >= lo, pltpu.roll(k_f, lo, axis=1) if lo else k_f, 0.0)
        bwd = jnp.where(lane < hi, pltpu.roll(k_b, hi, axis=1) if hi < S5_COLS else k_b, 0.0)
        blk = fwd + bwd
        blk = jnp.concatenate([pltpu.roll(blk[:, c * LANES:(c + 1) * LANES], k_id * SSM_GROUP, axis=1)
                               for c in range(S5_COLS // LANES)], axis=1)
        slot = (s // GROUPS_PER_COL) * GROUPS_PER_COL + ((s % GROUPS_PER_COL + k_id) & (GROUPS_PER_COL - 1))
        toep_ref[pl.ds(pl.multiple_of(slot * SSM_GROUP, SSM_GROUP), SSM_GROUP), :] = blk.astype(BF16)
    toep = toep_ref[...]
    seqs = [(u_ref, y_ref, b) for u_ref, y_ref, nb in zip(u_refs, y_refs, batches) for b in range(nb)]

    def u_rows(u_ref, b):
        return u_ref[0, b].astype(BF16)

    for slot in range(S5_ROWS):
        rows = pl.ds(slot, n_chunks, stride=S5_ROWS)
        if slot < len(seqs):
            st = _dot(u_rows(seqs[slot][0], seqs[slot][2]), mst_ref[0])
        else:
            st = jnp.zeros((n_chunks, S5_STATE_COLS), F32)
        st_ref[0, rows, :] = st[:, 0:LANES]
        st_ref[1, rows, :] = st[:, LANES:2 * LANES]

    a_re = jnp.broadcast_to(are_ref[0], (S5_ROWS, LANES))
    a_im = jnp.broadcast_to(aim_ref[0], (S5_ROWS, LANES))
    is_fwd = lax.broadcasted_iota(jnp.int32, (S5_ROWS, LANES), 1) < SSM_STATE
    half = SSM_STATE

    def body(i, carry):
        x_re, x_im = carry
        rf = pl.multiple_of(i * S5_ROWS, S5_ROWS)
        rb = pl.multiple_of((n_chunks - 1 - i) * S5_ROWS, S5_ROWS)
        xin_ref[0, pl.ds(rf, S5_ROWS), 0:half] = x_re[:, 0:half]
        xin_ref[0, pl.ds(rb, S5_ROWS), half:LANES] = x_re[:, half:LANES]
        xin_ref[1, pl.ds(rf, S5_ROWS), 0:half] = x_im[:, 0:half]
        xin_ref[1, pl.ds(rb, S5_ROWS), half:LANES] = x_im[:, half:LANES]
        s_re = jnp.where(is_fwd, st_ref[0, pl.ds(rf, S5_ROWS), :], st_ref[0, pl.ds(rb, S5_ROWS), :])
        s_im = jnp.where(is_fwd, st_ref[1, pl.ds(rf, S5_ROWS), :], st_ref[1, pl.ds(rb, S5_ROWS), :])
        n_re = a_re * x_re - a_im * x_im + s_re
        n_im = a_re * x_im + a_im * x_re + s_im
        return n_re, n_im

    zero = jnp.zeros((S5_ROWS, LANES), F32)
    lax.fori_loop(0, n_chunks, body, (zero, zero), unroll=4)

    for slot, (u_ref, y_ref, b) in enumerate(seqs):
        rows = pl.ds(slot, n_chunks, stride=S5_ROWS)
        x_in = jnp.concatenate([xin_ref[0, rows, :], xin_ref[1, rows, :]], axis=1).astype(BF16)
        y_ref[0, b] = _dot(u_rows(u_ref, b), toep) + _dot(x_in, mout_ref[0])


def s5_scan(u_rows, bt_f, cl_f, bt_b, cl_b, m_st, m_out, a_re, a_im):
    g, _, n_chunks, _ = u_rows[0].shape
    batches = tuple(u.shape[1] for u in u_rows)
    assert sum(batches) <= S5_ROWS
    gmap = lambda gi: (gi, 0, 0)
    io_specs = [pl.BlockSpec((1, nb, n_chunks, S5_COLS), lambda gi: (gi, 0, 0, 0)) for nb in batches]
    return pl.pallas_call(
        functools.partial(_s5_kernel, batches=batches),
        out_shape=tuple(jax.ShapeDtypeStruct(u.shape, F32) for u in u_rows),
        grid=(g,),
        in_specs=io_specs + [
            pl.BlockSpec((1, SSM_GROUP, 2 * SSM_STATE), gmap),
            pl.BlockSpec((1, 2 * SSM_STATE, S5_COLS), gmap),
            pl.BlockSpec((1, SSM_GROUP, 2 * SSM_STATE), gmap),
            pl.BlockSpec((1, 2 * SSM_STATE, S5_COLS), gmap),
            pl.BlockSpec((1, S5_COLS, S5_STATE_COLS), gmap),
            pl.BlockSpec((1, S5_STATE_COLS, S5_COLS), gmap),
            pl.BlockSpec((1, 1, LANES), gmap),
            pl.BlockSpec((1, 1, LANES), gmap),
        ],
        out_specs=tuple(io_specs),
        scratch_shapes=[pltpu.VMEM((2, n_chunks * S5_ROWS, LANES), F32),
                        pltpu.VMEM((2, n_chunks * S5_ROWS, LANES), F32),
                        pltpu.VMEM((S5_COLS, S5_COLS), BF16)],
        compiler_params=_params(("parallel",)),
        name="s5_scan",
    )(*u_rows, bt_f, cl_f, bt_b, cl_b, m_st, m_out, a_re, a_im)


def _gelu_tanh(x):
    return 0.5 * x * (1.0 + jnp.tanh(math.sqrt(2.0 / math.pi) * (x + 0.044715 * (x * x * x))))


def _post_mix_kernel(att_ref, ys_ref, u_ref, x_ref, mod_ref, dsk_ref, wglu_ref, bglu_ref,
                     ag_ref, sg_ref, wo_ref, g2_ref, x1_ref, h2_ref, y_cols):
    gate1 = mod_ref[0, 2:3, :]
    shift2 = mod_ref[0, 3:4, :]
    scale2 = mod_ref[0, 4:5, :]
    for sub in range(POST_MIX_SUBTILES):
        rows = slice(sub * S5_TILE, (sub + 1) * S5_TILE)
        yc = y_cols.at[:, rows]
        _from_chunk_rows(ys_ref.at[:, 0, sub * S5_SUB:(sub + 1) * S5_SUB], yc)
        y = jnp.concatenate([yc[j] for j in range(SSM_WIDTH // LANES)], axis=1)
        y = y + dsk_ref[...] * u_ref[0, rows]
        y = _gelu_tanh(y)
        z = _dot(y.astype(BF16), wglu_ref[...]) + bglu_ref[...]
        ssm = y * jax.nn.sigmoid(z)
        att = att_ref[0, rows].astype(F32)
        att_n = (att * _rms(att, ATT_WIDTH) * ag_ref[...]).astype(BF16)
        ssm_n = (ssm * _rms(ssm, SSM_WIDTH) * sg_ref[...]).astype(BF16)
        mix = _dot(att_n, wo_ref[0:ATT_WIDTH, :]) + _dot(ssm_n, wo_ref[ATT_WIDTH:2 * ATT_WIDTH, :])
        x1 = x_ref[0, rows] + gate1 * mix
        x1_ref[0, rows] = x1
        h2 = (x1 * _rms(x1, D_MODEL) * g2_ref[...]) * (1.0 + scale2) + shift2
        h2_ref[0, rows] = h2.astype(BF16)


def post_mix(att, ys, u, x, mod, d_skip, w_glu, b_glu, att_g, ssm_g, w_o, g2):
    b, length, d = x.shape
    tm = POST_MIX_SUBTILES * S5_TILE
    assert length % tm == 0
    const = lambda bi, i: (0, 0)
    tile = lambda w: pl.BlockSpec((1, tm, w), lambda bi, i: (bi, i, 0))
    return pl.pallas_call(
        _post_mix_kernel,
        out_shape=(jax.ShapeDtypeStruct((b, length, d), F32),
                   jax.ShapeDtypeStruct((b, length, d), BF16)),
        grid=(b, length // tm),
        in_specs=[
            tile(ATT_WIDTH),
            pl.BlockSpec((N_GROUPS, 1, POST_MIX_SUBTILES * S5_SUB, S5_COLS), lambda bi, i: (0, bi, i, 0)),
            tile(SSM_WIDTH), tile(d),
            pl.BlockSpec((1, 6, d), lambda bi, i: (bi, 0, 0)),
            pl.BlockSpec((1, SSM_WIDTH), const),
            pl.BlockSpec((SSM_WIDTH, SSM_WIDTH), const),
            pl.BlockSpec((1, SSM_WIDTH), const),
            pl.BlockSpec((1, ATT_WIDTH), const),
            pl.BlockSpec((1, SSM_WIDTH), const),
            pl.BlockSpec((d, d), const),
            pl.BlockSpec((1, d), const),
        ],
        out_specs=(tile(d), tile(d)),
        scratch_shapes=[pltpu.VMEM((SSM_WIDTH // LANES, tm, LANES), F32)],
        compiler_params=_params(("parallel", "parallel")),
        name="post_mix",
    )(att, ys, u, x, mod, d_skip, w_glu, b_glu, att_g, ssm_g, w_o, g2)


def _ffn_up_kernel(h_ref, w1_ref, w3_ref, o_ref, w1_bf, w3_bf):
    @pl.when(jnp.logical_and(pl.program_id(1) == 0, pl.program_id(2) == 0))
    def _():
        w1_bf[...] = w1_ref[...].astype(BF16)
        w3_bf[...] = w3_ref[...].astype(BF16)

    h = h_ref[0]
    a = _dot(h, w1_bf[...])
    g = _dot(h, w3_bf[...])
    o_ref[0] = (a * jax.nn.sigmoid(a) * g).astype(BF16)


def ffn_up(h2, w1, w3):
    b, length, d = h2.shape
    tm, tf = min(FFN_UP_TM, length), FFN_UP_TF
    return pl.pallas_call(
        _ffn_up_kernel,
        out_shape=jax.ShapeDtypeStruct((b, length, D_FF), BF16),
        grid=(D_FF // tf, b, length // tm),
        in_specs=[
            pl.BlockSpec((1, tm, d), lambda f, bi, i: (bi, i, 0)),
            pl.BlockSpec((d, tf), lambda f, bi, i: (0, f)),
            pl.BlockSpec((d, tf), lambda f, bi, i: (0, f)),
        ],
        out_specs=pl.BlockSpec((1, tm, tf), lambda f, bi, i: (bi, i, f)),
        scratch_shapes=[pltpu.VMEM((d, tf), BF16), pltpu.VMEM((d, tf), BF16)],
        compiler_params=_params(("parallel", "arbitrary", "arbitrary")),
        name="ffn_up",
    )(h2, w1, w3)


def _ffn_down_kernel(a_ref, x1_ref, mod_ref, w2_ref, o_ref):
    o_ref[0] = x1_ref[0] + mod_ref[0, 5:6, :] * _dot(a_ref[0], w2_ref[...])


def ffn_down(act, x1, mod, w2):
    b, length, d = x1.shape
    tm, tn = min(FFN_DOWN_TM, length), FFN_DOWN_TN
    return pl.pallas_call(
        _ffn_down_kernel,
        out_shape=jax.ShapeDtypeStruct((b, length, d), F32),
        grid=(d // tn, b, length // tm),
        in_specs=[
            pl.BlockSpec((1, tm, D_FF), lambda n, bi, i: (bi, i, 0)),
            pl.BlockSpec((1, tm, tn), lambda n, bi, i: (bi, i, n)),
            pl.BlockSpec((1, 6, tn), lambda n, bi, i: (bi, 0, n)),
            pl.BlockSpec((D_FF, tn), lambda n, bi, i: (0, n)),
        ],
        out_specs=pl.BlockSpec((1, tm, tn), lambda n, bi, i: (bi, i, n)),
        compiler_params=_params(("parallel", "parallel", "parallel")),
        name="ffn_down",
    )(act, x1, mod, w2)


def _cast_kernel(x_ref, o_ref):
    o_ref[...] = x_ref[...].astype(o_ref.dtype)


def cast_bf16(w):
    rows, cols = w.shape
    tr = rows
    while tr * cols * 4 > CAST_BLOCK_BYTES and tr % (4 * SUBLANES) == 0:
        tr //= 2
    return pl.pallas_call(
        _cast_kernel,
        out_shape=jax.ShapeDtypeStruct((rows, cols), BF16),
        grid=(rows // tr,),
        in_specs=[pl.BlockSpec((tr, cols), lambda i: (i, 0))],
        out_specs=pl.BlockSpec((tr, cols), lambda i: (i, 0)),
        compiler_params=_params(("parallel",)),
        name="cast_bf16",
    )(w)


def _swap_halves(a, axis=-1):
    lo, hi = jnp.split(a, 2, axis=axis)
    return jnp.concatenate([hi, lo], axis=axis)


def _prep_w_in_kernel(w_ref, o_ref):
    w = w_ref[...]
    q_cols = N_HEADS * QK_DIM
    half = ROPE_DIM // 2

    def with_swapped(r):
        return [r, r[:, half:], r[:, :half]]

    cols = [w[:, hd * QK_DIM:hd * QK_DIM + NOPE_DIM] for hd in range(N_HEADS)]
    cols += [w[:, hd * QK_DIM + NOPE_DIM:(hd + 1) * QK_DIM] for hd in range(N_HEADS)]
    cols.append(w[:, q_cols:q_cols + KV_RANK])
    cols += with_swapped(w[:, q_cols + KV_RANK:q_cols + KV_RANK + ROPE_DIM])
    cols.append(w[:, q_cols + KV_RANK + ROPE_DIM:])
    o_ref[...] = jnp.concatenate(cols, axis=1).astype(BF16)


def prep_w_in(w_in):
    d, n = w_in.shape
    return pl.pallas_call(
        _prep_w_in_kernel,
        out_shape=jax.ShapeDtypeStruct((d, C_END), BF16),
        grid=(d // W_PREP_ROWS,),
        in_specs=[pl.BlockSpec((W_PREP_ROWS, n), lambda i: (i, 0))],
        out_specs=pl.BlockSpec((W_PREP_ROWS, C_END), lambda i: (i, 0)),
        compiler_params=_params(("parallel",)),
        name="prep_w_in",
    )(w_in)


def prep_w_ukv(w_ukv):
    rank, n = w_ukv.shape
    src = lambda j: (0, 2 * (j % N_HEADS) + j // N_HEADS)
    return pl.pallas_call(
        _cast_kernel,
        out_shape=jax.ShapeDtypeStruct((rank, n), BF16),
        grid=(n // LANES,),
        in_specs=[pl.BlockSpec((rank, LANES), src)],
        out_specs=pl.BlockSpec((rank, LANES), lambda j: (0, j)),
        compiler_params=_params(("parallel",)),
        name="prep_w_ukv",
    )(w_ukv)


def _rope_gain(g):
    gr = g[NOPE_DIM:]
    return jnp.concatenate([gr, _swap_halves(gr)]).reshape(1, 2 * ROPE_DIM)


def _rope_tables(length):
    pos = jnp.arange(length, dtype=F32)
    inv_freq = ROPE_BASE ** (-jnp.arange(0, ROPE_DIM, 2, dtype=F32) / ROPE_DIM)
    ang = pos[:, None] * inv_freq[None, :]
    cos, sin = jnp.cos(ang), jnp.sin(ang)
    return jnp.tile(cos, (1, 4)), jnp.tile(jnp.concatenate([-sin, sin], axis=-1), (1, 2))


def kernel(x_prompt, x_sample, c_prompt, c_sample, w_ada, b_ada, norm_mix_g, w_in, kv_norm_g, w_ukv,
           q_norm_g, k_norm_g, lam_re, lam_im, log_dt, b_re, b_im, c_re, c_im, d_skip, w_glu, b_glu,
           att_out_g, ssm_out_g, w_o, norm_ffn_g, w1, w3, w2):
    assert w_ada.shape[0] == 1, "single-layer kernel"
    xs = (x_prompt, x_sample)
    batches = tuple(x.shape[0] for x in xs)
    length = x_prompt.shape[1]
    assert x_sample.shape[1] == length and sum(batches) <= S5_ROWS

    c_all = jnp.concatenate([c_prompt, c_sample], axis=0)
    c8 = jnp.pad(c_all, ((0, SUBLANES - c_all.shape[0]), (0, 0)))
    mod_all = ada_mod(c8, w_ada[0], b_ada[0]).reshape(SUBLANES, 6, D_MODEL)
    mods = (mod_all[:batches[0]], mod_all[batches[0]:batches[0] + batches[1]])

    w_all = prep_w_in(w_in[0])
    w_ukv_p = prep_w_ukv(w_ukv[0])
    row = lambda a: a.reshape(1, -1)
    q_gn, k_gn = row(q_norm_g[0][:NOPE_DIM]), row(k_norm_g[0][:NOPE_DIM])
    q_gr = jnp.tile(q_norm_g[0][NOPE_DIM:], LANES // ROPE_DIM).reshape(1, LANES)
    k_gr = _rope_gain(k_norm_g[0])
    cos_t, sin_t = _rope_tables(length)
    s5_ops = s5_matrices(lam_re[0], lam_im[0], log_dt[0], b_re[0], b_im[0], c_re[0], c_im[0])
    w_glu_b, w_o_b = cast_bf16(w_glu[0]), cast_bf16(w_o[0])
    w2_b = cast_bf16(w2[0])

    proj = [in_proj(x, m, row(norm_mix_g[0]), w_all, w_ukv_p, row(kv_norm_g[0]), q_gn, k_gn, q_gr, k_gr,
                    cos_t, sin_t) for x, m in zip(xs, mods)]
    atts = [attention(q, k, v) for q, k, v, _, _ in proj]
    y_rows = s5_scan([p[4] for p in proj], *s5_ops)

    outs = []
    for x, m, att, ys, p in zip(xs, mods, atts, y_rows, proj):
        x1, h2 = post_mix(att, ys, p[3], x, m, row(d_skip[0]), w_glu_b, row(b_glu[0]), row(att_out_g[0]),
                          row(ssm_out_g[0]), w_o_b, row(norm_ffn_g[0]))
        outs.append(ffn_down(ffn_up(h2, w1[0], w3[0]), x1, m, w2_b))
    return tuple(outs)
```

```python
import functools
import math

import jax
import jax.numpy as jnp
import numpy as np
from jax import lax
from jax.experimental import pallas as pl
from jax.experimental.pallas import tpu as pltpu

F32 = jnp.float32
BF16 = jnp.bfloat16

D_MODEL = 2048
ATT_WIDTH = 1024
SSM_WIDTH = 1024
N_HEADS = 8
V_DIM = 128
NOPE_DIM = 128
ROPE_DIM = 64
QK_DIM = NOPE_DIM + ROPE_DIM
KV_RANK = 512
ROPE_BASE = 10000.0
SSM_GROUP = 16
N_GROUPS = SSM_WIDTH // SSM_GROUP
SSM_STATE = 64
D_FF = 5632
NORM_EPS = 1e-6

LANES = 128
SUBLANES = 8
VMEM_LIMIT = 56 * 1024 * 1024
CAST_BLOCK_BYTES = 4 * 1024 * 1024

ADA_TN = 1024
ATT_TQ, ATT_TK = 1024, 256
FFN_UP_TM, FFN_UP_TF = 1024, 512
FFN_DOWN_TM, FFN_DOWN_TN = 512, 1024
W_PREP_ROWS = 256

C_QN = 0
C_QR = C_QN + N_HEADS * NOPE_DIM
C_KV = C_QR + N_HEADS * ROPE_DIM
C_KR = C_KV + KV_RANK
C_U = C_KR + 2 * ROPE_DIM
C_END = C_U + SSM_WIDTH

S5_CHUNK = 32
S5_ROWS = SUBLANES
S5_SUB = SUBLANES
S5_TILE = S5_SUB * S5_CHUNK
S5_COLS = S5_CHUNK * SSM_GROUP
S5_STATE_COLS = 4 * SSM_STATE
GROUPS_PER_COL = LANES // SSM_GROUP


def _params(sem):
    return pltpu.CompilerParams(dimension_semantics=sem, vmem_limit_bytes=VMEM_LIMIT)


def _dot(a, b):
    return jnp.dot(a, b, preferred_element_type=F32)


def _dot_f32(a, b):
    a_hi = a.astype(BF16)
    a_lo = (a - a_hi.astype(F32)).astype(BF16)
    b_hi = b.astype(BF16)
    b_lo = (b - b_hi.astype(F32)).astype(BF16)
    return _dot(a_hi, b_hi) + _dot(a_lo, b_hi) + _dot(a_hi, b_lo)


def _ada_kernel(c_ref, w_ref, b_ref, o_ref):
    c = c_ref[...]
    o_ref[...] = _dot_f32(c * jax.nn.sigmoid(c), w_ref[...]) + b_ref[...]


def ada_mod(c8, w_ada, b_ada):
    rows, d = c8.shape
    n = w_ada.shape[1]
    tn = ADA_TN
    return pl.pallas_call(
        _ada_kernel,
        out_shape=jax.ShapeDtypeStruct((rows, n), F32),
        grid=(n // tn,),
        in_specs=[
            pl.BlockSpec((rows, d), lambda j: (0, 0)),
            pl.BlockSpec((d, tn), lambda j: (0, j)),
            pl.BlockSpec((1, tn), lambda j: (0, j)),
        ],
        out_specs=pl.BlockSpec((rows, tn), lambda j: (0, j)),
        compiler_params=_params(("arbitrary",)),
        name="ada_mod",
    )(c8, w_ada, b_ada.reshape(1, n))


def _rms(x, width):
    return lax.rsqrt(jnp.sum(x * x, axis=-1, keepdims=True) * (1.0 / width) + NORM_EPS)


def _granule_masks():
    pos = lax.broadcasted_iota(jnp.int32, (SUBLANES, LANES), 1) // SSM_GROUP
    return [pos == p for p in range(GROUPS_PER_COL)]


def _to_chunk_rows(u_cols, dst):
    masks = _granule_masks()
    for j in range(SSM_WIDTH // LANES):
        rolled = []
        for t in range(S5_CHUNK):
            v = u_cols[j, pl.ds(t, S5_SUB, stride=S5_CHUNK), :]
            r = t % GROUPS_PER_COL
            rolled.append(pltpu.roll(v, SSM_GROUP * r, axis=1) if r else v)
        for k in range(GROUPS_PER_COL):
            for d in range(S5_COLS // LANES):
                acc = rolled[GROUPS_PER_COL * d + GROUPS_PER_COL - 1]
                for rho in range(GROUPS_PER_COL - 2, -1, -1):
                    acc = jnp.where(masks[(k + rho) % GROUPS_PER_COL], rolled[GROUPS_PER_COL * d + rho], acc)
                dst[GROUPS_PER_COL * j + k, :, d * LANES:(d + 1) * LANES] = acc


def _from_chunk_rows(src, y_cols):
    masks = _granule_masks()
    for j in range(SSM_WIDTH // LANES):
        for d in range(S5_COLS // LANES):
            cols = [src[GROUPS_PER_COL * j + k, :, d * LANES:(d + 1) * LANES] for k in range(GROUPS_PER_COL)]
            for rho in range(GROUPS_PER_COL):
                acc = cols[GROUPS_PER_COL - 1]
                for k in range(GROUPS_PER_COL - 2, -1, -1):
                    acc = jnp.where(masks[(k + rho) % GROUPS_PER_COL], cols[k], acc)
                v = pltpu.roll(acc, LANES - SSM_GROUP * rho, axis=1) if rho else acc
                y_cols[j, pl.ds(GROUPS_PER_COL * d + rho, S5_SUB, stride=S5_CHUNK), :] = v


def _in_proj_kernel(x_ref, mod_ref, g1_ref, w_ref, wukv_ref, kvg_ref, qg_ref, kg_ref,
                    qgr_ref, kgr_ref, cos_ref, sin_ref, q_ref, k_ref, v_ref, u_ref, ur_ref, u_cols,
                    *, q_scale):
    x = x_ref[0]
    shift = mod_ref[0, 0:1, :]
    scale = mod_ref[0, 1:2, :]
    h = (x * _rms(x, D_MODEL) * g1_ref[...]) * (1.0 + scale) + shift
    h = h.astype(BF16)

    cos = cos_ref[...]
    sin = sin_ref[...]

    qn = _dot(h, w_ref[:, C_QN:C_QR])
    qr = _dot(h, w_ref[:, C_QR:C_KV])
    qg_n = qg_ref[...]
    qg_r = qgr_ref[...]
    lane = lax.broadcasted_iota(jnp.int32, (x.shape[0], LANES), 1)
    low_half = (lane & (ROPE_DIM - 1)) < ROPE_DIM // 2
    even_head = lane < ROPE_DIM
    for pair in range(N_HEADS // 2):
        v = qr[:, pair * LANES:(pair + 1) * LANES]
        cg = v * qg_r
        swapped = jnp.where(low_half, pltpu.roll(cg, LANES - ROPE_DIM // 2, axis=1),
                            pltpu.roll(cg, ROPE_DIM // 2, axis=1))
        rot = cg * cos + swapped * sin
        sq = v * v
        ss_even = jnp.sum(jnp.where(even_head, sq, 0.0), axis=-1, keepdims=True)
        ss_odd = jnp.sum(sq, axis=-1, keepdims=True) - ss_even
        for odd, ss_r in ((0, ss_even), (1, ss_odd)):
            hd = 2 * pair + odd
            qn_h = qn[:, hd * LANES:(hd + 1) * LANES]
            ss = jnp.sum(qn_h * qn_h, axis=-1, keepdims=True) + ss_r
            r = lax.rsqrt(ss * (1.0 / QK_DIM) + NORM_EPS) * q_scale
            q_ref[0, hd, :, 0:NOPE_DIM] = (qn_h * r * qg_n).astype(BF16)
            rot_h = pltpu.roll(rot, ROPE_DIM, axis=1) if odd else rot
            q_ref[0, hd, :, NOPE_DIM:QK_DIM] = (rot_h * r)[:, 0:ROPE_DIM].astype(BF16)

    ckv = _dot(h, w_ref[:, C_KV:C_KR])
    ckv_n = (ckv * _rms(ckv, KV_RANK) * kvg_ref[...]).astype(BF16)
    kv = _dot(ckv_n, wukv_ref[...])
    kr = _dot(h, w_ref[:, C_KR:C_U])
    kr_ss = 0.5 * jnp.sum(kr * kr, axis=-1, keepdims=True)
    kr_g = kr * kgr_ref[...]
    kr_rot = kr_g * cos + pltpu.roll(kr_g, ROPE_DIM, axis=1) * sin
    kg_n = kg_ref[...]
    ones = jnp.ones((x.shape[0], LANES), BF16)
    for hd in range(N_HEADS):
        kn_h = kv[:, hd * LANES:(hd + 1) * LANES]
        ss = jnp.sum(kn_h * kn_h, axis=-1, keepdims=True) + kr_ss
        r = lax.rsqrt(ss * (1.0 / QK_DIM) + NORM_EPS)
        k_ref[0, hd, :, 0:NOPE_DIM] = (kn_h * r * kg_n).astype(BF16)
        k_ref[0, hd, :, NOPE_DIM:QK_DIM] = (kr_rot * r)[:, 0:ROPE_DIM].astype(BF16)
        v_h = kv[:, ATT_WIDTH + hd * LANES:ATT_WIDTH + (hd + 1) * LANES]
        v_ref[0, hd, :, 0:V_DIM] = v_h.astype(BF16)
        v_ref[0, hd, :, V_DIM:2 * V_DIM] = ones

    u = _dot(h, w_ref[:, C_U:C_END])
    u_ref[0] = u
    for j in range(SSM_WIDTH // LANES):
        u_cols[j] = u[:, j * LANES:(j + 1) * LANES]
    _to_chunk_rows(u_cols, ur_ref.at[:, 0])


def in_proj(x, mod, g1, w_all, w_ukv, kv_g, q_gn, k_gn, q_gr, k_gr, cos_t, sin_t):
    b, length, d = x.shape
    tm = S5_TILE
    assert length % tm == 0
    const = lambda bi, i: (0, 0)
    kern = functools.partial(_in_proj_kernel, q_scale=math.log2(math.e) / math.sqrt(QK_DIM))
    return pl.pallas_call(
        kern,
        out_shape=(
            jax.ShapeDtypeStruct((b, N_HEADS, length, QK_DIM), BF16),
            jax.ShapeDtypeStruct((b, N_HEADS, length, QK_DIM), BF16),
            jax.ShapeDtypeStruct((b, N_HEADS, length, 2 * V_DIM), BF16),
            jax.ShapeDtypeStruct((b, length, SSM_WIDTH), F32),
            jax.ShapeDtypeStruct((N_GROUPS, b, length // S5_CHUNK, S5_COLS), F32),
        ),
        grid=(b, length // tm),
        in_specs=[
            pl.BlockSpec((1, tm, d), lambda bi, i: (bi, i, 0)),
            pl.BlockSpec((1, 6, d), lambda bi, i: (bi, 0, 0)),
            pl.BlockSpec((1, d), const),
            pl.BlockSpec((d, C_END), const),
            pl.BlockSpec((KV_RANK, 2 * ATT_WIDTH), const),
            pl.BlockSpec((1, KV_RANK), const),
            pl.BlockSpec((1, LANES), const),
            pl.BlockSpec((1, LANES), const),
            pl.BlockSpec((1, LANES), const),
            pl.BlockSpec((1, LANES), const),
            pl.BlockSpec((tm, LANES), lambda bi, i: (i, 0)),
            pl.BlockSpec((tm, LANES), lambda bi, i: (i, 0)),
        ],
        out_specs=(
            pl.BlockSpec((1, N_HEADS, tm, QK_DIM), lambda bi, i: (bi, 0, i, 0)),
            pl.BlockSpec((1, N_HEADS, tm, QK_DIM), lambda bi, i: (bi, 0, i, 0)),
            pl.BlockSpec((1, N_HEADS, tm, 2 * V_DIM), lambda bi, i: (bi, 0, i, 0)),
            pl.BlockSpec((1, tm, SSM_WIDTH), lambda bi, i: (bi, i, 0)),
            pl.BlockSpec((N_GROUPS, 1, S5_SUB, S5_COLS), lambda bi, i: (0, bi, i, 0)),
        ),
        scratch_shapes=[pltpu.VMEM((SSM_WIDTH // LANES, tm, LANES), F32)],
        compiler_params=_params(("parallel", "parallel")),
        name="in_proj",
    )(x, mod, g1, w_all, w_ukv, kv_g, q_gn, k_gn, q_gr, k_gr, cos_t, sin_t)


def _attn_kernel(q_ref, k_ref, v_ref, o_ref, *, tk):
    q = q_ref[0, 0]
    tq = q.shape[0]
    m = jnp.full((tq, 1), -jnp.inf, F32)
    acc = jnp.zeros((tq, 2 * V_DIM), F32)
    for j in range(k_ref.shape[2] // tk):
        kc = k_ref[0, 0, j * tk:(j + 1) * tk, :]
        vc = v_ref[0, 0, j * tk:(j + 1) * tk, :]
        s = lax.dot_general(q, kc, (((1,), (1,)), ((), ())), preferred_element_type=F32)
        m_new = jnp.maximum(m, jnp.max(s, axis=-1, keepdims=True))
        alpha = jnp.exp2(m - m_new)
        p = jnp.exp2(s - m_new).astype(BF16)
        acc = alpha * acc + _dot(p, vc)
        m = m_new
    o_ref[0] = (acc[:, 0:V_DIM] / acc[:, V_DIM:2 * V_DIM]).astype(o_ref.dtype)


def attention(q, k, v):
    b, h, length, _ = q.shape
    tq = min(ATT_TQ, length)
    tk = min(ATT_TK, length)
    return pl.pallas_call(
        functools.partial(_attn_kernel, tk=tk),
        out_shape=jax.ShapeDtypeStruct((b, length, h * V_DIM), BF16),
        grid=(b, h, length // tq),
        in_specs=[
            pl.BlockSpec((1, 1, tq, QK_DIM), lambda bi, hi, i: (bi, hi, i, 0)),
            pl.BlockSpec((1, 1, length, QK_DIM), lambda bi, hi, i: (bi, hi, 0, 0)),
            pl.BlockSpec((1, 1, length, 2 * V_DIM), lambda bi, hi, i: (bi, hi, 0, 0)),
        ],
        out_specs=pl.BlockSpec((1, tq, V_DIM), lambda bi, hi, i: (bi, i, hi)),
        compiler_params=_params(("parallel", "parallel", "parallel")),
        name="attention",
    )(q, k, v)


def _s5_steps():
    g = np.arange(N_GROUPS)[:, None] % GROUPS_PER_COL
    slot = np.arange(S5_CHUNK)[None, :]
    return GROUPS_PER_COL * (slot // GROUPS_PER_COL) + (slot % GROUPS_PER_COL - g) % GROUPS_PER_COL


def s5_matrices(lam_re, lam_im, log_dt, b_re, b_im, c_re, c_im):
    t = S5_CHUNK
    dt = jnp.exp(log_dt)[..., None]
    la, lb = lam_re * dt, lam_im * dt
    lb_re, lb_im = jnp.exp(la) * jnp.cos(lb), jnp.exp(la) * jnp.sin(lb)
    den = lam_re * lam_re + lam_im * lam_im
    n_re, n_im = lb_re - 1.0, lb_im
    cf_re = (n_re * lam_re + n_im * lam_im) / den
    cf_im = (n_im * lam_re - n_re * lam_im) / den
    bb_re = cf_re[..., None] * b_re[None] - cf_im[..., None] * b_im[None]
    bb_im = cf_re[..., None] * b_im[None] + cf_im[..., None] * b_re[None]

    steps = _s5_steps().astype(np.float32)

    def power(d, expo):
        e = jnp.asarray(expo)[:, None, :]
        mag = jnp.exp(la[d][..., None] * e)
        return mag * jnp.cos(lb[d][..., None] * e), mag * jnp.sin(lb[d][..., None] * e)

    cat = jnp.concatenate

    def rows_op(pw, xs1, xs2):
        n = len(xs1)
        pr = cat([jnp.swapaxes(p[0], 1, 2) for p in pw], axis=-1)[:, :, None, :]
        pi = cat([jnp.swapaxes(p[1], 1, 2) for p in pw], axis=-1)[:, :, None, :]
        x1 = cat([jnp.swapaxes(x, 1, 2) for x in xs1], axis=-1)[:, None]
        x2 = cat([jnp.swapaxes(x, 1, 2) for x in xs2], axis=-1)[:, None]
        return (pr * x1 + pi * x2).reshape(N_GROUPS, S5_COLS, n * SSM_STATE)

    def cols_op(pw, xs1, xs2):
        pr = jnp.repeat(cat([p[0] for p in pw], axis=1), SSM_GROUP, axis=2)
        pi = jnp.repeat(cat([p[1] for p in pw], axis=1), SSM_GROUP, axis=2)
        x1 = jnp.tile(cat(xs1, axis=1), (1, 1, S5_CHUNK))
        x2 = jnp.tile(cat(xs2, axis=1), (1, 1, S5_CHUNK))
        return pr * x1 + pi * x2

    bf_re, bf_im, bb_re_, bb_im_ = bb_re[0], bb_im[0], bb_re[1], bb_im[1]
    cf_re_, cf_im_, cb_re, cb_im = c_re[0], c_im[0], c_re[1], c_im[1]
    tr = lambda x: jnp.swapaxes(x, 1, 2)
    p_sb, p_sf = power(1, steps), power(0, t - 1 - steps)
    p_of, p_ob = power(0, steps + 1), power(1, t - steps)

    lags = np.broadcast_to(np.arange(t, dtype=np.float32), (N_GROUPS, t))
    p_kf, p_kb = power(0, lags), power(1, t - 1 - lags)
    cl_f = cols_op([p_kf, p_kf], [tr(cf_re_), tr(cf_im_)], [-tr(cf_im_), tr(cf_re_)])
    cl_b = cols_op([p_kb, p_kb], [tr(cb_re), tr(cb_im)], [-tr(cb_im), tr(cb_re)])
    bt_f = cat([tr(bf_re), -tr(bf_im)], axis=-1)
    bt_b = cat([tr(bb_re_), -tr(bb_im_)], axis=-1)
    m_st = rows_op([p_sf, p_sb, p_sf, p_sb], [bf_re, bb_re_, bf_im, bb_im_],
                   [-bf_im, -bb_im_, bf_re, bb_re_]).astype(BF16)
    m_out = cols_op([p_of, p_ob, p_of, p_ob], [tr(cf_re_), tr(cb_re), -tr(cf_im_), -tr(cb_im)],
                    [-tr(cf_im_), -tr(cb_im), -tr(cf_re_), -tr(cb_re)]).astype(BF16)
    a_re = cat([jnp.exp(t * la[0]) * jnp.cos(t * lb[0]), jnp.exp(t * la[1]) * jnp.cos(t * lb[1])], axis=-1)[:, None, :]
    a_im = cat([jnp.exp(t * la[0]) * jnp.sin(t * lb[0]), jnp.exp(t * la[1]) * jnp.sin(t * lb[1])], axis=-1)[:, None, :]
    return bt_f, cl_f, bt_b, cl_b, m_st, m_out, a_re, a_im


def _s5_kernel(*refs, batches):
    n_in = len(batches)
    u_refs = refs[:n_in]
    btf_ref, clf_ref, btb_ref, clb_ref, mst_ref, mout_ref, are_ref, aim_ref = refs[n_in:n_in + 8]
    y_refs = refs[n_in + 8:2 * n_in + 8]
    st_ref, xin_ref, toep_ref = refs[2 * n_in + 8:]
    n_chunks = u_refs[0].shape[2]

    k_f = _dot_f32(btf_ref[0], clf_ref[0])
    k_b = _dot_f32(btb_ref[0], clb_ref[0])
    lane = lax.broadcasted_iota(jnp.int32, (SSM_GROUP, S5_COLS), 1)
    k_id = pl.program_id(0) % GROUPS_PER_COL
    for s in range(S5_CHUNK):
        lo, hi = s * SSM_GROUP, (s + 1) * SSM_GROUP
        fwd = jnp.where(lane >= lo, pltpu.roll(k_f, lo, axis=1) if lo else k_f, 0.0)
        bwd = jnp.where(lane < hi, pltpu.roll(k_b, hi, axis=1) if hi < S5_COLS else k_b, 0.0)
        blk = fwd + bwd
        blk = jnp.concatenate([pltpu.roll(blk[:, c * LANES:(c + 1) * LANES], k_id * SSM_GROUP, axis=1)
                               for c in range(S5_COLS // LANES)], axis=1)
        slot = (s // GROUPS_PER_COL) * GROUPS_PER_COL + ((s % GROUPS_PER_COL + k_id) & (GROUPS_PER_COL - 1))
        toep_ref[pl.ds(pl.multiple_of(slot * SSM_GROUP, SSM_GROUP), SSM_GROUP), :] = blk.astype(BF16)
    toep = toep_ref[...]
    seqs = [(u_ref, y_ref, b) for u_ref, y_ref, nb in zip(u_refs, y_refs, batches) for b in range(nb)]

    def u_rows(u_ref, b):
        return u_ref[0, b].astype(BF16)

    for slot in range(S5_ROWS):
        rows = pl.ds(slot, n_chunks, stride=S5_ROWS)
        if slot < len(seqs):
            st = _dot(u_rows(seqs[slot][0], seqs[slot][2]), mst_ref[0])
        else:
            st = jnp.zeros((n_chunks, S5_STATE_COLS), F32)
        st_ref[0, rows, :] = st[:, 0:LANES]
        st_ref[1, rows, :] = st[:, LANES:2 * LANES]

    a_re = jnp.broadcast_to(are_ref[0], (S5_ROWS, LANES))
    a_im = jnp.broadcast_to(aim_ref[0], (S5_ROWS, LANES))
    is_fwd = lax.broadcasted_iota(jnp.int32, (S5_ROWS, LANES), 1) < SSM_STATE
    half = SSM_STATE

    def body(i, carry):
        x_re, x_im = carry
        rf = pl.multiple_of(i * S5_ROWS, S5_ROWS)
        rb = pl.multiple_of((n_chunks - 1 - i) * S5_ROWS, S5_ROWS)
        xin_ref[0, pl.ds(rf, S5_ROWS), 0:half] = x_re[:, 0:half]
        xin_ref[0, pl.ds(rb, S5_ROWS), half:LANES] = x_re[:, half:LANES]
        xin_ref[1, pl.ds(rf, S5_ROWS), 0:half] = x_im[:, 0:half]
        xin_ref[1, pl.ds(rb, S5_ROWS), half:LANES] = x_im[:, half:LANES]
        s_re = jnp.where(is_fwd, st_ref[0, pl.ds(rf, S5_ROWS), :], st_ref[0, pl.ds(rb, S5_ROWS), :])
        s_im = jnp.where(is_fwd, st_ref[1, pl.ds(rf, S5_ROWS), :], st_ref[1, pl.ds(rb, S5_ROWS), :])
        n_re = a_re * x_re - a_im * x_im + s_re
        n_im = a_re * x_im + a_im * x_re + s_im
        return n_re, n_im

    zero = jnp.zeros((S5_ROWS, LANES), F32)
    lax.fori_loop(0, n_chunks, body, (zero, zero), unroll=4)

    for slot, (u_ref, y_ref, b) in enumerate(seqs):
        rows = pl.ds(slot, n_chunks, stride=S5_ROWS)
        x_in = jnp.concatenate([xin_ref[0, rows, :], xin_ref[1, rows, :]], axis=1).astype(BF16)
        y_ref[0, b] = _dot(u_rows(u_ref, b), toep) + _dot(x_in, mout_ref[0])


def s5_scan(u_rows, bt_f, cl_f, bt_b, cl_b, m_st, m_out, a_re, a_im):
    g, _, n_chunks, _ = u_rows[0].shape
    batches = tuple(u.shape[1] for u in u_rows)
    assert sum(batches) <= S5_ROWS
    gmap = lambda gi: (gi, 0, 0)
    io_specs = [pl.BlockSpec((1, nb, n_chunks, S5_COLS), lambda gi: (gi, 0, 0, 0)) for nb in batches]
    return pl.pallas_call(
        functools.partial(_s5_kernel, batches=batches),
        out_shape=tuple(jax.ShapeDtypeStruct(u.shape, F32) for u in u_rows),
        grid=(g,),
        in_specs=io_specs + [
            pl.BlockSpec((1, SSM_GROUP, 2 * SSM_STATE), gmap),
            pl.BlockSpec((1, 2 * SSM_STATE, S5_COLS), gmap),
            pl.BlockSpec((1, SSM_GROUP, 2 * SSM_STATE), gmap),
            pl.BlockSpec((1, 2 * SSM_STATE, S5_COLS), gmap),
            pl.BlockSpec((1, S5_COLS, S5_STATE_COLS), gmap),
            pl.BlockSpec((1, S5_STATE_COLS, S5_COLS), gmap),
            pl.BlockSpec((1, 1, LANES), gmap),
            pl.BlockSpec((1, 1, LANES), gmap),
        ],
        out_specs=tuple(io_specs),
        scratch_shapes=[pltpu.VMEM((2, n_chunks * S5_ROWS, LANES), F32),
                        pltpu.VMEM((2, n_chunks * S5_ROWS, LANES), F32),
                        pltpu.VMEM((S5_COLS, S5_COLS), BF16)],
        compiler_params=_params(("parallel",)),
        name="s5_scan",
    )(*u_rows, bt_f, cl_f, bt_b, cl_b, m_st, m_out, a_re, a_im)


def _gelu_tanh(x):
    return 0.5 * x * (1.0 + jnp.tanh(math.sqrt(2.0 / math.pi) * (x + 0.044715 * (x * x * x))))


def _post_mix_kernel(att_ref, ys_ref, u_ref, x_ref, mod_ref, dsk_ref, wglu_ref, bglu_ref,
                     ag_ref, sg_ref, wo_ref, g2_ref, x1_ref, h2_ref, y_cols):
    gate1 = mod_ref[0, 2:3, :]
    shift2 = mod_ref[0, 3:4, :]
    scale2 = mod_ref[0, 4:5, :]
    _from_chunk_rows(ys_ref.at[:, 0], y_cols)
    y = jnp.concatenate([y_cols[j] for j in range(SSM_WIDTH // LANES)], axis=1)
    y = y + dsk_ref[...] * u_ref[0]
    y = _gelu_tanh(y)
    z = _dot(y.astype(BF16), wglu_ref[...]) + bglu_ref[...]
    ssm = y * jax.nn.sigmoid(z)
    att = att_ref[0].astype(F32)
    att_n = (att * _rms(att, ATT_WIDTH) * ag_ref[...]).astype(BF16)
    ssm_n = (ssm * _rms(ssm, SSM_WIDTH) * sg_ref[...]).astype(BF16)
    mix = _dot(att_n, wo_ref[0:ATT_WIDTH, :]) + _dot(ssm_n, wo_ref[ATT_WIDTH:2 * ATT_WIDTH, :])
    x1 = x_ref[0] + gate1 * mix
    x1_ref[0] = x1
    h2 = (x1 * _rms(x1, D_MODEL) * g2_ref[...]) * (1.0 + scale2) + shift2
    h2_ref[0] = h2.astype(BF16)


def post_mix(att, ys, u, x, mod, d_skip, w_glu, b_glu, att_g, ssm_g, w_o, g2):
    b, length, d = x.shape
    tm = S5_TILE
    const = lambda bi, i: (0, 0)
    tile = lambda w: pl.BlockSpec((1, tm, w), lambda bi, i: (bi, i, 0))
    return pl.pallas_call(
        _post_mix_kernel,
        out_shape=(jax.ShapeDtypeStruct((b, length, d), F32),
                   jax.ShapeDtypeStruct((b, length, d), BF16)),
        grid=(b, length // tm),
        in_specs=[
            tile(ATT_WIDTH),
            pl.BlockSpec((N_GROUPS, 1, S5_SUB, S5_COLS), lambda bi, i: (0, bi, i, 0)),
            tile(SSM_WIDTH), tile(d),
            pl.BlockSpec((1, 6, d), lambda bi, i: (bi, 0, 0)),
            pl.BlockSpec((1, SSM_WIDTH), const),
            pl.BlockSpec((SSM_WIDTH, SSM_WIDTH), const),
            pl.BlockSpec((1, SSM_WIDTH), const),
            pl.BlockSpec((1, ATT_WIDTH), const),
            pl.BlockSpec((1, SSM_WIDTH), const),
            pl.BlockSpec((d, d), const),
            pl.BlockSpec((1, d), const),
        ],
        out_specs=(tile(d), tile(d)),
        scratch_shapes=[pltpu.VMEM((SSM_WIDTH // LANES, tm, LANES), F32)],
        compiler_params=_params(("parallel", "parallel")),
        name="post_mix",
    )(att, ys, u, x, mod, d_skip, w_glu, b_glu, att_g, ssm_g, w_o, g2)


def _ffn_up_kernel(h_ref, w1_ref, w3_ref, o_ref, w1_bf, w3_bf):
    @pl.when(jnp.logical_and(pl.program_id(1) == 0, pl.program_id(2) == 0))
    def _():
        w1_bf[...] = w1_ref[...].astype(BF16)
        w3_bf[...] = w3_ref[...].astype(BF16)

    h = h_ref[0]
    half = w1_bf.shape[1] // 2
    for c in range(2):
        a = _dot(h, w1_bf[:, c * half:(c + 1) * half])
        g = _dot(h, w3_bf[:, c * half:(c + 1) * half])
        o_ref[0, :, c * half:(c + 1) * half] = (a * jax.nn.sigmoid(a) * g).astype(BF16)


def ffn_up(h2, w1, w3):
    b, length, d = h2.shape
    tm, tf = min(FFN_UP_TM, length), FFN_UP_TF
    return pl.pallas_call(
        _ffn_up_kernel,
        out_shape=jax.ShapeDtypeStruct((b, length, D_FF), BF16),
        grid=(D_FF // tf, b, length // tm),
        in_specs=[
            pl.BlockSpec((1, tm, d), lambda f, bi, i: (bi, i, 0)),
            pl.BlockSpec((d, tf), lambda f, bi, i: (0, f)),
            pl.BlockSpec((d, tf), lambda f, bi, i: (0, f)),
        ],
        out_specs=pl.BlockSpec((1, tm, tf), lambda f, bi, i: (bi, i, f)),
        scratch_shapes=[pltpu.VMEM((d, tf), BF16), pltpu.VMEM((d, tf), BF16)],
        compiler_params=_params(("parallel", "arbitrary", "arbitrary")),
        name="ffn_up",
    )(h2, w1, w3)


def _ffn_down_kernel(a_ref, x1_ref, mod_ref, w2_ref, o_ref):
    o_ref[0] = x1_ref[0] + mod_ref[0, 5:6, :] * _dot(a_ref[0], w2_ref[...])


def ffn_down(act, x1, mod, w2):
    b, length, d = x1.shape
    tm, tn = min(FFN_DOWN_TM, length), FFN_DOWN_TN
    return pl.pallas_call(
        _ffn_down_kernel,
        out_shape=jax.ShapeDtypeStruct((b, length, d), F32),
        grid=(d // tn, b, length // tm),
        in_specs=[
            pl.BlockSpec((1, tm, D_FF), lambda n, bi, i: (bi, i, 0)),
            pl.BlockSpec((1, tm, tn), lambda n, bi, i: (bi, i, n)),
            pl.BlockSpec((1, 6, tn), lambda n, bi, i: (bi, 0, n)),
            pl.BlockSpec((D_FF, tn), lambda n, bi, i: (0, n)),
        ],
        out_specs=pl.BlockSpec((1, tm, tn), lambda n, bi, i: (bi, i, n)),
        compiler_params=_params(("parallel", "parallel", "parallel")),
        name="ffn_down",
    )(act, x1, mod, w2)


def _cast_kernel(x_ref, o_ref):
    o_ref[...] = x_ref[...].astype(o_ref.dtype)


def cast_bf16(w):
    rows, cols = w.shape
    tr = rows
    while tr * cols * 4 > CAST_BLOCK_BYTES and tr % (4 * SUBLANES) == 0:
        tr //= 2
    return pl.pallas_call(
        _cast_kernel,
        out_shape=jax.ShapeDtypeStruct((rows, cols), BF16),
        grid=(rows // tr,),
        in_specs=[pl.BlockSpec((tr, cols), lambda i: (i, 0))],
        out_specs=pl.BlockSpec((tr, cols), lambda i: (i, 0)),
        compiler_params=_params(("parallel",)),
        name="cast_bf16",
    )(w)


def _swap_halves(a, axis=-1):
    lo, hi = jnp.split(a, 2, axis=axis)
    return jnp.concatenate([hi, lo], axis=axis)


def _prep_w_in_kernel(w_ref, o_ref):
    w = w_ref[...]
    q_cols = N_HEADS * QK_DIM
    half = ROPE_DIM // 2

    def with_swapped(r):
        return [r, r[:, half:], r[:, :half]]

    cols = [w[:, hd * QK_DIM:hd * QK_DIM + NOPE_DIM] for hd in range(N_HEADS)]
    cols += [w[:, hd * QK_DIM + NOPE_DIM:(hd + 1) * QK_DIM] for hd in range(N_HEADS)]
    cols.append(w[:, q_cols:q_cols + KV_RANK])
    cols += with_swapped(w[:, q_cols + KV_RANK:q_cols + KV_RANK + ROPE_DIM])
    cols.append(w[:, q_cols + KV_RANK + ROPE_DIM:])
    o_ref[...] = jnp.concatenate(cols, axis=1).astype(BF16)


def prep_w_in(w_in):
    d, n = w_in.shape
    return pl.pallas_call(
        _prep_w_in_kernel,
        out_shape=jax.ShapeDtypeStruct((d, C_END), BF16),
        grid=(d // W_PREP_ROWS,),
        in_specs=[pl.BlockSpec((W_PREP_ROWS, n), lambda i: (i, 0))],
        out_specs=pl.BlockSpec((W_PREP_ROWS, C_END), lambda i: (i, 0)),
        compiler_params=_params(("parallel",)),
        name="prep_w_in",
    )(w_in)


def prep_w_ukv(w_ukv):
    rank, n = w_ukv.shape
    src = lambda j: (0, 2 * (j % N_HEADS) + j // N_HEADS)
    return pl.pallas_call(
        _cast_kernel,
        out_shape=jax.ShapeDtypeStruct((rank, n), BF16),
        grid=(n // LANES,),
        in_specs=[pl.BlockSpec((rank, LANES), src)],
        out_specs=pl.BlockSpec((rank, LANES), lambda j: (0, j)),
        compiler_params=_params(("parallel",)),
        name="prep_w_ukv",
    )(w_ukv)


def _rope_gain(g):
    gr = g[NOPE_DIM:]
    return jnp.concatenate([gr, _swap_halves(gr)]).reshape(1, 2 * ROPE_DIM)


def _rope_tables(length):
    pos = jnp.arange(length, dtype=F32)
    inv_freq = ROPE_BASE ** (-jnp.arange(0, ROPE_DIM, 2, dtype=F32) / ROPE_DIM)
    ang = pos[:, None] * inv_freq[None, :]
    cos, sin = jnp.cos(ang), jnp.sin(ang)
    return jnp.tile(cos, (1, 4)), jnp.tile(jnp.concatenate([-sin, sin], axis=-1), (1, 2))


def kernel(x_prompt, x_sample, c_prompt, c_sample, w_ada, b_ada, norm_mix_g, w_in, kv_norm_g, w_ukv,
           q_norm_g, k_norm_g, lam_re, lam_im, log_dt, b_re, b_im, c_re, c_im, d_skip, w_glu, b_glu,
           att_out_g, ssm_out_g, w_o, norm_ffn_g, w1, w3, w2):
    assert w_ada.shape[0] == 1, "single-layer kernel"
    xs = (x_prompt, x_sample)
    batches = tuple(x.shape[0] for x in xs)
    length = x_prompt.shape[1]
    assert x_sample.shape[1] == length and sum(batches) <= S5_ROWS

    c_all = jnp.concatenate([c_prompt, c_sample], axis=0)
    c8 = jnp.pad(c_all, ((0, SUBLANES - c_all.shape[0]), (0, 0)))
    mod_all = ada_mod(c8, w_ada[0], b_ada[0]).reshape(SUBLANES, 6, D_MODEL)
    mods = (mod_all[:batches[0]], mod_all[batches[0]:batches[0] + batches[1]])

    w_all = prep_w_in(w_in[0])
    w_ukv_p = prep_w_ukv(w_ukv[0])
    row = lambda a: a.reshape(1, -1)
    q_gn, k_gn = row(q_norm_g[0][:NOPE_DIM]), row(k_norm_g[0][:NOPE_DIM])
    q_gr = jnp.tile(q_norm_g[0][NOPE_DIM:], LANES // ROPE_DIM).reshape(1, LANES)
    k_gr = _rope_gain(k_norm_g[0])
    cos_t, sin_t = _rope_tables(length)
    s5_ops = s5_matrices(lam_re[0], lam_im[0], log_dt[0], b_re[0], b_im[0], c_re[0], c_im[0])
    w_glu_b, w_o_b = cast_bf16(w_glu[0]), cast_bf16(w_o[0])
    w2_b = cast_bf16(w2[0])

    proj = [in_proj(x, m, row(norm_mix_g[0]), w_all, w_ukv_p, row(kv_norm_g[0]), q_gn, k_gn, q_gr, k_gr,
                    cos_t, sin_t) for x, m in zip(xs, mods)]
    atts = [attention(q, k, v) for q, k, v, _, _ in proj]
    y_rows = s5_scan([p[4] for p in proj], *s5_ops)

    outs = []
    for x, m, att, ys, p in zip(xs, mods, atts, y_rows, proj):
        x1, h2 = post_mix(att, ys, p[3], x, m, row(d_skip[0]), w_glu_b, row(b_glu[0]), row(att_out_g[0]),
                          row(ssm_out_g[0]), w_o_b, row(norm_ffn_g[0]))
        outs.append(ffn_down(ffn_up(h2, w1[0], w3[0]), x1, m, w2_b))
    return tuple(outs)
```
